```python
import math
import jax, jax.numpy as jnp
from jax import lax
import numpy as np

D_MODEL = 1024
BATCH = 8
SEQ = 2048
DEPTH = 4

GRID_W = 64
CTX_LEN = 256
EPS = 1e-6

HY_WIDTH = 512
HY_ORDER = 2
HY_SHORT = 3
HY_BANDS = 16
HY_EMB = 1 + 2 * HY_BANDS
HY_FFN = 64
HY_DECAY_TARGET = 1e-2
HY_FAST_PCT = 0.3
HY_SLOW_PCT = 1.5
HY_MAX_DECAY = math.log(HY_DECAY_TARGET) / HY_FAST_PCT
HY_MIN_DECAY = math.log(HY_DECAY_TARGET) / HY_SLOW_PCT

DA_HEADS = 4
DA_HEAD_DIM = 64
DA_V_DIM = 2 * DA_HEAD_DIM
DA_WIDTH = DA_HEADS * DA_V_DIM
ROPE_BASE = 10000.0
Q_BLOCK = 128

SG_WIDTH = 512
SG_GROUPS = 4
SG_CHUNK = 128
SG_GROUP_CH = SG_WIDTH // SG_GROUPS

N_BRANCH = 3
BR_WIDTH = HY_WIDTH

D_FF = 2816
N_EXPERTS = 8
TOP_K = 2
D_FF_EXPERT = 3584

HY_COLS = (HY_ORDER + 1) * HY_WIDTH
DA_QK_COLS = DA_HEADS * 2 * DA_HEAD_DIM
DA_COLS = 2 * DA_QK_COLS + DA_WIDTH
SG_COLS = 2 * SG_WIDTH
GATE_COLS = N_BRANCH * D_MODEL
DA_OFF = HY_COLS
SG_OFF = DA_OFF + DA_COLS
GATE_OFF = SG_OFF + SG_COLS
IN_COLS = GATE_OFF + GATE_COLS

kernel_name = 'hybrid_hyena_diffattn_gmlp_moe_dit'


def rmsnorm(x, g):
    xf = x.astype(jnp.float32)
    y = xf * lax.rsqrt(jnp.mean(xf * xf, axis=-1, keepdims=True) + EPS)
    return (y * g.astype(jnp.float32)).astype(x.dtype)


def modulate(x, g, shift, scale):
    return rmsnorm(x, g) * (1 + scale) + shift


def short_conv(x, w, b):
    L = x.shape[1]
    pad = HY_SHORT // 2
    xp = jnp.pad(x, ((0, 0), (pad, pad), (0, 0)))
    y = b
    for j in range(HY_SHORT):
        y = y + w[j] * xp[:, j:j + L]
    return y


def hyena_filters(L, w1, b1, w2, b2, w3, freq):
    t = jnp.linspace(0.0, 1.0, L, dtype=jnp.float32)[:, None]
    pos = jnp.arange(L, dtype=jnp.float32)[:, None]
    bands = jnp.linspace(1e-4, HY_BANDS - 1, HY_BANDS, dtype=jnp.float32)[None]
    ang = 2.0 * math.pi * pos * bands / L
    z = jnp.concatenate([t, jnp.cos(ang), jnp.sin(ang)], axis=-1)
    fr = freq.astype(jnp.float32)
    h = jnp.sin(fr * (z @ w1.astype(jnp.float32) + b1.astype(jnp.float32)))
    h = jnp.sin(fr * (h @ w2.astype(jnp.float32) + b2.astype(jnp.float32)))
    h = (h @ w3.astype(jnp.float32)).reshape(L, 2, HY_ORDER, HY_WIDTH)
    deltas = jnp.abs(jnp.linspace(HY_MIN_DECAY, HY_MAX_DECAY, HY_WIDTH, dtype=jnp.float32))
    window = jnp.exp(-t * deltas)
    h = h * window[:, None, None, :]
    fwd, bwd = h[:, 0], h[:, 1]
    k = jnp.concatenate([fwd[:1] + bwd[:1], fwd[1:],
                         jnp.zeros((1, HY_ORDER, HY_WIDTH), jnp.float32),
                         bwd[1:][::-1]], axis=0)
    return k / jnp.sum(jnp.abs(k), axis=0, keepdims=True)


def hyena_branch(p, conv_w, conv_b, w1, b1, w2, b2, w3, freq, skip):
    L = p.shape[1]
    u = short_conv(p, conv_w, conv_b)
    z = u[..., :HY_WIDTH]
    gates = [u[..., (o + 1) * HY_WIDTH:(o + 2) * HY_WIDTH] for o in range(HY_ORDER)]
    kf = jnp.fft.rfft(hyena_filters(L, w1, b1, w2, b2, w3, freq), axis=0)
    for o in range(HY_ORDER):
        zf = jnp.fft.rfft(z.astype(jnp.float32), n=2 * L, axis=1)
        conv = jnp.fft.irfft(zf * kf[None, :, o], n=2 * L, axis=1)[:, :L]
        z = gates[o] * (conv.astype(z.dtype) + skip[o] * z)
    return z


def axial_rope_tables(n_lat):
    rows = n_lat // GRID_W
    row = jnp.repeat(jnp.arange(rows), GRID_W).astype(jnp.float32)
    col = jnp.tile(jnp.arange(GRID_W), rows).astype(jnp.float32)
    half = DA_HEAD_DIM // 2
    inv = ROPE_BASE ** (-jnp.arange(0, half, 2, dtype=jnp.float32) / half)
    ang = jnp.stack([row[:, None] * inv, col[:, None] * inv], axis=1)
    return jnp.cos(ang), jnp.sin(ang)


def rope_rotate(x, cos, sin):
    h = x.shape[-1] // 2
    x1, x2 = x[..., :h], x[..., h:]
    return jnp.concatenate([x1 * cos - x2 * sin, x1 * sin + x2 * cos], axis=-1)


def apply_axial_rope(x, cos, sin):
    cos = cos.astype(x.dtype)[None, :, None, None]
    sin = sin.astype(x.dtype)[None, :, None, None]
    half = DA_HEAD_DIM // 2
    xr = rope_rotate(x[..., :half], cos[..., 0, :], sin[..., 0, :])
    xc = rope_rotate(x[..., half:], cos[..., 1, :], sin[..., 1, :])
    return jnp.concatenate([xr, xc], axis=-1)


def attn_heads(p, q_g, k_g):
    B, L = p.shape[:2]
    q = p[..., :DA_QK_COLS].reshape(B, L, DA_HEADS, 2, DA_HEAD_DIM)
    k = p[..., DA_QK_COLS:2 * DA_QK_COLS].reshape(B, L, DA_HEADS, 2, DA_HEAD_DIM)
    v = p[..., 2 * DA_QK_COLS:].reshape(B, L, DA_HEADS, DA_V_DIM)
    return rmsnorm(q, q_g), rmsnorm(k, k_g), v


def diff_lambda(lq1, lk1, lq2, lk2, lam_init):
    f = jnp.float32
    return (jnp.exp(jnp.sum(lq1.astype(f) * lk1.astype(f))) -
            jnp.exp(jnp.sum(lq2.astype(f) * lk2.astype(f))) + lam_init)


def diff_attend(q, k, v, lam):
    s = jnp.einsum('bqhcd,bkhcd->bhcqk', q, k).astype(jnp.float32) * (DA_HEAD_DIM ** -0.5)
    p = jax.nn.softmax(s, axis=-1)
    w = p[:, :, 0] - lam * p[:, :, 1]
    return jnp.einsum('bhqk,bkhe->bqhe', w.astype(v.dtype), v)


def blocked_diff_attention(q, k_all, v_all, lam):
    B, S = q.shape[:2]
    nb = S // Q_BLOCK
    qb = q.reshape(B, nb, Q_BLOCK, DA_HEADS, 2, DA_HEAD_DIM).swapaxes(0, 1)
    o = lax.map(lambda qq: diff_attend(qq, k_all, v_all, lam), qb)
    return o.swapaxes(0, 1).reshape(B, S, DA_HEADS, DA_V_DIM)


def diff_head_out(o, subln_g, lam_init):
    B, L = o.shape[:2]
    return (rmsnorm(o, subln_g) * (1.0 - lam_init)).reshape(B, L, DA_WIDTH)


def spatial_gating(p, norm_g, w_s, b_s):
    B, L, _ = p.shape
    z = jax.nn.gelu(p)
    u, v = z[..., :SG_WIDTH], z[..., SG_WIDTH:]
    v = rmsnorm(v, norm_g).reshape(B, L // SG_CHUNK, SG_CHUNK, SG_GROUPS, SG_GROUP_CH)
    s = jnp.einsum('gpq,bnqgc->bnpgc', w_s, v) + b_s.T[:, :, None]
    return u * s.reshape(B, L, SG_WIDTH)


def merge_branches(p_gate, y_hy, y_da, y_sg, w_branch, w_out):
    B, L, _ = p_gate.shape
    ys = jnp.stack([y_hy, y_da, y_sg], axis=2)
    yb = jnp.einsum('blnw,nwd->blnd', ys, w_branch)
    g = jax.nn.sigmoid(p_gate.reshape(B, L, N_BRANCH, D_MODEL))
    return jnp.sum(g * yb, axis=2) @ w_out


def swiglu(h, wg, wu, wd):
    return (jax.nn.silu(h @ wg) * (h @ wu)) @ wd


def moe_swiglu(h, w_router, wg, wu, wd):
    logits = (h @ w_router).astype(jnp.float32)
    top_v, top_i = lax.top_k(logits, TOP_K)
    top_w = jax.nn.softmax(top_v, axis=-1)
    combine = jnp.sum(jax.nn.one_hot(top_i, N_EXPERTS, dtype=jnp.float32) * top_w[..., None], axis=-2)
    out = jnp.zeros_like(h)
    for e in range(N_EXPERTS):
        out = out + combine[..., e:e + 1].astype(h.dtype) * swiglu(h, wg[e], wu[e], wd[e])
    return out


def setup_inputs(seed: int = 0) -> dict:
    key = jax.random.key(seed)
    ks = iter(jax.random.split(key, 64))
    f32 = jnp.float32
    D = D_MODEL
    n_dense = (DEPTH + 1) // 2
    n_moe = DEPTH // 2

    def nrm(shape, scale):
        return jax.random.normal(next(ks), shape, f32) * scale

    return {
        'x': nrm((BATCH, SEQ, D), 1.0),
        'c': nrm((BATCH, D), 1.0),
        'ctx': nrm((BATCH, CTX_LEN, D), 1.0),
        'c_ctx': nrm((D,), 1.0),
        'w_mod': nrm((DEPTH, D, 6 * D), 0.5 * D ** -0.5),
        'b_mod': nrm((DEPTH, 6 * D), 0.02),
        'norm1_g': 1.0 + nrm((DEPTH, D), 0.02),
        'norm2_g': 1.0 + nrm((DEPTH, D), 0.02),
        'w_in': nrm((DEPTH, D, IN_COLS), D ** -0.5),
        'hy_conv_w': nrm((DEPTH, HY_SHORT, HY_COLS), HY_SHORT ** -0.5),
        'hy_conv_b': nrm((DEPTH, HY_COLS), 0.02),
        'hy_f_w1': nrm((DEPTH, HY_EMB, HY_FFN), HY_EMB ** -0.5),
        'hy_f_b1': nrm((DEPTH, HY_FFN), 0.02),
        'hy_f_w2': nrm((DEPTH, HY_FFN, HY_FFN), HY_FFN ** -0.5),
        'hy_f_b2': nrm((DEPTH, HY_FFN), 0.02),
        'hy_f_w3': nrm((DEPTH, HY_FFN, 2 * HY_ORDER * HY_WIDTH), HY_FFN ** -0.5),
        'hy_f_freq': 1.0 + nrm((DEPTH, HY_FFN), 0.02),
        'hy_skip': nrm((DEPTH, HY_ORDER, HY_WIDTH), 0.5),
        'q_norm_g': 1.0 + nrm((DEPTH, DA_HEAD_DIM), 0.02),
        'k_norm_g': 1.0 + nrm((DEPTH, DA_HEAD_DIM), 0.02),
        'lam_q1': nrm((DEPTH, DA_HEAD_DIM), 0.1),
        'lam_k1': nrm((DEPTH, DA_HEAD_DIM), 0.1),
        'lam_q2': nrm((DEPTH, DA_HEAD_DIM), 0.1),
        'lam_k2': nrm((DEPTH, DA_HEAD_DIM), 0.1),
        'subln_g': 1.0 + nrm((DEPTH, DA_V_DIM), 0.02),
        'sg_norm_g': 1.0 + nrm((DEPTH, SG_WIDTH), 0.02),
        'sg_w': nrm((DEPTH, SG_GROUPS, SG_CHUNK, SG_CHUNK), SG_CHUNK ** -0.5),
        'sg_b': 1.0 + nrm((DEPTH, SG_GROUPS, SG_CHUNK), 0.02),
        'w_branch': nrm((DEPTH, N_BRANCH, BR_WIDTH, D), BR_WIDTH ** -0.5),
        'w_out': nrm((DEPTH, D, D), D ** -0.5),
        'ffn_wg': nrm((n_dense, D, D_FF), D ** -0.5),
        'ffn_wu': nrm((n_dense, D, D_FF), D ** -0.5),
        'ffn_wd': nrm((n_dense, D_FF, D), D_FF ** -0.5),
        'router_w': nrm((n_moe, D, N_EXPERTS), D ** -0.5),
        'moe_wg': nrm((n_moe, N_EXPERTS, D, D_FF_EXPERT), D ** -0.5),
        'moe_wu': nrm((n_moe, N_EXPERTS, D, D_FF_EXPERT), D ** -0.5),
        'moe_wd': nrm((n_moe, N_EXPERTS, D_FF_EXPERT, D), D_FF_EXPERT ** -0.5),
    }


def reference(x, c, ctx, c_ctx, w_mod, b_mod, norm1_g, norm2_g, w_in,
              hy_conv_w, hy_conv_b, hy_f_w1, hy_f_b1, hy_f_w2, hy_f_b2, hy_f_w3, hy_f_freq, hy_skip,
              q_norm_g, k_norm_g, lam_q1, lam_k1, lam_q2, lam_k2, subln_g,
              sg_norm_g, sg_w, sg_b, w_branch, w_out,
              ffn_wg, ffn_wu, ffn_wd, router_w, moe_wg, moe_wu, moe_wd):
    n_lat = x.shape[1]
    cos, sin = axial_rope_tables(n_lat)
    xl, xc = x, ctx
    cond_l = jax.nn.silu(c)
    cond_c = jax.nn.silu(c_ctx)
    for i in range(DEPTH):
        last = i == DEPTH - 1
        m_l = (cond_l @ w_mod[i] + b_mod[i]).reshape(-1, 1, 6, D_MODEL)
        m_c = (cond_c @ w_mod[i] + b_mod[i]).reshape(6, D_MODEL)
        sh1_l, sc1_l, g1_l, sh2_l, sc2_l, g2_l = [m_l[:, :, j] for j in range(6)]
        sh1_c, sc1_c, g1_c, sh2_c, sc2_c, g2_c = [m_c[j] for j in range(6)]

        hl = modulate(xl, norm1_g[i], sh1_l, sc1_l)
        hc = modulate(xc, norm1_g[i], sh1_c, sc1_c)
        pl = hl @ w_in[i]
        pc = hc @ w_in[i]

        lam_init = 0.8 - 0.6 * math.exp(-0.3 * i)
        lam = diff_lambda(lam_q1[i], lam_k1[i], lam_q2[i], lam_k2[i], lam_init)
        ql, kl, vl = attn_heads(pl[..., DA_OFF:SG_OFF], q_norm_g[i], k_norm_g[i])
        qc, kc, vc = attn_heads(pc[..., DA_OFF:SG_OFF], q_norm_g[i], k_norm_g[i])
        ql = apply_axial_rope(ql, cos, sin)
        kl = apply_axial_rope(kl, cos, sin)
        k_all = jnp.concatenate([kl, kc], axis=1)
        v_all = jnp.concatenate([vl, vc], axis=1)
        y_da_l = diff_head_out(blocked_diff_attention(ql, k_all, v_all, lam), subln_g[i], lam_init)
        y_hy_l = hyena_branch(pl[..., :DA_OFF], hy_conv_w[i], hy_conv_b[i], hy_f_w1[i], hy_f_b1[i],
                              hy_f_w2[i], hy_f_b2[i], hy_f_w3[i], hy_f_freq[i], hy_skip[i])
        y_sg_l = spatial_gating(pl[..., SG_OFF:GATE_OFF], sg_norm_g[i], sg_w[i], sg_b[i])
        mix_l = merge_branches(pl[..., GATE_OFF:], y_hy_l, y_da_l, y_sg_l, w_branch[i], w_out[i])

        if not last:
            y_da_c = diff_head_out(diff_attend(qc, kc, vc, lam), subln_g[i], lam_init)
            y_hy_c = hyena_branch(pc[..., :DA_OFF], hy_conv_w[i], hy_conv_b[i], hy_f_w1[i], hy_f_b1[i],
                                  hy_f_w2[i], hy_f_b2[i], hy_f_w3[i], hy_f_freq[i], hy_skip[i])
            y_sg_c = spatial_gating(pc[..., SG_OFF:GATE_OFF], sg_norm_g[i], sg_w[i], sg_b[i])
            mix_c = merge_branches(pc[..., GATE_OFF:], y_hy_c, y_da_c, y_sg_c, w_branch[i], w_out[i])
            xc = xc + g1_c * mix_c
        xl = xl + g1_l * mix_l

        def ffn(h):
            j = i // 2
            if i % 2 == 0:
                return swiglu(h, ffn_wg[j], ffn_wu[j], ffn_wd[j])
            return moe_swiglu(h, router_w[j], moe_wg[j], moe_wu[j], moe_wd[j])

        xl = xl + g2_l * ffn(modulate(xl, norm2_g[i], sh2_l, sc2_l))
        if not last:
            xc = xc + g2_c * ffn(modulate(xc, norm2_g[i], sh2_c, sc2_c))
    return xl
```

```python
import functools
import math

import jax
import jax.numpy as jnp
from jax import lax
from jax.experimental import pallas as pl
from jax.experimental.pallas import tpu as pltpu

F32 = jnp.float32
BF16 = jnp.bfloat16
EPS = 1e-6

GRID_W = 64
HY_WIDTH = 512
HY_ORDER = 2
HY_BANDS = 16
HY_FFN = 64
HY_MAX_DECAY = math.log(1e-2) / 0.3
HY_MIN_DECAY = math.log(1e-2) / 1.5
DA_HEADS = 4
DA_HEAD_DIM = 64
DA_V_DIM = 2 * DA_HEAD_DIM
ROPE_BASE = 10000.0
SG_WIDTH = 512
SG_GROUPS = 4
SG_CHUNK = 128
N_EXPERTS = 8
HY_COLS = 3 * HY_WIDTH
QK_COLS = DA_HEADS * 2 * DA_HEAD_DIM
DA_COLS = 2 * QK_COLS + DA_HEADS * DA_V_DIM
SG_COLS = 2 * SG_WIDTH
PRE_COLS = HY_COLS + DA_COLS + SG_COLS

LANES = 128
TOK_TILE = 256
CONV_TILE = 256
HY_CHANNELS_PER_STEP = 8
ROW_TILE = 512
FF_TILE = 512
MOD_ROWS = 16
VMEM_LIMIT = 56 * 1024 * 1024


def _dot(a, b):
    return jnp.dot(a, b, preferred_element_type=F32)


def _dot_nt(a, b):
    return lax.dot_general(a, b, (((1,), (1,)), ((), ())), preferred_element_type=F32)


_hdot = functools.partial(jnp.dot, precision=lax.Precision.HIGHEST, preferred_element_type=F32)


def _split_bf16(a):
    hi = a.astype(BF16)
    return hi, (a - hi.astype(F32)).astype(BF16)


def _norm_mod(x, g, shift, scale):
    ms = jnp.mean(x * x, axis=-1, keepdims=True)
    return (x * lax.rsqrt(ms + EPS) * g) * (1.0 + scale) + shift


def _params(*sem):
    return pltpu.CompilerParams(dimension_semantics=sem, vmem_limit_bytes=VMEM_LIMIT)


def _modvec_body(cond_ref, w_ref, b_ref, o_ref):
    cnd = cond_ref[...]
    s = cnd * jax.nn.sigmoid(cnd)
    s_hi, s_lo = _split_bf16(s)
    w_hi, w_lo = _split_bf16(w_ref[0])
    o_ref[0] = _dot(s_hi, w_hi) + _dot(s_lo, w_hi) + _dot(s_hi, w_lo) + b_ref[0]


def _modvec(cond, w_mod, b_mod):
    depth, d, n = w_mod.shape
    tn = n // 4
    return pl.pallas_call(
        _modvec_body,
        grid=(depth, n // tn),
        in_specs=[
            pl.BlockSpec((MOD_ROWS, d), lambda i, j: (0, 0)),
            pl.BlockSpec((1, d, tn), lambda i, j: (i, 0, j)),
            pl.BlockSpec((1, 1, tn), lambda i, j: (i, 0, j)),
        ],
        out_specs=pl.BlockSpec((1, MOD_ROWS, tn), lambda i, j: (i, 0, j)),
        out_shape=jax.ShapeDtypeStruct((depth, MOD_ROWS, n), F32),
        compiler_params=_params("parallel", "parallel"),
        name="modvec",
    )(cond, w_mod, b_mod.reshape(depth, 1, n))


def _tok_spec(width, tm):
    return pl.BlockSpec((1, tm, width), lambda b, j: (b, j, 0))


def _const_spec(shape):
    nd = len(shape)
    return pl.BlockSpec(shape, lambda b, j: (0,) * nd)


def _mod_spec(d, n_lat_tiles, n_samples):
    return pl.BlockSpec((1, 6, d), lambda b, j: (jnp.where(j < n_lat_tiles, b, n_samples), 0, 0))


def _qk_norm_rope(a, g_tile, gmat, cos, sin_signed, first_half):
    hi, lo = _split_bf16(a * a)
    ss = _dot(hi, gmat) + _dot(lo, gmat)
    an = a * lax.rsqrt(ss * (1.0 / DA_HEAD_DIM) + EPS) * g_tile
    outs = []
    for ci in range(QK_COLS // LANES):
        ch = an[:, ci * LANES:(ci + 1) * LANES]
        partner = jnp.where(first_half, pltpu.roll(ch, LANES - 16, 1), pltpu.roll(ch, 16, 1))
        outs.append(ch * cos + partner * sin_signed)
    return jnp.concatenate(outs, axis=1)


def _inproj_body(x_ref, mod_ref, g1_ref, w_ref, cos_ref, sin_ref, qg_ref, kg_ref, gmat_ref,
                 sgg_ref, sgw_ref, sgb_ref, hy_ref, q_ref, k_ref, v_ref, sg_ref):
    tm = x_ref.shape[1]
    h = _norm_mod(x_ref[0], g1_ref[...], mod_ref[0, 0:1, :], mod_ref[0, 1:2, :]).astype(BF16)
    hy_ref[0] = _dot(h, w_ref[:, 0:HY_COLS])

    lane = lax.broadcasted_iota(jnp.int32, (1, LANES), 1)
    first_half = (lane % 32) < 16
    cos = cos_ref[...]
    sin_signed = sin_ref[...]
    gmat = gmat_ref[...]
    o = HY_COLS
    q = _qk_norm_rope(_dot(h, w_ref[:, o:o + QK_COLS]), qg_ref[...], gmat, cos, sin_signed, first_half)
    q_ref[0] = (q * (DA_HEAD_DIM ** -0.5)).astype(BF16)
    o += QK_COLS
    k = _qk_norm_rope(_dot(h, w_ref[:, o:o + QK_COLS]), kg_ref[...], gmat, cos, sin_signed, first_half)
    k_ref[0] = k.astype(BF16)
    o += QK_COLS
    v_ref[0] = _dot(h, w_ref[:, o:o + DA_HEADS * DA_V_DIM]).astype(BF16)
    o += DA_HEADS * DA_V_DIM

    z = jax.nn.gelu(_dot(h, w_ref[:, o:o + SG_COLS]))
    u = z[:, :SG_WIDTH]
    vv = z[:, SG_WIDTH:]
    vn = (vv * lax.rsqrt(jnp.mean(vv * vv, axis=-1, keepdims=True) + EPS) * sgg_ref[...]).astype(BF16)
    gw = SG_WIDTH // SG_GROUPS
    for ch in range(tm // SG_CHUNK):
        r0 = ch * SG_CHUNK
        cols = []
        for g in range(SG_GROUPS):
            s = _dot(sgw_ref[g], vn[r0:r0 + SG_CHUNK, g * gw:(g + 1) * gw]) + sgb_ref[:, g * gw:(g + 1) * gw]
            cols.append(u[r0:r0 + SG_CHUNK, g * gw:(g + 1) * gw] * s)
        sg_ref[0, r0:r0 + SG_CHUNK, :] = jnp.concatenate(cols, axis=1).astype(BF16)


def _inproj(xa, mod_i, g1, w_pre, cos_t, sin_t, qg, kg, gmat, sgg, sgw, sgb, n_lat_tiles, n_samples):
    b, t, d = xa.shape
    tm = TOK_TILE
    outs = [jax.ShapeDtypeStruct((b, t, HY_COLS), F32)] + [jax.ShapeDtypeStruct((b, t, QK_COLS), BF16)] * 4
    return pl.pallas_call(
        _inproj_body,
        grid=(b, t // tm),
        in_specs=[
            _tok_spec(d, tm),
            _mod_spec(d, n_lat_tiles, n_samples),
            _const_spec((1, d)),
            _const_spec((d, PRE_COLS)),
            pl.BlockSpec((tm, LANES), lambda bi, j: (j, 0)),
            pl.BlockSpec((tm, LANES), lambda bi, j: (j, 0)),
            _const_spec((1, QK_COLS)),
            _const_spec((1, QK_COLS)),
            _const_spec((QK_COLS, QK_COLS)),
            _const_spec((1, SG_WIDTH)),
            _const_spec((SG_GROUPS, SG_CHUNK, SG_CHUNK)),
            _const_spec((SG_CHUNK, SG_WIDTH)),
        ],
        out_specs=[_tok_spec(HY_COLS, tm)] + [_tok_spec(QK_COLS, tm)] * 4,
        out_shape=outs,
        compiler_params=_params("parallel", "parallel"),
        name="inproj",
    )(xa, mod_i, g1, w_pre, cos_t, sin_t, qg, kg, gmat, sgg, sgw, sgb)


def _hyfilter_body(w1t_ref, w1c_ref, w1s_ref, b1_ref, w2_ref, b2_ref, fr_ref, w3f_ref, w3b_ref,
                   bands_ref, dl_ref, o_ref, *, seq):
    n = 2 * seq
    xi = lax.broadcasted_iota(jnp.int32, (1, n), 1)
    lag = xi - (seq - 1)
    pos = jnp.abs(lag).astype(F32)
    t = pos / (seq - 1)
    ang = 2.0 * math.pi * pos * bands_ref[...] / seq
    fr = fr_ref[...]
    z1 = w1t_ref[...] * t + _hdot(w1c_ref[...], jnp.cos(ang)) + _hdot(w1s_ref[...], jnp.sin(ang)) + b1_ref[...]
    h1 = jnp.sin(fr * z1)
    h2 = jnp.sin(fr * (_hdot(w2_ref[...], h1) + b2_ref[...]))
    fwd = _hdot(w3f_ref[...], h2)
    bwd = _hdot(w3b_ref[...], h2)
    window = jnp.exp(-t * dl_ref[...])
    k = jnp.where(lag > 0, fwd, jnp.where(lag < 0, bwd, fwd + bwd)) * window
    k = jnp.where(xi < n - 1, k, 0.0)
    o_ref[...] = k / jnp.sum(jnp.abs(k), axis=-1, keepdims=True)


def _hyfilter(fw, seq):
    rows = HY_ORDER * HY_WIDTH
    rb = 256
    n = 2 * seq
    small = lambda shape: pl.BlockSpec(shape, lambda i: (0, 0))
    return pl.pallas_call(
        functools.partial(_hyfilter_body, seq=seq),
        grid=(rows // rb,),
        in_specs=[
            small((HY_FFN, 1)), small((HY_FFN, HY_BANDS)), small((HY_FFN, HY_BANDS)), small((HY_FFN, 1)),
            small((HY_FFN, HY_FFN)), small((HY_FFN, 1)), small((HY_FFN, 1)),
            pl.BlockSpec((rb, HY_FFN), lambda i: (i, 0)),
            pl.BlockSpec((rb, HY_FFN), lambda i: (i, 0)),
            small((HY_BANDS, 1)),
            pl.BlockSpec((rb, 1), lambda i: (i, 0)),
        ],
        out_specs=pl.BlockSpec((rb, n), lambda i: (i, 0)),
        out_shape=jax.ShapeDtypeStruct((rows, n), F32),
        compiler_params=_params("parallel"),
        name="hyfilter",
    )(*fw)


def _hyconv_body(sc_ref, z_ref, x1_ref, x2_ref, hml_ref, hmc_ref, o_ref, big_l, big_c, *, seq, ctx, cb):
    tot = seq + ctx
    lane = lax.broadcasted_iota(jnp.int32, (1, tot), 1)
    has_prev = jnp.logical_and(lane != 0, lane != seq)
    has_next = jnp.logical_and(lane != seq - 1, lane != tot - 1)
    row = lax.broadcasted_iota(jnp.int32, (LANES, LANES), 0)
    col = lax.broadcasted_iota(jnp.int32, (LANES, LANES), 1)
    lower = col <= row
    c0 = pl.program_id(0) * cb

    def short_conv(p, base, part):
        prev = jnp.where(has_prev, pltpu.roll(p, 1, 1), 0.0)
        nxt = jnp.where(has_next, pltpu.roll(p, tot - 1, 1), 0.0)
        return (sc_ref[base + 9 + part] + sc_ref[base + part] * prev
                + sc_ref[base + 3 + part] * p + sc_ref[base + 6 + part] * nxt)

    def build(hm_ref, o, ci, big_ref):
        n_rows = hm_ref.shape[2]
        prev = None
        for rp in range(n_rows - 1, -1, -1):
            r = hm_ref[o, ci, rp:rp + 1, :]
            cur = pltpu.roll(jnp.broadcast_to(r, (LANES, LANES)), 1, 1, stride=1, stride_axis=0)
            if prev is not None:
                rho = n_rows - 2 - rp
                big_ref[rho * LANES:(rho + 1) * LANES, :] = jnp.where(lower, cur, prev).astype(BF16)
            prev = cur

    def long_conv(zz, big_ref, length, base):
        nblk = length // CONV_TILE
        mid = (2 * nblk - 1) * LANES
        ys = [None] * nblk
        for dd in range(-(nblk - 1), nblk):
            r0 = mid - CONV_TILE * dd
            w = jnp.concatenate([big_ref[r0:r0 + CONV_TILE, :], big_ref[r0 - LANES:r0 - LANES + CONV_TILE, :]], axis=1)
            js = list(range(max(0, -dd), min(nblk, nblk - dd)))
            lhs = jnp.concatenate([zz[:, base + j * CONV_TILE:base + (j + 1) * CONV_TILE] for j in js], axis=0)
            out = _dot(lhs.astype(BF16), w)
            nb = zz.shape[0]
            for kk, j in enumerate(js):
                piece = out[kk * nb:(kk + 1) * nb]
                ys[j + dd] = piece if ys[j + dd] is None else ys[j + dd] + piece
        return ys

    def chan(ci, carry):
        base = (c0 + ci) * 16
        z = short_conv(z_ref[ci], base, 0)
        gates = (short_conv(x1_ref[ci], base, 1), short_conv(x2_ref[ci], base, 2))
        for o in range(HY_ORDER):
            build(hml_ref, o, ci, big_l)
            build(hmc_ref, o, ci, big_c)
            y = jnp.concatenate(long_conv(z, big_l, seq, 0) + long_conv(z, big_c, ctx, seq), axis=1)
            z = gates[o] * (y + sc_ref[base + 12 + o] * z)
        o_ref[ci] = z
        return carry

    lax.fori_loop(0, cb, chan, 0)


def _hyconv(sc_tab, u3, hm_l, hm_c, seq, ctx):
    _, b, tot = u3.shape
    cb = HY_CHANNELS_PER_STEP
    nblk = HY_WIDTH // cb
    slab = lambda part: pl.BlockSpec((cb, b, tot), lambda i: (part * nblk + i, 0, 0))
    return pl.pallas_call(
        functools.partial(_hyconv_body, seq=seq, ctx=ctx, cb=cb),
        grid=(nblk,),
        in_specs=[
            pl.BlockSpec(memory_space=pltpu.SMEM),
            slab(0), slab(1), slab(2),
            pl.BlockSpec((HY_ORDER, cb, hm_l.shape[2], LANES), lambda i: (0, i, 0, 0)),
            pl.BlockSpec((HY_ORDER, cb, hm_c.shape[2], LANES), lambda i: (0, i, 0, 0)),
        ],
        out_specs=pl.BlockSpec((cb, b, tot), lambda i: (i, 0, 0)),
        out_shape=jax.ShapeDtypeStruct((HY_WIDTH, b, tot), F32),
        scratch_shapes=[
            pltpu.VMEM(((hm_l.shape[2] - 1) * LANES, LANES), BF16),
            pltpu.VMEM(((hm_c.shape[2] - 1) * LANES, LANES), BF16),
        ],
        compiler_params=_params("parallel"),
        name="hyconv",
    )(sc_tab, u3, u3, u3, hm_l, hm_c)


def _attn_body(lam_ref, q_ref, k_ref, v_ref, sub_ref, o_ref, *, lam_init, seq):
    tq = q_ref.shape[1]
    tot = k_ref.shape[1]
    q = q_ref[0]
    k = k_ref[0]
    lane = lax.broadcasted_iota(jnp.int32, (1, 2 * DA_HEAD_DIM), 1)
    first = lane < DA_HEAD_DIM
    zero = jnp.zeros_like(q)
    s1 = _dot_nt(jnp.where(first, q, zero), k)
    s2 = _dot_nt(jnp.where(first, zero, q), k)
    kidx = lax.broadcasted_iota(jnp.int32, (1, tot), 1)
    visible = jnp.logical_or(kidx >= seq, pl.program_id(2) * tq < seq)
    s1 = jnp.where(visible, s1, -1e30)
    s2 = jnp.where(visible, s2, -1e30)
    p1 = jnp.exp(s1 - jnp.max(s1, axis=-1, keepdims=True))
    p2 = jnp.exp(s2 - jnp.max(s2, axis=-1, keepdims=True))
    lp = lam_ref[...]
    lam = (jnp.exp(jnp.sum(lp[0:1] * lp[1:2], keepdims=True))
           - jnp.exp(jnp.sum(lp[2:3] * lp[3:4], keepdims=True)) + lam_init)
    w = p1 * (1.0 / jnp.sum(p1, axis=-1, keepdims=True)) - p2 * (lam / jnp.sum(p2, axis=-1, keepdims=True))
    o = _dot(w.astype(BF16), v_ref[0])
    on = o * lax.rsqrt(jnp.mean(o * o, axis=-1, keepdims=True) + EPS) * sub_ref[...]
    o_ref[0] = (on * (1.0 - lam_init)).astype(BF16)


def _attention(lam_p, q, k, v, sub, lam_init, seq):
    b, tot, _ = q.shape
    tq = TOK_TILE
    hw = 2 * DA_HEAD_DIM
    return pl.pallas_call(
        functools.partial(_attn_body, lam_init=lam_init, seq=seq),
        grid=(b, DA_HEADS, tot // tq),
        in_specs=[
            pl.BlockSpec((4, DA_HEAD_DIM), lambda bi, h, j: (0, 0)),
            pl.BlockSpec((1, tq, hw), lambda bi, h, j: (bi, j, h)),
            pl.BlockSpec((1, tot, hw), lambda bi, h, j: (bi, 0, h)),
            pl.BlockSpec((1, tot, hw), lambda bi, h, j: (bi, 0, h)),
            pl.BlockSpec((1, hw), lambda bi, h, j: (0, 0)),
        ],
        out_specs=pl.BlockSpec((1, tq, hw), lambda bi, h, j: (bi, j, h)),
        out_shape=jax.ShapeDtypeStruct((b, tot, DA_HEADS * DA_V_DIM), BF16),
        compiler_params=_params("parallel", "parallel", "parallel"),
        name="diffattn",
    )(lam_p, q, k, v, sub)


def _merge_body(x_ref, mod_ref, g1_ref, wg_ref, yh_ref, yd_ref, ys_ref, wb_ref, wo_ref, o_ref):
    x = x_ref[0]
    d = x.shape[1]
    h = _norm_mod(x, g1_ref[...], mod_ref[0, 0:1, :], mod_ref[0, 1:2, :]).astype(BF16)
    acc = None
    for n, y_ref in enumerate((yh_ref, yd_ref, ys_ref)):
        gate = jax.nn.sigmoid(_dot(h, wg_ref[:, n * d:(n + 1) * d]))
        term = gate * _dot(y_ref[0], wb_ref[n])
        acc = term if acc is None else acc + term
    o_ref[0] = x + mod_ref[0, 2:3, :] * _dot(acc.astype(BF16), wo_ref[...])


def _merge(xa, mod_i, g1, w_gate, y_hy, y_da, y_sg, w_br, w_out, n_lat_tiles, n_samples):
    b, t, d = xa.shape
    tm = TOK_TILE
    return pl.pallas_call(
        _merge_body,
        grid=(b, t // tm),
        in_specs=[
            _tok_spec(d, tm), _mod_spec(d, n_lat_tiles, n_samples), _const_spec((1, d)),
            _const_spec((d, 3 * d)),
            _tok_spec(HY_WIDTH, tm), _tok_spec(HY_WIDTH, tm), _tok_spec(HY_WIDTH, tm),
            _const_spec((3, HY_WIDTH, d)), _const_spec((d, d)),
        ],
        out_specs=_tok_spec(d, tm),
        out_shape=jax.ShapeDtypeStruct((b, t, d), F32),
        compiler_params=_params("parallel", "parallel"),
        name="merge",
    )(xa, mod_i, g1, w_gate, y_hy, y_da, y_sg, w_br, w_out)


def _ffn_body(x_ref, mod_ref, g2_ref, wg_ref, wu_ref, wd_ref, o_ref, *, chunk):
    x = x_ref[0]
    h = _norm_mod(x, g2_ref[...], mod_ref[0, 3:4, :], mod_ref[0, 4:5, :]).astype(BF16)
    ff = wg_ref.shape[1]
    acc = None
    for f0 in range(0, ff, chunk):
        f1 = min(ff, f0 + chunk)
        a = _dot(h, wg_ref[:, f0:f1])
        mid = (a * jax.nn.sigmoid(a) * _dot(h, wu_ref[:, f0:f1])).astype(BF16)
        term = _dot(mid, wd_ref[f0:f1, :])
        acc = term if acc is None else acc + term
    o_ref[0] = x + mod_ref[0, 5:6, :] * acc


def _ffn_dense(xa, mod_i, g2, wg, wu, wd, n_lat_tiles, n_samples):
    b, t, d = xa.shape
    ff = wg.shape[1]
    tm = TOK_TILE
    return pl.pallas_call(
        functools.partial(_ffn_body, chunk=1024),
        grid=(b, t // tm),
        in_specs=[
            _tok_spec(d, tm), _mod_spec(d, n_lat_tiles, n_samples), _const_spec((1, d)),
            _const_spec((d, ff)), _const_spec((d, ff)), _const_spec((ff, d)),
        ],
        out_specs=_tok_spec(d, tm),
        out_shape=jax.ShapeDtypeStruct((b, t, d), F32),
        compiler_params=_params("parallel", "parallel"),
        name="ffn_dense",
    )(xa, mod_i, g2, wg, wu, wd)


def _router_body(x_ref, mod_ref, g2_ref, wr_ref, h_ref, r_ref):
    h = _norm_mod(x_ref[0], g2_ref[...], mod_ref[0, 3:4, :], mod_ref[0, 4:5, :])
    h_ref[0] = h
    h_hi, h_lo = _split_bf16(h)
    w_hi, w_lo = _split_bf16(wr_ref[...])
    logits = _dot_nt(w_hi, h_hi) + _dot_nt(w_hi, h_lo) + _dot_nt(w_lo, h_hi)
    eid = lax.broadcasted_iota(jnp.int32, logits.shape, 0)
    m1 = jnp.max(logits, axis=0, keepdims=True)
    i1 = jnp.min(jnp.where(logits == m1, eid, N_EXPERTS), axis=0, keepdims=True)
    rest = jnp.where(eid == i1, -jnp.inf, logits)
    m2 = jnp.max(rest, axis=0, keepdims=True)
    i2 = jnp.min(jnp.where(rest == m2, eid, N_EXPERTS), axis=0, keepdims=True)
    w1 = 1.0 / (1.0 + jnp.exp(m2 - m1))
    rows = lax.broadcasted_iota(jnp.int32, logits.shape, 0)
    out = jnp.where(rows == 0, i1.astype(F32), jnp.where(rows == 1, i2.astype(F32),
                    jnp.where(rows == 2, w1, jnp.where(rows == 3, 1.0 - w1, 0.0))))
    r_ref[0] = out


def _router(xa, mod_i, g2, wr_t, n_lat_tiles, n_samples):
    b, t, d = xa.shape
    tm = TOK_TILE
    return pl.pallas_call(
        _router_body,
        grid=(b, t // tm),
        in_specs=[
            _tok_spec(d, tm), _mod_spec(d, n_lat_tiles, n_samples), _const_spec((1, d)),
            _const_spec((N_EXPERTS, d)),
        ],
        out_specs=[_tok_spec(d, tm), pl.BlockSpec((1, N_EXPERTS, tm), lambda bi, j: (bi, 0, j))],
        out_shape=[jax.ShapeDtypeStruct((b, t, d), F32), jax.ShapeDtypeStruct((b, N_EXPERTS, t), F32)],
        compiler_params=_params("parallel", "parallel"),
        name="router",
    )(xa, mod_i, g2, wr_t)


def _row_copy(src_hbm, dst, sem, src_row, dst_row):
    return pltpu.make_async_copy(src_hbm.at[pl.ds(src_row, 1), :], dst.at[pl.ds(dst_row, 1), :], sem)


def _expert_body(te_ref, nu_ref, rt_ref, h_hbm, rw_ref, wg_ref, wu_ref, wd_ref, o_ref, xbuf, xbf, acc, sem):
    t = pl.program_id(0)
    f = pl.program_id(1)
    rows = xbuf.shape[0]
    used = t < nu_ref[0]

    @pl.when(jnp.logical_and(used, f == 0))
    def _gather():
        def start(r, c):
            _row_copy(h_hbm, xbuf, sem, rt_ref[0, 0, r], r).start()
            return c
        lax.fori_loop(0, rows, start, 0)

        def wait(r, c):
            _row_copy(h_hbm, xbuf, sem, 0, r).wait()
            return c
        lax.fori_loop(0, rows, wait, 0)
        xbf[...] = xbuf[...].astype(BF16)
        acc[...] = jnp.zeros_like(acc)

    @pl.when(used)
    def _compute():
        xb = xbf[...]
        a = _dot(xb, wg_ref[0])
        mid = (a * jax.nn.sigmoid(a) * _dot(xb, wu_ref[0])).astype(BF16)
        acc[...] += _dot(mid, wd_ref[0])

    @pl.when(f == pl.num_programs(1) - 1)
    def _store():
        o_ref[...] = jnp.where(used, acc[...] * rw_ref[...], 0.0)


def _experts(tile_expert, n_used, row_token, h_flat, row_w, wg, wu, wd):
    n_rows = row_w.shape[0]
    d = h_flat.shape[1]
    ff = wg.shape[2]
    nt = n_rows // ROW_TILE
    grid_spec = pltpu.PrefetchScalarGridSpec(
        num_scalar_prefetch=2,
        grid=(nt, ff // FF_TILE),
        in_specs=[
            pl.BlockSpec((1, 1, ROW_TILE), lambda t, f, te, nu: (t, 0, 0), memory_space=pltpu.SMEM),
            pl.BlockSpec(memory_space=pl.ANY),
            pl.BlockSpec((ROW_TILE, 1), lambda t, f, te, nu: (t, 0)),
            pl.BlockSpec((1, d, FF_TILE), lambda t, f, te, nu: (te[t], 0, f)),
            pl.BlockSpec((1, d, FF_TILE), lambda t, f, te, nu: (te[t], 0, f)),
            pl.BlockSpec((1, FF_TILE, d), lambda t, f, te, nu: (te[t], f, 0)),
        ],
        out_specs=pl.BlockSpec((ROW_TILE, d), lambda t, f, te, nu: (t, 0)),
        scratch_shapes=[
            pltpu.VMEM((ROW_TILE, d), F32),
            pltpu.VMEM((ROW_TILE, d), BF16),
            pltpu.VMEM((ROW_TILE, d), F32),
            pltpu.SemaphoreType.DMA,
        ],
    )
    return pl.pallas_call(
        _expert_body,
        grid_spec=grid_spec,
        out_shape=jax.ShapeDtypeStruct((n_rows, d), F32),
        compiler_params=_params("arbitrary", "arbitrary"),
        name="experts",
    )(tile_expert, n_used, row_token.reshape(nt, 1, ROW_TILE), h_flat, row_w, wg, wu, wd)


def _combine_body(pos_ref, x_ref, mod_ref, ye_hbm, o_ref, buf, sem):
    tm = x_ref.shape[1]

    def start(r, c):
        _row_copy(ye_hbm, buf.at[0], sem, pos_ref[0, 0, r], r).start()
        _row_copy(ye_hbm, buf.at[1], sem, pos_ref[0, 1, r], r).start()
        return c
    lax.fori_loop(0, tm, start, 0)

    def wait(r, c):
        _row_copy(ye_hbm, buf.at[0], sem, 0, r).wait()
        _row_copy(ye_hbm, buf.at[1], sem, 0, r).wait()
        return c
    lax.fori_loop(0, tm, wait, 0)
    o_ref[0] = x_ref[0] + mod_ref[0, 5:6, :] * (buf[0] + buf[1])


def _combine(pos, xa, mod_i, ye, n_lat_tiles, n_samples):
    b, t, d = xa.shape
    tm = TOK_TILE
    nj = t // tm
    return pl.pallas_call(
        _combine_body,
        grid=(b, nj),
        in_specs=[
            pl.BlockSpec((1, 2, tm), lambda bi, j: (bi * nj + j, 0, 0), memory_space=pltpu.SMEM),
            _tok_spec(d, tm), _mod_spec(d, n_lat_tiles, n_samples),
            pl.BlockSpec(memory_space=pl.ANY),
        ],
        out_specs=_tok_spec(d, tm),
        out_shape=jax.ShapeDtypeStruct((b, t, d), F32),
        scratch_shapes=[pltpu.VMEM((2, tm, d), F32), pltpu.SemaphoreType.DMA],
        compiler_params=_params("arbitrary", "arbitrary"),
        name="moe_combine",
    )(pos, xa, mod_i, ye)


def _moe(xa, mod_i, g2, wr_t, wg, wu, wd, n_lat_tiles, n_samples):
    b, t, d = xa.shape
    h, route = _router(xa, mod_i, g2, wr_t, n_lat_tiles, n_samples)
    n_tok = b * t
    expert = route[:, 0:2, :].astype(jnp.int32)
    weight = route[:, 2:4, :]
    onehot = (expert.reshape(-1)[:, None] == jnp.arange(N_EXPERTS)[None, :]).astype(jnp.int32)
    counts = jnp.sum(onehot, axis=0)
    rank = jnp.sum((jnp.cumsum(onehot, axis=0) - onehot) * onehot, axis=1)
    padded = ((counts + ROW_TILE - 1) // ROW_TILE) * ROW_TILE
    ends = jnp.cumsum(padded)
    pos = ((ends - padded)[expert.reshape(-1)] + rank).reshape(b, 2, t)
    n_rows = ((2 * n_tok + N_EXPERTS * (ROW_TILE - 1)) // ROW_TILE + 1) * ROW_TILE
    token = jnp.broadcast_to(jnp.arange(b)[:, None, None] * t + jnp.arange(t)[None, None, :], (b, 2, t))
    row_token = jnp.zeros((n_rows,), jnp.int32).at[pos.reshape(-1)].set(token.reshape(-1))
    row_w = jnp.zeros((n_rows,), F32).at[pos.reshape(-1)].set(weight.reshape(-1)).reshape(n_rows, 1)
    tile_start = jnp.arange(n_rows // ROW_TILE) * ROW_TILE
    tile_expert = jnp.minimum(jnp.sum(tile_start[:, None] >= ends[None, :], axis=1), N_EXPERTS - 1).astype(jnp.int32)
    n_used = (ends[-1:] // ROW_TILE).astype(jnp.int32)
    ye = _experts(tile_expert, n_used, row_token, h.reshape(n_tok, d), row_w, wg, wu, wd)
    nj = t // TOK_TILE
    pos_tiles = pos.reshape(b, 2, nj, TOK_TILE).transpose(0, 2, 1, 3).reshape(b * nj, 2, TOK_TILE)
    return _combine(pos_tiles, xa, mod_i, ye, n_lat_tiles, n_samples)


def _rope_tables(seq, ctx):
    rows = seq // GRID_W
    row = jnp.repeat(jnp.arange(rows), GRID_W).astype(F32)
    col = jnp.tile(jnp.arange(GRID_W), rows).astype(F32)
    half = DA_HEAD_DIM // 2
    inv = ROPE_BASE ** (-jnp.arange(0, half, 2, dtype=F32) / half)
    ang = jnp.concatenate([row[:, None] * inv, row[:, None] * inv, col[:, None] * inv, col[:, None] * inv], axis=1)
    sign = jnp.tile(jnp.concatenate([-jnp.ones((half // 2,), F32), jnp.ones((half // 2,), F32)]), 2)
    cos = jnp.concatenate([jnp.cos(ang), jnp.ones((ctx, DA_HEAD_DIM), F32)], axis=0)
    sin = jnp.concatenate([jnp.sin(ang) * sign, jnp.zeros((ctx, DA_HEAD_DIM), F32)], axis=0)
    return jnp.tile(cos, (1, LANES // DA_HEAD_DIM)), jnp.tile(sin, (1, LANES // DA_HEAD_DIM))


def _filter_weights(w1, b1, w2, b2, w3, freq):
    col = lambda v: v.reshape(-1, 1)
    bands = jnp.linspace(1e-4, HY_BANDS - 1, HY_BANDS, dtype=F32)
    deltas = jnp.abs(jnp.linspace(HY_MIN_DECAY, HY_MAX_DECAY, HY_WIDTH, dtype=F32))
    half = HY_ORDER * HY_WIDTH
    return (w1[0:1].T, w1[1:1 + HY_BANDS].T, w1[1 + HY_BANDS:].T, col(b1), w2.T, col(b2), col(freq),
            w3[:, :half].T, w3[:, half:].T, col(bands), col(jnp.tile(deltas, HY_ORDER)))


def kernel(x, c, ctx, c_ctx, w_mod, b_mod, norm1_g, norm2_g, w_in, hy_conv_w, hy_conv_b, hy_f_w1, hy_f_b1, hy_f_w2, hy_f_b2, hy_f_w3, hy_f_freq, hy_skip, q_norm_g, k_norm_g, lam_q1, lam_k1, lam_q2, lam_k2, subln_g, sg_norm_g, sg_w, sg_b, w_branch, w_out, ffn_wg, ffn_wu, ffn_wd, router_w, moe_wg, moe_wu, moe_wd):
    n_samples, seq, d = x.shape
    n_ctx = ctx.shape[1]
    depth = w_mod.shape[0]
    assert seq % CONV_TILE == 0 and n_ctx % CONV_TILE == 0 and seq % GRID_W == 0
    assert n_samples + 1 <= MOD_ROWS
    n_lat_tiles = seq // TOK_TILE
    tot = seq + n_ctx

    xa = jnp.concatenate([x, ctx], axis=1)
    cond = jnp.zeros((MOD_ROWS, d), F32).at[:n_samples].set(c).at[n_samples].set(c_ctx)
    mod = _modvec(cond, w_mod, b_mod).reshape(depth, MOD_ROWS, 6, d)

    cos_t, sin_t = _rope_tables(seq, n_ctx)
    group = jnp.arange(QK_COLS) // DA_HEAD_DIM
    gmat = (group[:, None] == group[None, :]).astype(BF16)
    w_in_b = w_in.astype(BF16)
    w_br_b = w_branch.astype(BF16)
    w_out_b = w_out.astype(BF16)

    for i in range(depth):
        lam_init = 0.8 - 0.6 * math.exp(-0.3 * i)
        g1 = norm1_g[i].reshape(1, d)
        g2 = norm2_g[i].reshape(1, d)
        hy, q, k, v, y_sg = _inproj(
            xa, mod[i], g1, w_in_b[i, :, :PRE_COLS], cos_t, sin_t,
            jnp.tile(q_norm_g[i], QK_COLS // DA_HEAD_DIM).reshape(1, QK_COLS),
            jnp.tile(k_norm_g[i], QK_COLS // DA_HEAD_DIM).reshape(1, QK_COLS),
            gmat, sg_norm_g[i].reshape(1, SG_WIDTH), sg_w[i].astype(BF16),
            jnp.repeat(sg_b[i].T, SG_WIDTH // SG_GROUPS, axis=1), n_lat_tiles, n_samples)

        fw = _filter_weights(hy_f_w1[i], hy_f_b1[i], hy_f_w2[i], hy_f_b2[i], hy_f_w3[i], hy_f_freq[i])
        hm_l = _hyfilter(fw, seq).reshape(HY_ORDER, HY_WIDTH, 2 * seq // LANES, LANES)
        hm_c = _hyfilter(fw, n_ctx).reshape(HY_ORDER, HY_WIDTH, 2 * n_ctx // LANES, LANES)
        cw = hy_conv_w[i].reshape(3, 3, HY_WIDTH)
        cbias = hy_conv_b[i].reshape(3, HY_WIDTH)
        sc_tab = jnp.concatenate([cw.reshape(9, HY_WIDTH), cbias, hy_skip[i], jnp.zeros((2, HY_WIDTH), F32)],
                                 axis=0).T.reshape(-1)
        y_hy = _hyconv(sc_tab, jnp.transpose(hy, (2, 0, 1)), hm_l, hm_c, seq, n_ctx)
        y_hy = jnp.transpose(y_hy, (1, 2, 0)).astype(BF16)

        lam_p = jnp.stack([lam_q1[i], lam_k1[i], lam_q2[i], lam_k2[i]], axis=0)
        y_da = _attention(lam_p, q, k, v, subln_g[i].reshape(1, DA_V_DIM), lam_init, seq)

        xa = _merge(xa, mod[i], g1, w_in_b[i, :, PRE_COLS:], y_hy, y_da, y_sg, w_br_b[i], w_out_b[i],
                    n_lat_tiles, n_samples)

        j = i // 2
        if i % 2 == 0:
            xa = _ffn_dense(xa, mod[i], g2, ffn_wg[j].astype(BF16), ffn_wu[j].astype(BF16),
                            ffn_wd[j].astype(BF16), n_lat_tiles, n_samples)
        else:
            xa = _moe(xa, mod[i], g2, router_w[j].T, moe_wg[j].astype(BF16), moe_wu[j].astype(BF16),
                      moe_wd[j].astype(BF16), n_lat_tiles, n_samples)
    return xa[:, :seq]
```

```python
import functools
import math

import jax
import jax.numpy as jnp
from jax import lax
from jax.experimental import pallas as pl
from jax.experimental.pallas import tpu as pltpu

F32 = jnp.float32
BF16 = jnp.bfloat16
EPS = 1e-6

GRID_W = 64
HY_WIDTH = 512
HY_ORDER = 2
HY_BANDS = 16
HY_FFN = 64
HY_MAX_DECAY = math.log(1e-2) / 0.3
HY_MIN_DECAY = math.log(1e-2) / 1.5
DA_HEADS = 4
DA_HEAD_DIM = 64
DA_V_DIM = 2 * DA_HEAD_DIM
ROPE_BASE = 10000.0
SG_WIDTH = 512
SG_GROUPS = 4
SG_CHUNK = 128
N_EXPERTS = 8
HY_COLS = 3 * HY_WIDTH
QK_COLS = DA_HEADS * 2 * DA_HEAD_DIM
DA_COLS = 2 * QK_COLS + DA_HEADS * DA_V_DIM
SG_COLS = 2 * SG_WIDTH
PRE_COLS = HY_COLS + DA_COLS + SG_COLS

LANES = 128
TOK_TILE = 256
ATTN_ROWS = 256
CONV_TILE = 256
HY_CHANNELS_PER_STEP = 8
ROW_TILE = 512
FF_TILE = 512
DMA_UNROLL = 8
MOD_ROWS = 16
VMEM_LIMIT = 56 * 1024 * 1024


def _dot(a, b):
    return jnp.dot(a, b, preferred_element_type=F32)


def _dot_nt(a, b):
    return lax.dot_general(a, b, (((1,), (1,)), ((), ())), preferred_element_type=F32)


_hdot = functools.partial(jnp.dot, precision=lax.Precision.HIGHEST, preferred_element_type=F32)


def _split_bf16(a):
    hi = a.astype(BF16)
    return hi, (a - hi.astype(F32)).astype(BF16)


def _norm_mod(x, g, shift, scale):
    ms = jnp.mean(x * x, axis=-1, keepdims=True)
    return (x * lax.rsqrt(ms + EPS) * g) * (1.0 + scale) + shift


def _params(*sem):
    return pltpu.CompilerParams(dimension_semantics=sem, vmem_limit_bytes=VMEM_LIMIT)


def _modvec_body(cond_ref, w_ref, b_ref, o_ref):
    cnd = cond_ref[...]
    s = cnd * jax.nn.sigmoid(cnd)
    s_hi, s_lo = _split_bf16(s)
    w_hi, w_lo = _split_bf16(w_ref[0])
    o_ref[0] = _dot(s_hi, w_hi) + _dot(s_lo, w_hi) + _dot(s_hi, w_lo) + b_ref[0]


def _modvec(cond, w_mod, b_mod):
    depth, d, n = w_mod.shape
    tn = n // 4
    return pl.pallas_call(
        _modvec_body,
        grid=(depth, n // tn),
        in_specs=[
            pl.BlockSpec((MOD_ROWS, d), lambda i, j: (0, 0)),
            pl.BlockSpec((1, d, tn), lambda i, j: (i, 0, j)),
            pl.BlockSpec((1, 1, tn), lambda i, j: (i, 0, j)),
        ],
        out_specs=pl.BlockSpec((1, MOD_ROWS, tn), lambda i, j: (i, 0, j)),
        out_shape=jax.ShapeDtypeStruct((depth, MOD_ROWS, n), F32),
        compiler_params=_params("parallel", "parallel"),
        name="modvec",
    )(cond, w_mod, b_mod.reshape(depth, 1, n))


def _tok_spec(width, tm):
    return pl.BlockSpec((1, tm, width), lambda b, j: (b, j, 0))


def _const_spec(shape):
    nd = len(shape)
    return pl.BlockSpec(shape, lambda b, j: (0,) * nd)


def _mod_spec(d, n_lat_tiles, n_samples):
    return pl.BlockSpec((1, 6, d), lambda b, j: (jnp.where(j < n_lat_tiles, b, n_samples), 0, 0))


def _qk_norm_rope(a, g_tile, gmat, cos, sin_signed, first_half):
    hi, lo = _split_bf16(a * a)
    ss = _dot(hi, gmat) + _dot(lo, gmat)
    an = a * lax.rsqrt(ss * (1.0 / DA_HEAD_DIM) + EPS) * g_tile
    outs = []
    for ci in range(QK_COLS // LANES):
        ch = an[:, ci * LANES:(ci + 1) * LANES]
        partner = jnp.where(first_half, pltpu.roll(ch, LANES - 16, 1), pltpu.roll(ch, 16, 1))
        outs.append(ch * cos + partner * sin_signed)
    return jnp.concatenate(outs, axis=1)


def _inproj_body(x_ref, mod_ref, g1_ref, w_ref, cos_ref, sin_ref, qg_ref, kg_ref, gmat_ref,
                 sgg_ref, sgw_ref, sgb_ref, hy_ref, q_ref, k_ref, v_ref, sg_ref):
    tm = x_ref.shape[1]
    h = _norm_mod(x_ref[0], g1_ref[...], mod_ref[0, 0:1, :], mod_ref[0, 1:2, :]).astype(BF16)
    hy_ref[0] = _dot(h, w_ref[:, 0:HY_COLS])

    lane = lax.broadcasted_iota(jnp.int32, (1, LANES), 1)
    first_half = (lane % 32) < 16
    cos = cos_ref[...]
    sin_signed = sin_ref[...]
    gmat = gmat_ref[...]
    o = HY_COLS
    q = _qk_norm_rope(_dot(h, w_ref[:, o:o + QK_COLS]), qg_ref[...], gmat, cos, sin_signed, first_half)
    q_ref[0] = (q * (DA_HEAD_DIM ** -0.5 * math.log2(math.e))).astype(BF16)
    o += QK_COLS
    k = _qk_norm_rope(_dot(h, w_ref[:, o:o + QK_COLS]), kg_ref[...], gmat, cos, sin_signed, first_half)
    k_ref[0] = k.astype(BF16)
    o += QK_COLS
    v_ref[0] = _dot(h, w_ref[:, o:o + DA_HEADS * DA_V_DIM]).astype(BF16)
    o += DA_HEADS * DA_V_DIM

    z = jax.nn.gelu(_dot(h, w_ref[:, o:o + SG_COLS]))
    u = z[:, :SG_WIDTH]
    vv = z[:, SG_WIDTH:]
    vn = (vv * lax.rsqrt(jnp.mean(vv * vv, axis=-1, keepdims=True) + EPS) * sgg_ref[...]).astype(BF16)
    gw = SG_WIDTH // SG_GROUPS
    for ch in range(tm // SG_CHUNK):
        r0 = ch * SG_CHUNK
        cols = []
        for g in range(SG_GROUPS):
            s = _dot(sgw_ref[g], vn[r0:r0 + SG_CHUNK, g * gw:(g + 1) * gw]) + sgb_ref[:, g * gw:(g + 1) * gw]
            cols.append(u[r0:r0 + SG_CHUNK, g * gw:(g + 1) * gw] * s)
        sg_ref[0, r0:r0 + SG_CHUNK, :] = jnp.concatenate(cols, axis=1).astype(BF16)


def _inproj(xa, mod_i, g1, w_pre, cos_t, sin_t, qg, kg, gmat, sgg, sgw, sgb, n_lat_tiles, n_samples):
    b, t, d = xa.shape
    tm = TOK_TILE
    outs = [jax.ShapeDtypeStruct((b, t, HY_COLS), F32)] + [jax.ShapeDtypeStruct((b, t, QK_COLS), BF16)] * 4
    return pl.pallas_call(
        _inproj_body,
        grid=(b, t // tm),
        in_specs=[
            _tok_spec(d, tm),
            _mod_spec(d, n_lat_tiles, n_samples),
            _const_spec((1, d)),
            _const_spec((d, PRE_COLS)),
            pl.BlockSpec((tm, LANES), lambda bi, j: (j, 0)),
            pl.BlockSpec((tm, LANES), lambda bi, j: (j, 0)),
            _const_spec((1, QK_COLS)),
            _const_spec((1, QK_COLS)),
            _const_spec((QK_COLS, QK_COLS)),
            _const_spec((1, SG_WIDTH)),
            _const_spec((SG_GROUPS, SG_CHUNK, SG_CHUNK)),
            _const_spec((SG_CHUNK, SG_WIDTH)),
        ],
        out_specs=[_tok_spec(HY_COLS, tm)] + [_tok_spec(QK_COLS, tm)] * 4,
        out_shape=outs,
        compiler_params=_params("parallel", "parallel"),
        name="inproj",
    )(xa, mod_i, g1, w_pre, cos_t, sin_t, qg, kg, gmat, sgg, sgw, sgb)


def _hyfilter_body(w1t_ref, w1c_ref, w1s_ref, b1_ref, w2_ref, b2_ref, fr_ref, w3f_ref, w3b_ref,
                   bands_ref, dl_ref, o_ref, *, seq):
    n = 2 * seq
    xi = lax.broadcasted_iota(jnp.int32, (1, n), 1)
    lag = xi - (seq - 1)
    pos = jnp.abs(lag).astype(F32)
    t = pos / (seq - 1)
    ang = 2.0 * math.pi * pos * bands_ref[...] / seq
    fr = fr_ref[...]
    z1 = w1t_ref[...] * t + _hdot(w1c_ref[...], jnp.cos(ang)) + _hdot(w1s_ref[...], jnp.sin(ang)) + b1_ref[...]
    h1 = jnp.sin(fr * z1)
    h2 = jnp.sin(fr * (_hdot(w2_ref[...], h1) + b2_ref[...]))
    fwd = _hdot(w3f_ref[...], h2)
    bwd = _hdot(w3b_ref[...], h2)
    window = jnp.exp(-t * dl_ref[...])
    k = jnp.where(lag > 0, fwd, jnp.where(lag < 0, bwd, fwd + bwd)) * window
    k = jnp.where(xi < n - 1, k, 0.0)
    o_ref[...] = k / jnp.sum(jnp.abs(k), axis=-1, keepdims=True)


def _hyfilter(fw, seq):
    rows = HY_ORDER * HY_WIDTH
    rb = 256
    n = 2 * seq
    small = lambda shape: pl.BlockSpec(shape, lambda i: (0, 0))
    return pl.pallas_call(
        functools.partial(_hyfilter_body, seq=seq),
        grid=(rows // rb,),
        in_specs=[
            small((HY_FFN, 1)), small((HY_FFN, HY_BANDS)), small((HY_FFN, HY_BANDS)), small((HY_FFN, 1)),
            small((HY_FFN, HY_FFN)), small((HY_FFN, 1)), small((HY_FFN, 1)),
            pl.BlockSpec((rb, HY_FFN), lambda i: (i, 0)),
            pl.BlockSpec((rb, HY_FFN), lambda i: (i, 0)),
            small((HY_BANDS, 1)),
            pl.BlockSpec((rb, 1), lambda i: (i, 0)),
        ],
        out_specs=pl.BlockSpec((rb, n), lambda i: (i, 0)),
        out_shape=jax.ShapeDtypeStruct((rows, n), F32),
        compiler_params=_params("parallel"),
        name="hyfilter",
    )(*fw)


def _hyconv_body(sc_ref, z_ref, x1_ref, x2_ref, hml_ref, hmc_ref, o_ref, big_l, big_c, *, seq, ctx, cb):
    tot = seq + ctx
    lane = lax.broadcasted_iota(jnp.int32, (1, tot), 1)
    has_prev = jnp.logical_and(lane != 0, lane != seq)
    has_next = jnp.logical_and(lane != seq - 1, lane != tot - 1)
    row = lax.broadcasted_iota(jnp.int32, (LANES, LANES), 0)
    col = lax.broadcasted_iota(jnp.int32, (LANES, LANES), 1)
    lower = col <= row
    c0 = pl.program_id(0) * cb

    def short_conv(p, base, part):
        prev = jnp.where(has_prev, pltpu.roll(p, 1, 1), 0.0)
        nxt = jnp.where(has_next, pltpu.roll(p, tot - 1, 1), 0.0)
        return (sc_ref[base + 9 + part] + sc_ref[base + part] * prev
                + sc_ref[base + 3 + part] * p + sc_ref[base + 6 + part] * nxt)

    def build(hm_ref, o, ci, big_ref):
        n_rows = hm_ref.shape[2]
        prev = None
        for rp in range(n_rows - 1, -1, -1):
            r = hm_ref[o, ci, rp:rp + 1, :]
            cur = pltpu.roll(jnp.broadcast_to(r, (LANES, LANES)), 1, 1, stride=1, stride_axis=0)
            if prev is not None:
                rho = n_rows - 2 - rp
                big_ref[o, rho * LANES:(rho + 1) * LANES, :] = jnp.where(lower, cur, prev).astype(BF16)
            prev = cur

    def long_conv(zz, big_ref, o, length, base):
        nblk = length // CONV_TILE
        mid = (2 * nblk - 1) * LANES
        ys = [None] * nblk
        for dd in range(-(nblk - 1), nblk):
            r0 = mid - CONV_TILE * dd
            w = jnp.concatenate([big_ref[o, r0:r0 + CONV_TILE, :],
                                 big_ref[o, r0 - LANES:r0 - LANES + CONV_TILE, :]], axis=1)
            js = list(range(max(0, -dd), min(nblk, nblk - dd)))
            lhs = jnp.concatenate([zz[:, base + j * CONV_TILE:base + (j + 1) * CONV_TILE] for j in js], axis=0)
            out = _dot(lhs.astype(BF16), w)
            nb = zz.shape[0]
            for kk, j in enumerate(js):
                piece = out[kk * nb:(kk + 1) * nb]
                ys[j + dd] = piece if ys[j + dd] is None else ys[j + dd] + piece
        return ys

    def chan(ci, carry):
        base = (c0 + ci) * 16
        z = short_conv(z_ref[ci], base, 0)
        gates = (short_conv(x1_ref[ci], base, 1), short_conv(x2_ref[ci], base, 2))
        for o in range(HY_ORDER):
            build(hml_ref, o, ci, big_l)
            build(hmc_ref, o, ci, big_c)
        for o in range(HY_ORDER):
            y = jnp.concatenate(long_conv(z, big_l, o, seq, 0) + long_conv(z, big_c, o, ctx, seq), axis=1)
            z = gates[o] * (y + sc_ref[base + 12 + o] * z)
        o_ref[ci] = z
        return carry

    lax.fori_loop(0, cb, chan, 0, unroll=2)


def _hyconv(sc_tab, u3, hm_l, hm_c, seq, ctx):
    _, b, tot = u3.shape
    cb = HY_CHANNELS_PER_STEP
    nblk = HY_WIDTH // cb
    slab = lambda part: pl.BlockSpec((cb, b, tot), lambda i: (part * nblk + i, 0, 0))
    return pl.pallas_call(
        functools.partial(_hyconv_body, seq=seq, ctx=ctx, cb=cb),
        grid=(nblk,),
        in_specs=[
            pl.BlockSpec(memory_space=pltpu.SMEM),
            slab(0), slab(1), slab(2),
            pl.BlockSpec((HY_ORDER, cb, hm_l.shape[2], LANES), lambda i: (0, i, 0, 0)),
            pl.BlockSpec((HY_ORDER, cb, hm_c.shape[2], LANES), lambda i: (0, i, 0, 0)),
        ],
        out_specs=pl.BlockSpec((cb, b, tot), lambda i: (i, 0, 0)),
        out_shape=jax.ShapeDtypeStruct((HY_WIDTH, b, tot), F32),
        scratch_shapes=[
            pltpu.VMEM((HY_ORDER, (hm_l.shape[2] - 1) * LANES, LANES), BF16),
            pltpu.VMEM((HY_ORDER, (hm_c.shape[2] - 1) * LANES, LANES), BF16),
        ],
        compiler_params=_params("parallel"),
        name="hyconv",
    )(sc_tab, u3, u3, u3, hm_l, hm_c)


def _attn_body(lam_ref, q_ref, k_ref, v_ref, sub_ref, o_ref, *, lam_init):
    tq = q_ref.shape[1]
    k = k_ref[0]
    v = v_ref[0]
    lane = lax.broadcasted_iota(jnp.int32, (1, 2 * DA_HEAD_DIM), 1)
    first = lane < DA_HEAD_DIM
    lp = lam_ref[...]
    lam = (jnp.exp(jnp.sum(lp[0:1] * lp[1:2], keepdims=True))
           - jnp.exp(jnp.sum(lp[2:3] * lp[3:4], keepdims=True)) + lam_init)
    sub = sub_ref[...] * (1.0 - lam_init)
    for r0 in range(0, tq, ATTN_ROWS):
        q = q_ref[0, r0:r0 + ATTN_ROWS, :]
        zero = jnp.zeros_like(q)
        s1 = _dot_nt(jnp.where(first, q, zero), k)
        s2 = _dot_nt(jnp.where(first, zero, q), k)
        p1 = jnp.exp2(s1 - jnp.max(s1, axis=-1, keepdims=True))
        p2 = jnp.exp2(s2 - jnp.max(s2, axis=-1, keepdims=True))
        w = p1 * (1.0 / jnp.sum(p1, axis=-1, keepdims=True)) - p2 * (lam / jnp.sum(p2, axis=-1, keepdims=True))
        o = _dot(w.astype(BF16), v)
        on = o * lax.rsqrt(jnp.mean(o * o, axis=-1, keepdims=True) + EPS) * sub
        o_ref[0, r0:r0 + ATTN_ROWS, :] = on.astype(BF16)


def _attention(lam_p, q, k, v, sub, lam_init, seq):
    b, tot, _ = q.shape
    ctx = tot - seq
    tq = TOK_TILE
    hw = 2 * DA_HEAD_DIM
    body = functools.partial(_attn_body, lam_init=lam_init)
    small = [pl.BlockSpec((4, DA_HEAD_DIM), lambda bi, h, j: (0, 0)), pl.BlockSpec((1, hw), lambda bi, h, j: (0, 0))]
    lat = pl.pallas_call(
        body,
        grid=(b, DA_HEADS, seq // tq),
        in_specs=[
            small[0],
            pl.BlockSpec((1, tq, hw), lambda bi, h, j: (bi, j, h)),
            pl.BlockSpec((1, tot, hw), lambda bi, h, j: (bi, 0, h)),
            pl.BlockSpec((1, tot, hw), lambda bi, h, j: (bi, 0, h)),
            small[1],
        ],
        out_specs=pl.BlockSpec((1, tq, hw), lambda bi, h, j: (bi, j, h)),
        out_shape=jax.ShapeDtypeStruct((b, seq, DA_HEADS * DA_V_DIM), BF16),
        compiler_params=_params("parallel", "parallel", "parallel"),
        name="diffattn",
    )(lam_p, q, k, v, sub)
    q0 = seq // tq
    k0 = seq // ctx
    ctx_out = pl.pallas_call(
        body,
        grid=(b, DA_HEADS, ctx // tq),
        in_specs=[
            small[0],
            pl.BlockSpec((1, tq, hw), lambda bi, h, j: (bi, q0 + j, h)),
            pl.BlockSpec((1, ctx, hw), lambda bi, h, j: (bi, k0, h)),
            pl.BlockSpec((1, ctx, hw), lambda bi, h, j: (bi, k0, h)),
            small[1],
        ],
        out_specs=pl.BlockSpec((1, tq, hw), lambda bi, h, j: (bi, j, h)),
        out_shape=jax.ShapeDtypeStruct((b, ctx, DA_HEADS * DA_V_DIM), BF16),
        compiler_params=_params("parallel", "parallel", "parallel"),
        name="diffattn_ctx",
    )(lam_p, q, k, v, sub)
    return jnp.concatenate([lat, ctx_out], axis=1)


def _merge_body(x_ref, mod_ref, g1_ref, wg_ref, yh_ref, yd_ref, ys_ref, wb_ref, wo_ref, o_ref):
    x = x_ref[0]
    d = x.shape[1]
    h = _norm_mod(x, g1_ref[...], mod_ref[0, 0:1, :], mod_ref[0, 1:2, :]).astype(BF16)
    acc = None
    for n, y_ref in enumerate((yh_ref, yd_ref, ys_ref)):
        gate = jax.nn.sigmoid(_dot(h, wg_ref[:, n * d:(n + 1) * d]))
        term = gate * _dot(y_ref[0], wb_ref[n])
        acc = term if acc is None else acc + term
    o_ref[0] = x + mod_ref[0, 2:3, :] * _dot(acc.astype(BF16), wo_ref[...])


def _merge(xa, mod_i, g1, w_gate, y_hy, y_da, y_sg, w_br, w_out, n_lat_tiles, n_samples):
    b, t, d = xa.shape
    tm = TOK_TILE
    return pl.pallas_call(
        _merge_body,
        grid=(b, t // tm),
        in_specs=[
            _tok_spec(d, tm), _mod_spec(d, n_lat_tiles, n_samples), _const_spec((1, d)),
            _const_spec((d, 3 * d)),
            _tok_spec(HY_WIDTH, tm), _tok_spec(HY_WIDTH, tm), _tok_spec(HY_WIDTH, tm),
            _const_spec((3, HY_WIDTH, d)), _const_spec((d, d)),
        ],
        out_specs=_tok_spec(d, tm),
        out_shape=jax.ShapeDtypeStruct((b, t, d), F32),
        compiler_params=_params("parallel", "parallel"),
        name="merge",
    )(xa, mod_i, g1, w_gate, y_hy, y_da, y_sg, w_br, w_out)


def _ffn_body(x_ref, mod_ref, g2_ref, wg_ref, wu_ref, wd_ref, o_ref, *, chunk):
    x = x_ref[0]
    h = _norm_mod(x, g2_ref[...], mod_ref[0, 3:4, :], mod_ref[0, 4:5, :]).astype(BF16)
    ff = wg_ref.shape[1]
    acc = None
    for f0 in range(0, ff, chunk):
        f1 = min(ff, f0 + chunk)
        a = _dot(h, wg_ref[:, f0:f1])
        mid = (a * jax.nn.sigmoid(a) * _dot(h, wu_ref[:, f0:f1])).astype(BF16)
        term = _dot(mid, wd_ref[f0:f1, :])
        acc = term if acc is None else acc + term
    o_ref[0] = x + mod_ref[0, 5:6, :] * acc


def _ffn_dense(xa, mod_i, g2, wg, wu, wd, n_lat_tiles, n_samples):
    b, t, d = xa.shape
    ff = wg.shape[1]
    tm = TOK_TILE
    return pl.pallas_call(
        functools.partial(_ffn_body, chunk=1024),
        grid=(b, t // tm),
        in_specs=[
            _tok_spec(d, tm), _mod_spec(d, n_lat_tiles, n_samples), _const_spec((1, d)),
            _const_spec((d, ff)), _const_spec((d, ff)), _const_spec((ff, d)),
        ],
        out_specs=_tok_spec(d, tm),
        out_shape=jax.ShapeDtypeStruct((b, t, d), F32),
        compiler_params=_params("parallel", "parallel"),
        name="ffn_dense",
    )(xa, mod_i, g2, wg, wu, wd)


def _router_body(x_ref, mod_ref, g2_ref, wr_ref, h_ref, r_ref):
    h = _norm_mod(x_ref[0], g2_ref[...], mod_ref[0, 3:4, :], mod_ref[0, 4:5, :])
    h_ref[0] = h
    h_hi, h_lo = _split_bf16(h)
    w_hi, w_lo = _split_bf16(wr_ref[...])
    logits = _dot_nt(w_hi, h_hi) + _dot_nt(w_hi, h_lo) + _dot_nt(w_lo, h_hi)
    eid = lax.broadcasted_iota(jnp.int32, logits.shape, 0)
    m1 = jnp.max(logits, axis=0, keepdims=True)
    i1 = jnp.min(jnp.where(logits == m1, eid, N_EXPERTS), axis=0, keepdims=True)
    rest = jnp.where(eid == i1, -jnp.inf, logits)
    m2 = jnp.max(rest, axis=0, keepdims=True)
    i2 = jnp.min(jnp.where(rest == m2, eid, N_EXPERTS), axis=0, keepdims=True)
    w1 = 1.0 / (1.0 + jnp.exp(m2 - m1))
    rows = lax.broadcasted_iota(jnp.int32, logits.shape, 0)
    out = jnp.where(rows == 0, i1.astype(F32), jnp.where(rows == 1, i2.astype(F32),
                    jnp.where(rows == 2, w1, jnp.where(rows == 3, 1.0 - w1, 0.0))))
    r_ref[0] = out


def _router(xa, mod_i, g2, wr_t, n_lat_tiles, n_samples):
    b, t, d = xa.shape
    tm = TOK_TILE
    return pl.pallas_call(
        _router_body,
        grid=(b, t // tm),
        in_specs=[
            _tok_spec(d, tm), _mod_spec(d, n_lat_tiles, n_samples), _const_spec((1, d)),
            _const_spec((N_EXPERTS, d)),
        ],
        out_specs=[_tok_spec(d, tm), pl.BlockSpec((1, N_EXPERTS, tm), lambda bi, j: (bi, 0, j))],
        out_shape=[jax.ShapeDtypeStruct((b, t, d), F32), jax.ShapeDtypeStruct((b, N_EXPERTS, t), F32)],
        compiler_params=_params("parallel", "parallel"),
        name="router",
    )(xa, mod_i, g2, wr_t)


def _row_copy(src_hbm, dst, sem, src_row, dst_row):
    return pltpu.make_async_copy(src_hbm.at[pl.ds(src_row, 1), :], dst.at[pl.ds(dst_row, 1), :], sem)


def _gather_rows(src_hbm, dst, sem, idx):
    rows = dst.shape[0]

    def start(g, c):
        for u in range(DMA_UNROLL):
            r = g * DMA_UNROLL + u
            _row_copy(src_hbm, dst, sem, idx(r), r).start(priority=u % 2)
        return c
    lax.fori_loop(0, rows // DMA_UNROLL, start, 0)


def _wait_rows(src_hbm, dst, sem):
    pltpu.make_async_copy(src_hbm.at[pl.ds(0, dst.shape[0]), :], dst, sem).wait()


def _expert_body(te_ref, nu_ref, rt_ref, rtn_ref, h_hbm, wg_ref, wu_ref, wd_ref, o_ref, xbuf, xbf, acc, sem):
    t = pl.program_id(0)
    f = pl.program_id(1)
    used = t < nu_ref[0]
    slot = t % 2

    @pl.when(jnp.logical_and(f == 0, jnp.logical_and(used, t == 0)))
    def _first_gather():
        _gather_rows(h_hbm, xbuf.at[0], sem.at[0], lambda r: rt_ref[0, 0, r])

    @pl.when(jnp.logical_and(used, f == 0))
    def _stage():
        _wait_rows(h_hbm, xbuf.at[slot], sem.at[slot])
        xbf[...] = xbuf[slot].astype(BF16)
        acc[...] = jnp.zeros_like(acc)

    @pl.when(jnp.logical_and(f == 0, t + 1 < nu_ref[0]))
    def _next_gather():
        _gather_rows(h_hbm, xbuf.at[1 - slot], sem.at[1 - slot], lambda r: rtn_ref[0, 0, r])

    @pl.when(used)
    def _compute():
        xb = xbf[...]
        a = _dot(xb, wg_ref[0])
        mid = (a * jax.nn.sigmoid(a) * _dot(xb, wu_ref[0])).astype(BF16)
        acc[...] += _dot(mid, wd_ref[0])

    @pl.when(f == pl.num_programs(1) - 1)
    def _store():
        o_ref[...] = jnp.where(used, acc[...], 0.0)


def _experts(tile_expert, n_used, row_token, h_flat, wg, wu, wd):
    n_rows = row_token.shape[0]
    d = h_flat.shape[1]
    ff = wg.shape[2]
    nt = n_rows // ROW_TILE
    rt3 = row_token.reshape(nt, 1, ROW_TILE)
    grid_spec = pltpu.PrefetchScalarGridSpec(
        num_scalar_prefetch=2,
        grid=(nt, ff // FF_TILE),
        in_specs=[
            pl.BlockSpec((1, 1, ROW_TILE), lambda t, f, te, nu: (t, 0, 0), memory_space=pltpu.SMEM),
            pl.BlockSpec((1, 1, ROW_TILE), lambda t, f, te, nu: (jnp.minimum(t + 1, nt - 1), 0, 0),
                         memory_space=pltpu.SMEM),
            pl.BlockSpec(memory_space=pl.ANY),
            pl.BlockSpec((1, d, FF_TILE), lambda t, f, te, nu: (te[t], 0, f)),
            pl.BlockSpec((1, d, FF_TILE), lambda t, f, te, nu: (te[t], 0, f)),
            pl.BlockSpec((1, FF_TILE, d), lambda t, f, te, nu: (te[t], f, 0)),
        ],
        out_specs=pl.BlockSpec((ROW_TILE, d), lambda t, f, te, nu: (t, 0)),
        scratch_shapes=[
            pltpu.VMEM((2, ROW_TILE, d), F32),
            pltpu.VMEM((ROW_TILE, d), BF16),
            pltpu.VMEM((ROW_TILE, d), F32),
            pltpu.SemaphoreType.DMA((2,)),
        ],
    )
    return pl.pallas_call(
        _expert_body,
        grid_spec=grid_spec,
        out_shape=jax.ShapeDtypeStruct((n_rows, d), F32),
        compiler_params=_params("arbitrary", "arbitrary"),
        name="experts",
    )(tile_expert, n_used, rt3, rt3, h_flat, wg, wu, wd)


def _combine_body(pos_ref, r_ref, x_ref, mod_ref, ye_hbm, o_ref, buf, sem):
    tm = x_ref.shape[1]
    for kk in range(2):
        _gather_rows(ye_hbm, buf.at[kk], sem.at[kk], lambda r, kk=kk: pos_ref[0, kk, r])
    eye = (lax.broadcasted_iota(jnp.int32, (tm, tm), 0) == lax.broadcasted_iota(jnp.int32, (tm, tm), 1)).astype(BF16)
    r_hi, r_lo = _split_bf16(r_ref[0])
    rcol = _dot_nt(eye, r_hi) + _dot_nt(eye, r_lo)
    for kk in range(2):
        _wait_rows(ye_hbm, buf.at[kk], sem.at[kk])
    mix = buf[0] * rcol[:, 2:3] + buf[1] * rcol[:, 3:4]
    o_ref[0] = x_ref[0] + mod_ref[0, 5:6, :] * mix


def _combine(pos, route, xa, mod_i, ye, n_lat_tiles, n_samples):
    b, t, d = xa.shape
    tm = TOK_TILE
    nj = t // tm
    return pl.pallas_call(
        _combine_body,
        grid=(b, nj),
        in_specs=[
            pl.BlockSpec((1, 2, tm), lambda bi, j: (bi * nj + j, 0, 0), memory_space=pltpu.SMEM),
            pl.BlockSpec((1, N_EXPERTS, tm), lambda bi, j: (bi, 0, j)),
            _tok_spec(d, tm), _mod_spec(d, n_lat_tiles, n_samples),
            pl.BlockSpec(memory_space=pl.ANY),
        ],
        out_specs=_tok_spec(d, tm),
        out_shape=jax.ShapeDtypeStruct((b, t, d), F32),
        scratch_shapes=[pltpu.VMEM((2, tm, d), F32), pltpu.SemaphoreType.DMA((2,))],
        compiler_params=_params("arbitrary", "arbitrary"),
        name="moe_combine",
    )(pos, route, xa, mod_i, ye)


def _moe(xa, mod_i, g2, wr_t, wg, wu, wd, n_lat_tiles, n_samples):
    b, t, d = xa.shape
    h, route = _router(xa, mod_i, g2, wr_t, n_lat_tiles, n_samples)
    n_tok = b * t
    expert = route[:, 0:2, :].astype(jnp.int32)
    onehot = (expert.reshape(-1)[:, None] == jnp.arange(N_EXPERTS)[None, :]).astype(jnp.int32)
    counts = jnp.sum(onehot, axis=0)
    rank = jnp.sum((jnp.cumsum(onehot, axis=0) - onehot) * onehot, axis=1)
    padded = ((counts + ROW_TILE - 1) // ROW_TILE) * ROW_TILE
    ends = jnp.cumsum(padded)
    pos = ((ends - padded)[expert.reshape(-1)] + rank).reshape(b, 2, t)
    n_rows = ((2 * n_tok + N_EXPERTS * (ROW_TILE - 1)) // ROW_TILE + 1) * ROW_TILE
    token = jnp.broadcast_to(jnp.arange(b)[:, None, None] * t + jnp.arange(t)[None, None, :], (b, 2, t))
    row_token = jnp.zeros((n_rows,), jnp.int32).at[pos.reshape(-1)].set(token.reshape(-1))
    tile_start = jnp.arange(n_rows // ROW_TILE) * ROW_TILE
    tile_expert = jnp.minimum(jnp.sum(tile_start[:, None] >= ends[None, :], axis=1), N_EXPERTS - 1).astype(jnp.int32)
    n_used = (ends[-1:] // ROW_TILE).astype(jnp.int32)
    ye = _experts(tile_expert, n_used, row_token, h.reshape(n_tok, d), wg, wu, wd)
    nj = t // TOK_TILE
    pos_tiles = pos.reshape(b, 2, nj, TOK_TILE).transpose(0, 2, 1, 3).reshape(b * nj, 2, TOK_TILE)
    return _combine(pos_tiles, route, xa, mod_i, ye, n_lat_tiles, n_samples)


def _rope_tables(seq, ctx):
    rows = seq // GRID_W
    row = jnp.repeat(jnp.arange(rows), GRID_W).astype(F32)
    col = jnp.tile(jnp.arange(GRID_W), rows).astype(F32)
    half = DA_HEAD_DIM // 2
    inv = ROPE_BASE ** (-jnp.arange(0, half, 2, dtype=F32) / half)
    ang = jnp.concatenate([row[:, None] * inv, row[:, None] * inv, col[:, None] * inv, col[:, None] * inv], axis=1)
    sign = jnp.tile(jnp.concatenate([-jnp.ones((half // 2,), F32), jnp.ones((half // 2,), F32)]), 2)
    cos = jnp.concatenate([jnp.cos(ang), jnp.ones((ctx, DA_HEAD_DIM), F32)], axis=0)
    sin = jnp.concatenate([jnp.sin(ang) * sign, jnp.zeros((ctx, DA_HEAD_DIM), F32)], axis=0)
    return jnp.tile(cos, (1, LANES // DA_HEAD_DIM)), jnp.tile(sin, (1, LANES // DA_HEAD_DIM))


def _filter_weights(w1, b1, w2, b2, w3, freq):
    col = lambda v: v.reshape(-1, 1)
    bands = jnp.linspace(1e-4, HY_BANDS - 1, HY_BANDS, dtype=F32)
    deltas = jnp.abs(jnp.linspace(HY_MIN_DECAY, HY_MAX_DECAY, HY_WIDTH, dtype=F32))
    half = HY_ORDER * HY_WIDTH
    return (w1[0:1].T, w1[1:1 + HY_BANDS].T, w1[1 + HY_BANDS:].T, col(b1), w2.T, col(b2), col(freq),
            w3[:, :half].T, w3[:, half:].T, col(bands), col(jnp.tile(deltas, HY_ORDER)))


def kernel(x, c, ctx, c_ctx, w_mod, b_mod, norm1_g, norm2_g, w_in, hy_conv_w, hy_conv_b, hy_f_w1, hy_f_b1, hy_f_w2, hy_f_b2, hy_f_w3, hy_f_freq, hy_skip, q_norm_g, k_norm_g, lam_q1, lam_k1, lam_q2, lam_k2, subln_g, sg_norm_g, sg_w, sg_b, w_branch, w_out, ffn_wg, ffn_wu, ffn_wd, router_w, moe_wg, moe_wu, moe_wd):
    n_samples, seq, d = x.shape
    n_ctx = ctx.shape[1]
    depth = w_mod.shape[0]
    assert seq % CONV_TILE == 0 and n_ctx % CONV_TILE == 0 and seq % GRID_W == 0 and seq % n_ctx == 0
    assert n_samples + 1 <= MOD_ROWS
    n_lat_tiles = seq // TOK_TILE
    tot = seq + n_ctx

    xa = jnp.concatenate([x, ctx], axis=1)
    cond = jnp.zeros((MOD_ROWS, d), F32).at[:n_samples].set(c).at[n_samples].set(c_ctx)
    mod = _modvec(cond, w_mod, b_mod).reshape(depth, MOD_ROWS, 6, d)

    cos_t, sin_t = _rope_tables(seq, n_ctx)
    group = jnp.arange(QK_COLS) // DA_HEAD_DIM
    gmat = (group[:, None] == group[None, :]).astype(BF16)
    w_in_b = w_in.astype(BF16)
    w_br_b = w_branch.astype(BF16)
    w_out_b = w_out.astype(BF16)

    for i in range(depth):
        lam_init = 0.8 - 0.6 * math.exp(-0.3 * i)
        g1 = norm1_g[i].reshape(1, d)
        g2 = norm2_g[i].reshape(1, d)
        hy, q, k, v, y_sg = _inproj(
            xa, mod[i], g1, w_in_b[i, :, :PRE_COLS], cos_t, sin_t,
            jnp.tile(q_norm_g[i], QK_COLS // DA_HEAD_DIM).reshape(1, QK_COLS),
            jnp.tile(k_norm_g[i], QK_COLS // DA_HEAD_DIM).reshape(1, QK_COLS),
            gmat, sg_norm_g[i].reshape(1, SG_WIDTH), sg_w[i].astype(BF16),
            jnp.repeat(sg_b[i].T, SG_WIDTH // SG_GROUPS, axis=1), n_lat_tiles, n_samples)

        fw = _filter_weights(hy_f_w1[i], hy_f_b1[i], hy_f_w2[i], hy_f_b2[i], hy_f_w3[i], hy_f_freq[i])
        hm_l = _hyfilter(fw, seq).reshape(HY_ORDER, HY_WIDTH, 2 * seq // LANES, LANES)
        hm_c = _hyfilter(fw, n_ctx).reshape(HY_ORDER, HY_WIDTH, 2 * n_ctx // LANES, LANES)
        cw = hy_conv_w[i].reshape(3, 3, HY_WIDTH)
        cbias = hy_conv_b[i].reshape(3, HY_WIDTH)
        sc_tab = jnp.concatenate([cw.reshape(9, HY_WIDTH), cbias, hy_skip[i], jnp.zeros((2, HY_WIDTH), F32)],
                                 axis=0).T.reshape(-1)
        y_hy = _hyconv(sc_tab, jnp.transpose(hy, (2, 0, 1)), hm_l, hm_c, seq, n_ctx)
        y_hy = jnp.transpose(y_hy, (1, 2, 0)).astype(BF16)

        lam_p = jnp.stack([lam_q1[i], lam_k1[i], lam_q2[i], lam_k2[i]], axis=0)
        y_da = _attention(lam_p, q, k, v, subln_g[i].reshape(1, DA_V_DIM), lam_init, seq)

        xa = _merge(xa, mod[i], g1, w_in_b[i, :, PRE_COLS:], y_hy, y_da, y_sg, w_br_b[i], w_out_b[i],
                    n_lat_tiles, n_samples)

        j = i // 2
        if i % 2 == 0:
            xa = _ffn_dense(xa, mod[i], g2, ffn_wg[j].astype(BF16), ffn_wu[j].astype(BF16),
                            ffn_wd[j].astype(BF16), n_lat_tiles, n_samples)
        else:
            xa = _moe(xa, mod[i], g2, router_w[j].T, moe_wg[j].astype(BF16), moe_wu[j].astype(BF16),
                      moe_wd[j].astype(BF16), n_lat_tiles, n_samples)
    return xa[:, :seq]
```

```python
import functools
import math

import jax
import jax.numpy as jnp
from jax import lax
from jax.experimental import pallas as pl
from jax.experimental.pallas import tpu as pltpu

F32 = jnp.float32
BF16 = jnp.bfloat16
EPS = 1e-6

GRID_W = 64
HY_WIDTH = 512
HY_ORDER = 2
HY_BANDS = 16
HY_FFN = 64
HY_MAX_DECAY = math.log(1e-2) / 0.3
HY_MIN_DECAY = math.log(1e-2) / 1.5
DA_HEADS = 4
DA_HEAD_DIM = 64
DA_V_DIM = 2 * DA_HEAD_DIM
ROPE_BASE = 10000.0
SG_WIDTH = 512
SG_GROUPS = 4
SG_CHUNK = 128
N_EXPERTS = 8
HY_COLS = 3 * HY_WIDTH
QK_COLS = DA_HEADS * 2 * DA_HEAD_DIM
DA_COLS = 2 * QK_COLS + DA_HEADS * DA_V_DIM
SG_COLS = 2 * SG_WIDTH
PRE_COLS = HY_COLS + DA_COLS + SG_COLS

LANES = 128
TOK_TILE = 256
ATTN_ROWS = 256
CONV_TILE = 256
HY_CHANNELS_PER_STEP = 8
ROW_TILE = 1024
FF_TILE = 512
DMA_UNROLL = 8
MOD_ROWS = 16
VMEM_LIMIT = 56 * 1024 * 1024


def _dot(a, b):
    return jnp.dot(a, b, preferred_element_type=F32)


def _dot_nt(a, b):
    return lax.dot_general(a, b, (((1,), (1,)), ((), ())), preferred_element_type=F32)


_hdot = functools.partial(jnp.dot, precision=lax.Precision.HIGHEST, preferred_element_type=F32)


def _split_bf16(a):
    hi = a.astype(BF16)
    return hi, (a - hi.astype(F32)).astype(BF16)


def _norm_mod(x, g, shift, scale):
    ms = jnp.mean(x * x, axis=-1, keepdims=True)
    return (x * lax.rsqrt(ms + EPS) * g) * (1.0 + scale) + shift


def _params(*sem):
    return pltpu.CompilerParams(dimension_semantics=sem, vmem_limit_bytes=VMEM_LIMIT)


def _modvec_body(cond_ref, w_ref, b_ref, o_ref):
    cnd = cond_ref[...]
    s = cnd * jax.nn.sigmoid(cnd)
    s_hi, s_lo = _split_bf16(s)
    w_hi, w_lo = _split_bf16(w_ref[0])
    o_ref[0] = _dot(s_hi, w_hi) + _dot(s_lo, w_hi) + _dot(s_hi, w_lo) + b_ref[0]


def _modvec(cond, w_mod, b_mod):
    depth, d, n = w_mod.shape
    tn = n // 4
    return pl.pallas_call(
        _modvec_body,
        grid=(depth, n // tn),
        in_specs=[
            pl.BlockSpec((MOD_ROWS, d), lambda i, j: (0, 0)),
            pl.BlockSpec((1, d, tn), lambda i, j: (i, 0, j)),
            pl.BlockSpec((1, 1, tn), lambda i, j: (i, 0, j)),
        ],
        out_specs=pl.BlockSpec((1, MOD_ROWS, tn), lambda i, j: (i, 0, j)),
        out_shape=jax.ShapeDtypeStruct((depth, MOD_ROWS, n), F32),
        compiler_params=_params("parallel", "parallel"),
        name="modvec",
    )(cond, w_mod, b_mod.reshape(depth, 1, n))


def _tok_spec(width, tm):
    return pl.BlockSpec((1, tm, width), lambda b, j: (b, j, 0))


def _const_spec(shape):
    nd = len(shape)
    return pl.BlockSpec(shape, lambda b, j: (0,) * nd)


def _mod_spec(d, n_lat_tiles, n_samples):
    return pl.BlockSpec((1, 6, d), lambda b, j: (jnp.where(j < n_lat_tiles, b, n_samples), 0, 0))


def _qk_norm_rope(a, g_tile, gmat, cos, sin_signed, first_half):
    hi, lo = _split_bf16(a * a)
    ss = _dot(hi, gmat) + _dot(lo, gmat)
    an = a * lax.rsqrt(ss * (1.0 / DA_HEAD_DIM) + EPS) * g_tile
    outs = []
    for ci in range(QK_COLS // LANES):
        ch = an[:, ci * LANES:(ci + 1) * LANES]
        partner = jnp.where(first_half, pltpu.roll(ch, LANES - 16, 1), pltpu.roll(ch, 16, 1))
        outs.append(ch * cos + partner * sin_signed)
    return jnp.concatenate(outs, axis=1)


def _inproj_body(x_ref, mod_ref, g1_ref, w_ref, cos_ref, sin_ref, qg_ref, kg_ref, gmat_ref,
                 sgg_ref, sgw_ref, sgb_ref, hy_ref, q_ref, k_ref, v_ref, sg_ref):
    tm = x_ref.shape[1]
    h = _norm_mod(x_ref[0], g1_ref[...], mod_ref[0, 0:1, :], mod_ref[0, 1:2, :]).astype(BF16)
    hy_ref[0] = _dot(h, w_ref[0, :, 0:HY_COLS])

    lane = lax.broadcasted_iota(jnp.int32, (1, LANES), 1)
    first_half = (lane % 32) < 16
    cos = cos_ref[...]
    sin_signed = sin_ref[...]
    gmat = gmat_ref[...]
    o = HY_COLS
    q = _qk_norm_rope(_dot(h, w_ref[0, :, o:o + QK_COLS]), qg_ref[...], gmat, cos, sin_signed, first_half)
    q_ref[0] = (q * (DA_HEAD_DIM ** -0.5 * math.log2(math.e))).astype(BF16)
    o += QK_COLS
    k = _qk_norm_rope(_dot(h, w_ref[0, :, o:o + QK_COLS]), kg_ref[...], gmat, cos, sin_signed, first_half)
    k_ref[0] = k.astype(BF16)
    o += QK_COLS
    v_ref[0] = _dot(h, w_ref[0, :, o:o + DA_HEADS * DA_V_DIM]).astype(BF16)
    o += DA_HEADS * DA_V_DIM

    z = jax.nn.gelu(_dot(h, w_ref[0, :, o:o + SG_COLS]))
    u = z[:, :SG_WIDTH]
    vv = z[:, SG_WIDTH:]
    vn = (vv * lax.rsqrt(jnp.mean(vv * vv, axis=-1, keepdims=True) + EPS) * sgg_ref[...]).astype(BF16)
    gw = SG_WIDTH // SG_GROUPS
    for ch in range(tm // SG_CHUNK):
        r0 = ch * SG_CHUNK
        cols = []
        for g in range(SG_GROUPS):
            s = _dot(sgw_ref[g], vn[r0:r0 + SG_CHUNK, g * gw:(g + 1) * gw]) + sgb_ref[:, g * gw:(g + 1) * gw]
            cols.append(u[r0:r0 + SG_CHUNK, g * gw:(g + 1) * gw] * s)
        sg_ref[0, r0:r0 + SG_CHUNK, :] = jnp.concatenate(cols, axis=1).astype(BF16)


def _inproj(xa, mod_i, g1, w_in, layer, cos_t, sin_t, qg, kg, gmat, sgg, sgw, sgb, n_lat_tiles, n_samples):
    b, t, d = xa.shape
    tm = TOK_TILE
    outs = [jax.ShapeDtypeStruct((b, t, HY_COLS), F32)] + [jax.ShapeDtypeStruct((b, t, QK_COLS), BF16)] * 4
    return pl.pallas_call(
        _inproj_body,
        grid=(b, t // tm),
        in_specs=[
            _tok_spec(d, tm),
            _mod_spec(d, n_lat_tiles, n_samples),
            _const_spec((1, d)),
            pl.BlockSpec((1, d, PRE_COLS), lambda bi, j: (layer, 0, 0)),
            pl.BlockSpec((tm, LANES), lambda bi, j: (j, 0)),
            pl.BlockSpec((tm, LANES), lambda bi, j: (j, 0)),
            _const_spec((1, QK_COLS)),
            _const_spec((1, QK_COLS)),
            _const_spec((QK_COLS, QK_COLS)),
            _const_spec((1, SG_WIDTH)),
            _const_spec((SG_GROUPS, SG_CHUNK, SG_CHUNK)),
            _const_spec((SG_CHUNK, SG_WIDTH)),
        ],
        out_specs=[_tok_spec(HY_COLS, tm)] + [_tok_spec(QK_COLS, tm)] * 4,
        out_shape=outs,
        compiler_params=_params("parallel", "parallel"),
        name="inproj",
    )(xa, mod_i, g1, w_in, cos_t, sin_t, qg, kg, gmat, sgg, sgw, sgb)


def _hyfilter_body(w1t_ref, w1c_ref, w1s_ref, b1_ref, w2_ref, b2_ref, fr_ref, w3f_ref, w3b_ref,
                   bands_ref, dl_ref, o_ref, *, seq):
    n = 2 * seq
    xi = lax.broadcasted_iota(jnp.int32, (1, n), 1)
    lag = xi - (seq - 1)
    pos = jnp.abs(lag).astype(F32)
    t = pos / (seq - 1)
    ang = 2.0 * math.pi * pos * bands_ref[...] / seq
    fr = fr_ref[...]
    z1 = w1t_ref[...] * t + _hdot(w1c_ref[...], jnp.cos(ang)) + _hdot(w1s_ref[...], jnp.sin(ang)) + b1_ref[...]
    h1 = jnp.sin(fr * z1)
    h2 = jnp.sin(fr * (_hdot(w2_ref[...], h1) + b2_ref[...]))
    fwd = _hdot(w3f_ref[...], h2)
    bwd = _hdot(w3b_ref[...], h2)
    window = jnp.exp(-t * dl_ref[...])
    k = jnp.where(lag > 0, fwd, jnp.where(lag < 0, bwd, fwd + bwd)) * window
    k = jnp.where(xi < n - 1, k, 0.0)
    o_ref[...] = k / jnp.sum(jnp.abs(k), axis=-1, keepdims=True)


def _hyfilter(fw, seq):
    rows = HY_ORDER * HY_WIDTH
    rb = 256
    n = 2 * seq
    small = lambda shape: pl.BlockSpec(shape, lambda i: (0, 0))
    return pl.pallas_call(
        functools.partial(_hyfilter_body, seq=seq),
        grid=(rows // rb,),
        in_specs=[
            small((HY_FFN, 1)), small((HY_FFN, HY_BANDS)), small((HY_FFN, HY_BANDS)), small((HY_FFN, 1)),
            small((HY_FFN, HY_FFN)), small((HY_FFN, 1)), small((HY_FFN, 1)),
            pl.BlockSpec((rb, HY_FFN), lambda i: (i, 0)),
            pl.BlockSpec((rb, HY_FFN), lambda i: (i, 0)),
            small((HY_BANDS, 1)),
            pl.BlockSpec((rb, 1), lambda i: (i, 0)),
        ],
        out_specs=pl.BlockSpec((rb, n), lambda i: (i, 0)),
        out_shape=jax.ShapeDtypeStruct((rows, n), F32),
        compiler_params=_params("parallel"),
        name="hyfilter",
    )(*fw)


def _hyconv_body(sc_ref, z_ref, x1_ref, x2_ref, hml_ref, hmc_ref, o_ref, big_l, big_c, *, seq, ctx, cb):
    tot = seq + ctx
    lane = lax.broadcasted_iota(jnp.int32, (1, tot), 1)
    has_prev = jnp.logical_and(lane != 0, lane != seq)
    has_next = jnp.logical_and(lane != seq - 1, lane != tot - 1)
    row = lax.broadcasted_iota(jnp.int32, (LANES, LANES), 0)
    col = lax.broadcasted_iota(jnp.int32, (LANES, LANES), 1)
    lower = col <= row
    c0 = pl.program_id(0) * cb

    def short_conv(p, base, part):
        prev = jnp.where(has_prev, pltpu.roll(p, 1, 1), 0.0)
        nxt = jnp.where(has_next, pltpu.roll(p, tot - 1, 1), 0.0)
        return (sc_ref[base + 9 + part] + sc_ref[base + part] * prev
                + sc_ref[base + 3 + part] * p + sc_ref[base + 6 + part] * nxt)

    def build(hm_ref, o, ci, big_ref):
        n_rows = hm_ref.shape[2]
        prev = None
        for rp in range(n_rows - 1, -1, -1):
            r = hm_ref[o, ci, rp:rp + 1, :]
            cur = pltpu.roll(jnp.broadcast_to(r, (LANES, LANES)), 1, 1, stride=1, stride_axis=0)
            if prev is not None:
                rho = n_rows - 2 - rp
                big_ref[o, rho * LANES:(rho + 1) * LANES, :] = jnp.where(lower, cur, prev).astype(BF16)
            prev = cur

    def long_conv(zz, big_ref, o, length, base):
        nblk = length // CONV_TILE
        mid = (2 * nblk - 1) * LANES
        ys = [None] * nblk
        for dd in range(-(nblk - 1), nblk):
            r0 = mid - CONV_TILE * dd
            w = jnp.concatenate([big_ref[o, r0:r0 + CONV_TILE, :],
                                 big_ref[o, r0 - LANES:r0 - LANES + CONV_TILE, :]], axis=1)
            js = list(range(max(0, -dd), min(nblk, nblk - dd)))
            lhs = jnp.concatenate([zz[:, base + j * CONV_TILE:base + (j + 1) * CONV_TILE] for j in js], axis=0)
            out = _dot(lhs.astype(BF16), w)
            nb = zz.shape[0]
            for kk, j in enumerate(js):
                piece = out[kk * nb:(kk + 1) * nb]
                ys[j + dd] = piece if ys[j + dd] is None else ys[j + dd] + piece
        return ys

    def chan(ci, carry):
        base = (c0 + ci) * 16
        z = short_conv(z_ref[ci], base, 0)
        gates = (short_conv(x1_ref[ci], base, 1), short_conv(x2_ref[ci], base, 2))
        for o in range(HY_ORDER):
            build(hml_ref, o, ci, big_l)
            build(hmc_ref, o, ci, big_c)
        for o in range(HY_ORDER):
            y = jnp.concatenate(long_conv(z, big_l, o, seq, 0) + long_conv(z, big_c, o, ctx, seq), axis=1)
            z = gates[o] * (y + sc_ref[base + 12 + o] * z)
        o_ref[ci] = z
        return carry

    lax.fori_loop(0, cb, chan, 0, unroll=4)


def _hyconv(sc_tab, u3, hm_l, hm_c, seq, ctx):
    _, b, tot = u3.shape
    cb = HY_CHANNELS_PER_STEP
    nblk = HY_WIDTH // cb
    slab = lambda part: pl.BlockSpec((cb, b, tot), lambda i: (part * nblk + i, 0, 0))
    return pl.pallas_call(
        functools.partial(_hyconv_body, seq=seq, ctx=ctx, cb=cb),
        grid=(nblk,),
        in_specs=[
            pl.BlockSpec(memory_space=pltpu.SMEM),
            slab(0), slab(1), slab(2),
            pl.BlockSpec((HY_ORDER, cb, hm_l.shape[2], LANES), lambda i: (0, i, 0, 0)),
            pl.BlockSpec((HY_ORDER, cb, hm_c.shape[2], LANES), lambda i: (0, i, 0, 0)),
        ],
        out_specs=pl.BlockSpec((cb, b, tot), lambda i: (i, 0, 0)),
        out_shape=jax.ShapeDtypeStruct((HY_WIDTH, b, tot), F32),
        scratch_shapes=[
            pltpu.VMEM((HY_ORDER, (hm_l.shape[2] - 1) * LANES, LANES), BF16),
            pltpu.VMEM((HY_ORDER, (hm_c.shape[2] - 1) * LANES, LANES), BF16),
        ],
        compiler_params=_params("parallel"),
        name="hyconv",
    )(sc_tab, u3, u3, u3, hm_l, hm_c)


def _attn_body(lam_ref, q_ref, k_ref, v_ref, sub_ref, o_ref, *, lam_init):
    tq = q_ref.shape[1]
    k = k_ref[0]
    v = v_ref[0]
    lane = lax.broadcasted_iota(jnp.int32, (1, 2 * DA_HEAD_DIM), 1)
    first = lane < DA_HEAD_DIM
    lp = lam_ref[...]
    lam = (jnp.exp(jnp.sum(lp[0:1] * lp[1:2], keepdims=True))
           - jnp.exp(jnp.sum(lp[2:3] * lp[3:4], keepdims=True)) + lam_init)
    sub = sub_ref[...] * (1.0 - lam_init)
    for r0 in range(0, tq, ATTN_ROWS):
        q = q_ref[0, r0:r0 + ATTN_ROWS, :]
        zero = jnp.zeros_like(q)
        s1 = _dot_nt(jnp.where(first, q, zero), k)
        s2 = _dot_nt(jnp.where(first, zero, q), k)
        p1 = jnp.exp2(s1 - jnp.max(s1, axis=-1, keepdims=True))
        p2 = jnp.exp2(s2 - jnp.max(s2, axis=-1, keepdims=True))
        w = p1 * (1.0 / jnp.sum(p1, axis=-1, keepdims=True)) - p2 * (lam / jnp.sum(p2, axis=-1, keepdims=True))
        o = _dot(w.astype(BF16), v)
        on = o * lax.rsqrt(jnp.mean(o * o, axis=-1, keepdims=True) + EPS) * sub
        o_ref[0, r0:r0 + ATTN_ROWS, :] = on.astype(BF16)


def _attention(lam_p, q, k, v, sub, lam_init, seq):
    b, tot, _ = q.shape
    ctx = tot - seq
    tq = TOK_TILE
    hw = 2 * DA_HEAD_DIM
    body = functools.partial(_attn_body, lam_init=lam_init)
    small = [pl.BlockSpec((4, DA_HEAD_DIM), lambda bi, h, j: (0, 0)), pl.BlockSpec((1, hw), lambda bi, h, j: (0, 0))]
    lat = pl.pallas_call(
        body,
        grid=(b, DA_HEADS, seq // tq),
        in_specs=[
            small[0],
            pl.BlockSpec((1, tq, hw), lambda bi, h, j: (bi, j, h)),
            pl.BlockSpec((1, tot, hw), lambda bi, h, j: (bi, 0, h)),
            pl.BlockSpec((1, tot, hw), lambda bi, h, j: (bi, 0, h)),
            small[1],
        ],
        out_specs=pl.BlockSpec((1, tq, hw), lambda bi, h, j: (bi, j, h)),
        out_shape=jax.ShapeDtypeStruct((b, seq, DA_HEADS * DA_V_DIM), BF16),
        compiler_params=_params("parallel", "parallel", "parallel"),
        name="diffattn",
    )(lam_p, q, k, v, sub)
    q0 = seq // tq
    k0 = seq // ctx
    ctx_out = pl.pallas_call(
        body,
        grid=(b, DA_HEADS, ctx // tq),
        in_specs=[
            small[0],
            pl.BlockSpec((1, tq, hw), lambda bi, h, j: (bi, q0 + j, h)),
            pl.BlockSpec((1, ctx, hw), lambda bi, h, j: (bi, k0, h)),
            pl.BlockSpec((1, ctx, hw), lambda bi, h, j: (bi, k0, h)),
            small[1],
        ],
        out_specs=pl.BlockSpec((1, tq, hw), lambda bi, h, j: (bi, j, h)),
        out_shape=jax.ShapeDtypeStruct((b, ctx, DA_HEADS * DA_V_DIM), BF16),
        compiler_params=_params("parallel", "parallel", "parallel"),
        name="diffattn_ctx",
    )(lam_p, q, k, v, sub)
    return jnp.concatenate([lat, ctx_out], axis=1)


def _merge_body(x_ref, mod_ref, g1_ref, wg0_ref, wg1_ref, wg2_ref, yh_ref, yd_ref, ys_ref, wb_ref, wo_ref, o_ref):
    x = x_ref[0]
    h = _norm_mod(x, g1_ref[...], mod_ref[0, 0:1, :], mod_ref[0, 1:2, :]).astype(BF16)
    acc = None
    for n, (wg_ref, y_ref) in enumerate(((wg0_ref, yh_ref), (wg1_ref, yd_ref), (wg2_ref, ys_ref))):
        gate = jax.nn.sigmoid(_dot(h, wg_ref[0]))
        term = gate * _dot(y_ref[0], wb_ref[0, n])
        acc = term if acc is None else acc + term
    o_ref[0] = x + mod_ref[0, 2:3, :] * _dot(acc.astype(BF16), wo_ref[0])


def _merge(xa, mod_i, g1, w_in, layer, y_hy, y_da, y_sg, w_br, w_out, n_lat_tiles, n_samples):
    b, t, d = xa.shape
    tm = TOK_TILE
    assert PRE_COLS % d == 0
    gate_spec = lambda n: pl.BlockSpec((1, d, d), lambda bi, j: (layer, 0, PRE_COLS // d + n))
    return pl.pallas_call(
        _merge_body,
        grid=(b, t // tm),
        in_specs=[
            _tok_spec(d, tm), _mod_spec(d, n_lat_tiles, n_samples), _const_spec((1, d)),
            gate_spec(0), gate_spec(1), gate_spec(2),
            _tok_spec(HY_WIDTH, tm), _tok_spec(HY_WIDTH, tm), _tok_spec(HY_WIDTH, tm),
            pl.BlockSpec((1, 3, HY_WIDTH, d), lambda bi, j: (layer, 0, 0, 0)),
            pl.BlockSpec((1, d, d), lambda bi, j: (layer, 0, 0)),
        ],
        out_specs=_tok_spec(d, tm),
        out_shape=jax.ShapeDtypeStruct((b, t, d), F32),
        compiler_params=_params("parallel", "parallel"),
        name="merge",
    )(xa, mod_i, g1, w_in, w_in, w_in, y_hy, y_da, y_sg, w_br, w_out)


def _ffn_body(x_ref, mod_ref, g2_ref, wg_ref, wu_ref, wd_ref, o_ref, *, chunk):
    x = x_ref[0]
    h = _norm_mod(x, g2_ref[...], mod_ref[0, 3:4, :], mod_ref[0, 4:5, :]).astype(BF16)
    ff = wg_ref.shape[2]
    acc = None
    for f0 in range(0, ff, chunk):
        f1 = min(ff, f0 + chunk)
        a = _dot(h, wg_ref[0, :, f0:f1])
        mid = (a * jax.nn.sigmoid(a) * _dot(h, wu_ref[0, :, f0:f1])).astype(BF16)
        term = _dot(mid, wd_ref[0, f0:f1, :])
        acc = term if acc is None else acc + term
    o_ref[0] = x + mod_ref[0, 5:6, :] * acc


def _ffn_dense(xa, mod_i, g2, wg, wu, wd, layer, n_lat_tiles, n_samples):
    b, t, d = xa.shape
    ff = wg.shape[2]
    tm = TOK_TILE
    return pl.pallas_call(
        functools.partial(_ffn_body, chunk=1024),
        grid=(b, t // tm),
        in_specs=[
            _tok_spec(d, tm), _mod_spec(d, n_lat_tiles, n_samples), _const_spec((1, d)),
            pl.BlockSpec((1, d, ff), lambda bi, j: (layer, 0, 0)),
            pl.BlockSpec((1, d, ff), lambda bi, j: (layer, 0, 0)),
            pl.BlockSpec((1, ff, d), lambda bi, j: (layer, 0, 0)),
        ],
        out_specs=_tok_spec(d, tm),
        out_shape=jax.ShapeDtypeStruct((b, t, d), F32),
        compiler_params=_params("parallel", "parallel"),
        name="ffn_dense",
    )(xa, mod_i, g2, wg, wu, wd)


def _router_body(x_ref, mod_ref, g2_ref, wr_ref, h_ref, r_ref):
    h = _norm_mod(x_ref[0], g2_ref[...], mod_ref[0, 3:4, :], mod_ref[0, 4:5, :])
    h_ref[0] = h
    h_hi, h_lo = _split_bf16(h)
    w_hi, w_lo = _split_bf16(wr_ref[...])
    logits = _dot_nt(w_hi, h_hi) + _dot_nt(w_hi, h_lo) + _dot_nt(w_lo, h_hi)
    eid = lax.broadcasted_iota(jnp.int32, logits.shape, 0)
    m1 = jnp.max(logits, axis=0, keepdims=True)
    i1 = jnp.min(jnp.where(logits == m1, eid, N_EXPERTS), axis=0, keepdims=True)
    rest = jnp.where(eid == i1, -jnp.inf, logits)
    m2 = jnp.max(rest, axis=0, keepdims=True)
    i2 = jnp.min(jnp.where(rest == m2, eid, N_EXPERTS), axis=0, keepdims=True)
    w1 = 1.0 / (1.0 + jnp.exp(m2 - m1))
    rows = lax.broadcasted_iota(jnp.int32, logits.shape, 0)
    out = jnp.where(rows == 0, i1.astype(F32), jnp.where(rows == 1, i2.astype(F32),
                    jnp.where(rows == 2, w1, jnp.where(rows == 3, 1.0 - w1, 0.0))))
    r_ref[0] = out


def _router(xa, mod_i, g2, wr_t, n_lat_tiles, n_samples):
    b, t, d = xa.shape
    tm = TOK_TILE
    return pl.pallas_call(
        _router_body,
        grid=(b, t // tm),
        in_specs=[
            _tok_spec(d, tm), _mod_spec(d, n_lat_tiles, n_samples), _const_spec((1, d)),
            _const_spec((N_EXPERTS, d)),
        ],
        out_specs=[_tok_spec(d, tm), pl.BlockSpec((1, N_EXPERTS, tm), lambda bi, j: (bi, 0, j))],
        out_shape=[jax.ShapeDtypeStruct((b, t, d), F32), jax.ShapeDtypeStruct((b, N_EXPERTS, t), F32)],
        compiler_params=_params("parallel", "parallel"),
        name="router",
    )(xa, mod_i, g2, wr_t)


def _row_copy(src_hbm, dst, sem, src_row, dst_row):
    return pltpu.make_async_copy(src_hbm.at[pl.ds(src_row, 1), :], dst.at[pl.ds(dst_row, 1), :], sem)


def _gather_rows(src_hbm, dst, sem, idx):
    rows = dst.shape[0]

    def start(g, c):
        for u in range(DMA_UNROLL):
            r = g * DMA_UNROLL + u
            _row_copy(src_hbm, dst, sem, idx(r), r).start(priority=u % 2)
        return c
    lax.fori_loop(0, rows // DMA_UNROLL, start, 0)


def _wait_rows(src_hbm, dst, sem):
    pltpu.make_async_copy(src_hbm.at[pl.ds(0, dst.shape[0]), :], dst, sem).wait()


def _expert_body(te_ref, nu_ref, rt_ref, rtn_ref, h_hbm, wg_ref, wu_ref, wd_ref, o_ref, xbuf, xbf, acc, sem):
    t = pl.program_id(0)
    f = pl.program_id(1)
    used = t < nu_ref[0]
    slot = t % 2

    @pl.when(jnp.logical_and(f == 0, jnp.logical_and(used, t == 0)))
    def _first_gather():
        _gather_rows(h_hbm, xbuf.at[0], sem.at[0], lambda r: rt_ref[0, 0, r])

    @pl.when(jnp.logical_and(used, f == 0))
    def _stage():
        _wait_rows(h_hbm, xbuf.at[slot], sem.at[slot])
        xbf[...] = xbuf[slot].astype(BF16)
        acc[...] = jnp.zeros_like(acc)

    @pl.when(jnp.logical_and(f == 0, t + 1 < nu_ref[0]))
    def _next_gather():
        _gather_rows(h_hbm, xbuf.at[1 - slot], sem.at[1 - slot], lambda r: rtn_ref[0, 0, r])

    @pl.when(used)
    def _compute():
        xb = xbf[...]
        a = _dot(xb, wg_ref[0, 0])
        mid = (a * jax.nn.sigmoid(a) * _dot(xb, wu_ref[0, 0])).astype(BF16)
        acc[...] += _dot(mid, wd_ref[0, 0])

    @pl.when(f == pl.num_programs(1) - 1)
    def _store():
        o_ref[...] = jnp.where(used, acc[...], 0.0)


def _experts(tile_expert, n_used, row_token, h_flat, wg, wu, wd, layer):
    n_rows = row_token.shape[0]
    d = h_flat.shape[1]
    ff = wg.shape[3]
    nt = n_rows // ROW_TILE
    rt3 = row_token.reshape(nt, 1, ROW_TILE)
    grid_spec = pltpu.PrefetchScalarGridSpec(
        num_scalar_prefetch=2,
        grid=(nt, ff // FF_TILE),
        in_specs=[
            pl.BlockSpec((1, 1, ROW_TILE), lambda t, f, te, nu: (t, 0, 0), memory_space=pltpu.SMEM),
            pl.BlockSpec((1, 1, ROW_TILE), lambda t, f, te, nu: (jnp.minimum(t + 1, nt - 1), 0, 0),
                         memory_space=pltpu.SMEM),
            pl.BlockSpec(memory_space=pl.ANY),
            pl.BlockSpec((1, 1, d, FF_TILE), lambda t, f, te, nu: (layer, te[t], 0, f)),
            pl.BlockSpec((1, 1, d, FF_TILE), lambda t, f, te, nu: (layer, te[t], 0, f)),
            pl.BlockSpec((1, 1, FF_TILE, d), lambda t, f, te, nu: (layer, te[t], f, 0)),
        ],
        out_specs=pl.BlockSpec((ROW_TILE, d), lambda t, f, te, nu: (t, 0)),
        scratch_shapes=[
            pltpu.VMEM((2, ROW_TILE, d), F32),
            pltpu.VMEM((ROW_TILE, d), BF16),
            pltpu.VMEM((ROW_TILE, d), F32),
            pltpu.SemaphoreType.DMA((2,)),
        ],
    )
    return pl.pallas_call(
        _expert_body,
        grid_spec=grid_spec,
        out_shape=jax.ShapeDtypeStruct((n_rows, d), F32),
        compiler_params=_params("arbitrary", "arbitrary"),
        name="experts",
    )(tile_expert, n_used, rt3, rt3, h_flat, wg, wu, wd)


def _combine_body(pos_ref, r_ref, x_ref, mod_ref, ye_hbm, o_ref, buf, sem):
    tm = x_ref.shape[1]
    for kk in range(2):
        _gather_rows(ye_hbm, buf.at[kk], sem.at[kk], lambda r, kk=kk: pos_ref[0, kk, r])
    eye = (lax.broadcasted_iota(jnp.int32, (tm, tm), 0) == lax.broadcasted_iota(jnp.int32, (tm, tm), 1)).astype(BF16)
    r_hi, r_lo = _split_bf16(r_ref[0])
    rcol = _dot_nt(eye, r_hi) + _dot_nt(eye, r_lo)
    for kk in range(2):
        _wait_rows(ye_hbm, buf.at[kk], sem.at[kk])
    mix = buf[0] * rcol[:, 2:3] + buf[1] * rcol[:, 3:4]
    o_ref[0] = x_ref[0] + mod_ref[0, 5:6, :] * mix


def _combine(pos, route, xa, mod_i, ye, n_lat_tiles, n_samples):
    b, t, d = xa.shape
    tm = TOK_TILE
    nj = t // tm
    return pl.pallas_call(
        _combine_body,
        grid=(b, nj),
        in_specs=[
            pl.BlockSpec((1, 2, tm), lambda bi, j: (bi * nj + j, 0, 0), memory_space=pltpu.SMEM),
            pl.BlockSpec((1, N_EXPERTS, tm), lambda bi, j: (bi, 0, j)),
            _tok_spec(d, tm), _mod_spec(d, n_lat_tiles, n_samples),
            pl.BlockSpec(memory_space=pl.ANY),
        ],
        out_specs=_tok_spec(d, tm),
        out_shape=jax.ShapeDtypeStruct((b, t, d), F32),
        scratch_shapes=[pltpu.VMEM((2, tm, d), F32), pltpu.SemaphoreType.DMA((2,))],
        compiler_params=_params("arbitrary", "arbitrary"),
        name="moe_combine",
    )(pos, route, xa, mod_i, ye)


def _moe(xa, mod_i, g2, wr_t, wg, wu, wd, layer, n_lat_tiles, n_samples):
    b, t, d = xa.shape
    h, route = _router(xa, mod_i, g2, wr_t, n_lat_tiles, n_samples)
    n_tok = b * t
    expert = route[:, 0:2, :].astype(jnp.int32)
    onehot = (expert.reshape(-1)[:, None] == jnp.arange(N_EXPERTS)[None, :]).astype(jnp.int32)
    counts = jnp.sum(onehot, axis=0)
    rank = jnp.sum((jnp.cumsum(onehot, axis=0) - onehot) * onehot, axis=1)
    padded = ((counts + ROW_TILE - 1) // ROW_TILE) * ROW_TILE
    ends = jnp.cumsum(padded)
    pos = ((ends - padded)[expert.reshape(-1)] + rank).reshape(b, 2, t)
    n_rows = ((2 * n_tok + N_EXPERTS * (ROW_TILE - 1)) // ROW_TILE + 1) * ROW_TILE
    token = jnp.broadcast_to(jnp.arange(b)[:, None, None] * t + jnp.arange(t)[None, None, :], (b, 2, t))
    row_token = jnp.zeros((n_rows,), jnp.int32).at[pos.reshape(-1)].set(token.reshape(-1))
    tile_start = jnp.arange(n_rows // ROW_TILE) * ROW_TILE
    tile_expert = jnp.minimum(jnp.sum(tile_start[:, None] >= ends[None, :], axis=1), N_EXPERTS - 1).astype(jnp.int32)
    n_used = (ends[-1:] // ROW_TILE).astype(jnp.int32)
    ye = _experts(tile_expert, n_used, row_token, h.reshape(n_tok, d), wg, wu, wd, layer)
    nj = t // TOK_TILE
    pos_tiles = pos.reshape(b, 2, nj, TOK_TILE).transpose(0, 2, 1, 3).reshape(b * nj, 2, TOK_TILE)
    return _combine(pos_tiles, route, xa, mod_i, ye, n_lat_tiles, n_samples)


def _rope_tables(seq, ctx):
    rows = seq // GRID_W
    row = jnp.repeat(jnp.arange(rows), GRID_W).astype(F32)
    col = jnp.tile(jnp.arange(GRID_W), rows).astype(F32)
    half = DA_HEAD_DIM // 2
    inv = ROPE_BASE ** (-jnp.arange(0, half, 2, dtype=F32) / half)
    ang = jnp.concatenate([row[:, None] * inv, row[:, None] * inv, col[:, None] * inv, col[:, None] * inv], axis=1)
    sign = jnp.tile(jnp.concatenate([-jnp.ones((half // 2,), F32), jnp.ones((half // 2,), F32)]), 2)
    cos = jnp.concatenate([jnp.cos(ang), jnp.ones((ctx, DA_HEAD_DIM), F32)], axis=0)
    sin = jnp.concatenate([jnp.sin(ang) * sign, jnp.zeros((ctx, DA_HEAD_DIM), F32)], axis=0)
    return jnp.tile(cos, (1, LANES // DA_HEAD_DIM)), jnp.tile(sin, (1, LANES // DA_HEAD_DIM))


def _filter_weights(w1, b1, w2, b2, w3, freq):
    col = lambda v: v.reshape(-1, 1)
    bands = jnp.linspace(1e-4, HY_BANDS - 1, HY_BANDS, dtype=F32)
    deltas = jnp.abs(jnp.linspace(HY_MIN_DECAY, HY_MAX_DECAY, HY_WIDTH, dtype=F32))
    half = HY_ORDER * HY_WIDTH
    return (w1[0:1].T, w1[1:1 + HY_BANDS].T, w1[1 + HY_BANDS:].T, col(b1), w2.T, col(b2), col(freq),
            w3[:, :half].T, w3[:, half:].T, col(bands), col(jnp.tile(deltas, HY_ORDER)))


def kernel(x, c, ctx, c_ctx, w_mod, b_mod, norm1_g, norm2_g, w_in, hy_conv_w, hy_conv_b, hy_f_w1, hy_f_b1, hy_f_w2, hy_f_b2, hy_f_w3, hy_f_freq, hy_skip, q_norm_g, k_norm_g, lam_q1, lam_k1, lam_q2, lam_k2, subln_g, sg_norm_g, sg_w, sg_b, w_branch, w_out, ffn_wg, ffn_wu, ffn_wd, router_w, moe_wg, moe_wu, moe_wd):
    n_samples, seq, d = x.shape
    n_ctx = ctx.shape[1]
    depth = w_mod.shape[0]
    assert seq % CONV_TILE == 0 and n_ctx % CONV_TILE == 0 and seq % GRID_W == 0 and seq % n_ctx == 0
    assert n_samples + 1 <= MOD_ROWS
    n_lat_tiles = seq // TOK_TILE
    tot = seq + n_ctx

    xa = jnp.concatenate([x, ctx], axis=1)
    cond = jnp.zeros((MOD_ROWS, d), F32).at[:n_samples].set(c).at[n_samples].set(c_ctx)
    mod = _modvec(cond, w_mod, b_mod).reshape(depth, MOD_ROWS, 6, d)

    cos_t, sin_t = _rope_tables(seq, n_ctx)
    group = jnp.arange(QK_COLS) // DA_HEAD_DIM
    gmat = (group[:, None] == group[None, :]).astype(BF16)
    w_in_b = w_in.astype(BF16)
    w_br_b = w_branch.astype(BF16)
    w_out_b = w_out.astype(BF16)
    ffn_b = [w.astype(BF16) for w in (ffn_wg, ffn_wu, ffn_wd)]
    moe_b = [w.astype(BF16) for w in (moe_wg, moe_wu, moe_wd)]

    for i in range(depth):
        lam_init = 0.8 - 0.6 * math.exp(-0.3 * i)
        g1 = norm1_g[i].reshape(1, d)
        g2 = norm2_g[i].reshape(1, d)
        hy, q, k, v, y_sg = _inproj(
            xa, mod[i], g1, w_in_b, i, cos_t, sin_t,
            jnp.tile(q_norm_g[i], QK_COLS // DA_HEAD_DIM).reshape(1, QK_COLS),
            jnp.tile(k_norm_g[i], QK_COLS // DA_HEAD_DIM).reshape(1, QK_COLS),
            gmat, sg_norm_g[i].reshape(1, SG_WIDTH), sg_w[i].astype(BF16),
            jnp.repeat(sg_b[i].T, SG_WIDTH // SG_GROUPS, axis=1), n_lat_tiles, n_samples)

        fw = _filter_weights(hy_f_w1[i], hy_f_b1[i], hy_f_w2[i], hy_f_b2[i], hy_f_w3[i], hy_f_freq[i])
        hm_l = _hyfilter(fw, seq).reshape(HY_ORDER, HY_WIDTH, 2 * seq // LANES, LANES)
        hm_c = _hyfilter(fw, n_ctx).reshape(HY_ORDER, HY_WIDTH, 2 * n_ctx // LANES, LANES)
        cw = hy_conv_w[i].reshape(3, 3, HY_WIDTH)
        cbias = hy_conv_b[i].reshape(3, HY_WIDTH)
        sc_tab = jnp.concatenate([cw.reshape(9, HY_WIDTH), cbias, hy_skip[i], jnp.zeros((2, HY_WIDTH), F32)],
                                 axis=0).T.reshape(-1)
        y_hy = _hyconv(sc_tab, jnp.transpose(hy, (2, 0, 1)), hm_l, hm_c, seq, n_ctx)
        y_hy = jnp.transpose(y_hy, (1, 2, 0)).astype(BF16)

        lam_p = jnp.stack([lam_q1[i], lam_k1[i], lam_q2[i], lam_k2[i]], axis=0)
        y_da = _attention(lam_p, q, k, v, subln_g[i].reshape(1, DA_V_DIM), lam_init, seq)

        xa = _merge(xa, mod[i], g1, w_in_b, i, y_hy, y_da, y_sg, w_br_b, w_out_b, n_lat_tiles, n_samples)

        j = i // 2
        if i % 2 == 0:
            xa = _ffn_dense(xa, mod[i], g2, *ffn_b, j, n_lat_tiles, n_samples)
        else:
            xa = _moe(xa, mod[i], g2, router_w[j].T, *moe_b, j, n_lat_tiles, n_samples)
    return xa[:, :seq]
```

```python
import functools
import math

import jax
import jax.numpy as jnp
from jax import lax
from jax.experimental import pallas as pl
from jax.experimental.pallas import tpu as pltpu

F32 = jnp.float32
BF16 = jnp.bfloat16
EPS = 1e-6

GRID_W = 64
HY_WIDTH = 512
HY_ORDER = 2
HY_BANDS = 16
HY_FFN = 64
HY_MAX_DECAY = math.log(1e-2) / 0.3
HY_MIN_DECAY = math.log(1e-2) / 1.5
DA_HEADS = 4
DA_HEAD_DIM = 64
DA_V_DIM = 2 * DA_HEAD_DIM
ROPE_BASE = 10000.0
SG_WIDTH = 512
SG_GROUPS = 4
SG_CHUNK = 128
N_EXPERTS = 8
HY_COLS = 3 * HY_WIDTH
QK_COLS = DA_HEADS * 2 * DA_HEAD_DIM
DA_COLS = 2 * QK_COLS + DA_HEADS * DA_V_DIM
SG_COLS = 2 * SG_WIDTH
PRE_COLS = HY_COLS + DA_COLS + SG_COLS

LANES = 128
TOK_TILE = 256
ATTN_ROWS = 256
CONV_TILE = 256
HY_CHANNELS_PER_STEP = 8
ROW_TILE = 896
FF_TILE = 512
DMA_UNROLL = 8
MOD_ROWS = 16
VMEM_LIMIT = 56 * 1024 * 1024


def _dot(a, b):
    return jnp.dot(a, b, preferred_element_type=F32)


def _dot_nt(a, b):
    return lax.dot_general(a, b, (((1,), (1,)), ((), ())), preferred_element_type=F32)


_hdot = functools.partial(jnp.dot, precision=lax.Precision.HIGHEST, preferred_element_type=F32)


def _split_bf16(a):
    hi = a.astype(BF16)
    return hi, (a - hi.astype(F32)).astype(BF16)


def _norm_mod(x, g, shift, scale):
    ms = jnp.mean(x * x, axis=-1, keepdims=True)
    return (x * lax.rsqrt(ms + EPS) * g) * (1.0 + scale) + shift


def _params(*sem):
    return pltpu.CompilerParams(dimension_semantics=sem, vmem_limit_bytes=VMEM_LIMIT)


def _modvec_body(cond_ref, w_ref, b_ref, o_ref):
    cnd = cond_ref[...]
    s = cnd * jax.nn.sigmoid(cnd)
    s_hi, s_lo = _split_bf16(s)
    w_hi, w_lo = _split_bf16(w_ref[0])
    o_ref[0] = _dot(s_hi, w_hi) + _dot(s_lo, w_hi) + _dot(s_hi, w_lo) + b_ref[0]


def _modvec(cond, w_mod, b_mod):
    depth, d, n = w_mod.shape
    tn = n // 4
    return pl.pallas_call(
        _modvec_body,
        grid=(depth, n // tn),
        in_specs=[
            pl.BlockSpec((MOD_ROWS, d), lambda i, j: (0, 0)),
            pl.BlockSpec((1, d, tn), lambda i, j: (i, 0, j)),
            pl.BlockSpec((1, 1, tn), lambda i, j: (i, 0, j)),
        ],
        out_specs=pl.BlockSpec((1, MOD_ROWS, tn), lambda i, j: (i, 0, j)),
        out_shape=jax.ShapeDtypeStruct((depth, MOD_ROWS, n), F32),
        compiler_params=_params("parallel", "parallel"),
        name="modvec",
    )(cond, w_mod, b_mod.reshape(depth, 1, n))


def _tok_spec(width, tm):
    return pl.BlockSpec((1, tm, width), lambda b, j: (b, j, 0))


def _const_spec(shape):
    nd = len(shape)
    return pl.BlockSpec(shape, lambda b, j: (0,) * nd)


def _mod_spec(d, n_lat_tiles, n_samples):
    return pl.BlockSpec((1, 6, d), lambda b, j: (jnp.where(j < n_lat_tiles, b, n_samples), 0, 0))


def _qk_norm_rope(a, g_tile, gmat, cos, sin_signed, first_half):
    ss = _dot((a * a).astype(BF16), gmat)
    an = a * lax.rsqrt(ss * (1.0 / DA_HEAD_DIM) + EPS) * g_tile
    outs = []
    for ci in range(QK_COLS // LANES):
        ch = an[:, ci * LANES:(ci + 1) * LANES]
        partner = jnp.where(first_half, pltpu.roll(ch, LANES - 16, 1), pltpu.roll(ch, 16, 1))
        outs.append(ch * cos + partner * sin_signed)
    return jnp.concatenate(outs, axis=1)


def _inproj_body(x_ref, mod_ref, g1_ref, w_ref, cos_ref, sin_ref, qg_ref, kg_ref, gmat_ref,
                 sgg_ref, sgw_ref, sgb_ref, hy_ref, q_ref, k_ref, v_ref, sg_ref):
    tm = x_ref.shape[1]
    h = _norm_mod(x_ref[0], g1_ref[...], mod_ref[0, 0:1, :], mod_ref[0, 1:2, :]).astype(BF16)
    hy_ref[0] = _dot(h, w_ref[0, :, 0:HY_COLS])

    lane = lax.broadcasted_iota(jnp.int32, (1, LANES), 1)
    first_half = (lane % 32) < 16
    cos = cos_ref[...]
    sin_signed = sin_ref[...]
    gmat = gmat_ref[...]
    o = HY_COLS
    q = _qk_norm_rope(_dot(h, w_ref[0, :, o:o + QK_COLS]), qg_ref[...], gmat, cos, sin_signed, first_half)
    q_ref[0] = (q * (DA_HEAD_DIM ** -0.5 * math.log2(math.e))).astype(BF16)
    o += QK_COLS
    k = _qk_norm_rope(_dot(h, w_ref[0, :, o:o + QK_COLS]), kg_ref[...], gmat, cos, sin_signed, first_half)
    k_ref[0] = k.astype(BF16)
    o += QK_COLS
    v_ref[0] = _dot(h, w_ref[0, :, o:o + DA_HEADS * DA_V_DIM]).astype(BF16)
    o += DA_HEADS * DA_V_DIM

    z = jax.nn.gelu(_dot(h, w_ref[0, :, o:o + SG_COLS]))
    u = z[:, :SG_WIDTH]
    vv = z[:, SG_WIDTH:]
    vn = (vv * lax.rsqrt(jnp.mean(vv * vv, axis=-1, keepdims=True) + EPS) * sgg_ref[...]).astype(BF16)
    gw = SG_WIDTH // SG_GROUPS
    for ch in range(tm // SG_CHUNK):
        r0 = ch * SG_CHUNK
        cols = []
        for g in range(SG_GROUPS):
            s = _dot(sgw_ref[g], vn[r0:r0 + SG_CHUNK, g * gw:(g + 1) * gw]) + sgb_ref[:, g * gw:(g + 1) * gw]
            cols.append(u[r0:r0 + SG_CHUNK, g * gw:(g + 1) * gw] * s)
        sg_ref[0, r0:r0 + SG_CHUNK, :] = jnp.concatenate(cols, axis=1).astype(BF16)


def _inproj(xa, mod_i, g1, w_in, layer, cos_t, sin_t, qg, kg, gmat, sgg, sgw, sgb, n_lat_tiles, n_samples):
    b, t, d = xa.shape
    tm = TOK_TILE
    outs = [jax.ShapeDtypeStruct((b, t, HY_COLS), F32)] + [jax.ShapeDtypeStruct((b, t, QK_COLS), BF16)] * 4
    return pl.pallas_call(
        _inproj_body,
        grid=(b, t // tm),
        in_specs=[
            _tok_spec(d, tm),
            _mod_spec(d, n_lat_tiles, n_samples),
            _const_spec((1, d)),
            pl.BlockSpec((1, d, PRE_COLS), lambda bi, j: (layer, 0, 0)),
            pl.BlockSpec((tm, LANES), lambda bi, j: (j, 0)),
            pl.BlockSpec((tm, LANES), lambda bi, j: (j, 0)),
            _const_spec((1, QK_COLS)),
            _const_spec((1, QK_COLS)),
            _const_spec((QK_COLS, QK_COLS)),
            _const_spec((1, SG_WIDTH)),
            _const_spec((SG_GROUPS, SG_CHUNK, SG_CHUNK)),
            _const_spec((SG_CHUNK, SG_WIDTH)),
        ],
        out_specs=[_tok_spec(HY_COLS, tm)] + [_tok_spec(QK_COLS, tm)] * 4,
        out_shape=outs,
        compiler_params=_params("parallel", "parallel"),
        name="inproj",
    )(xa, mod_i, g1, w_in, cos_t, sin_t, qg, kg, gmat, sgg, sgw, sgb)


def _hyfilter_body(w1t_ref, w1c_ref, w1s_ref, b1_ref, w2_ref, b2_ref, fr_ref, w3f_ref, w3b_ref,
                   bands_ref, dl_ref, o_ref, *, seq):
    n = 2 * seq
    xi = lax.broadcasted_iota(jnp.int32, (1, n), 1)
    lag = xi - (seq - 1)
    pos = jnp.abs(lag).astype(F32)
    t = pos / (seq - 1)
    ang = 2.0 * math.pi * pos * bands_ref[...] / seq
    fr = fr_ref[...]
    z1 = w1t_ref[...] * t + _hdot(w1c_ref[...], jnp.cos(ang)) + _hdot(w1s_ref[...], jnp.sin(ang)) + b1_ref[...]
    h1 = jnp.sin(fr * z1)
    h2 = jnp.sin(fr * (_hdot(w2_ref[...], h1) + b2_ref[...]))
    fwd = _hdot(w3f_ref[...], h2)
    bwd = _hdot(w3b_ref[...], h2)
    window = jnp.exp(-t * dl_ref[...])
    k = jnp.where(lag > 0, fwd, jnp.where(lag < 0, bwd, fwd + bwd)) * window
    k = jnp.where(xi < n - 1, k, 0.0)
    o_ref[...] = k / jnp.sum(jnp.abs(k), axis=-1, keepdims=True)


def _hyfilter(fw, seq):
    rows = HY_ORDER * HY_WIDTH
    rb = 256
    n = 2 * seq
    small = lambda shape: pl.BlockSpec(shape, lambda i: (0, 0))
    return pl.pallas_call(
        functools.partial(_hyfilter_body, seq=seq),
        grid=(rows // rb,),
        in_specs=[
            small((HY_FFN, 1)), small((HY_FFN, HY_BANDS)), small((HY_FFN, HY_BANDS)), small((HY_FFN, 1)),
            small((HY_FFN, HY_FFN)), small((HY_FFN, 1)), small((HY_FFN, 1)),
            pl.BlockSpec((rb, HY_FFN), lambda i: (i, 0)),
            pl.BlockSpec((rb, HY_FFN), lambda i: (i, 0)),
            small((HY_BANDS, 1)),
            pl.BlockSpec((rb, 1), lambda i: (i, 0)),
        ],
        out_specs=pl.BlockSpec((rb, n), lambda i: (i, 0)),
        out_shape=jax.ShapeDtypeStruct((rows, n), F32),
        compiler_params=_params("parallel"),
        name="hyfilter",
    )(*fw)


def _hyconv_body(sc_ref, z_ref, x1_ref, x2_ref, hml_ref, hmc_ref, o_ref, big_l, big_c, *, seq, ctx, cb):
    tot = seq + ctx
    lane = lax.broadcasted_iota(jnp.int32, (1, tot), 1)
    has_prev = jnp.logical_and(lane != 0, lane != seq)
    has_next = jnp.logical_and(lane != seq - 1, lane != tot - 1)
    row = lax.broadcasted_iota(jnp.int32, (LANES, LANES), 0)
    col = lax.broadcasted_iota(jnp.int32, (LANES, LANES), 1)
    lower = col <= row
    c0 = pl.program_id(0) * cb

    def short_conv(p, base, part):
        prev = jnp.where(has_prev, pltpu.roll(p, 1, 1), 0.0)
        nxt = jnp.where(has_next, pltpu.roll(p, tot - 1, 1), 0.0)
        return (sc_ref[base + 9 + part] + sc_ref[base + part] * prev
                + sc_ref[base + 3 + part] * p + sc_ref[base + 6 + part] * nxt)

    def build(hm_ref, o, ci, big_ref):
        n_rows = hm_ref.shape[2]
        prev = None
        for rp in range(n_rows - 1, -1, -1):
            r = hm_ref[o, ci, rp:rp + 1, :]
            cur = pltpu.roll(jnp.broadcast_to(r, (LANES, LANES)), 1, 1, stride=1, stride_axis=0)
            if prev is not None:
                rho = n_rows - 2 - rp
                big_ref[o, rho * LANES:(rho + 1) * LANES, :] = jnp.where(lower, cur, prev).astype(BF16)
            prev = cur

    def long_conv(zz, big_ref, o, length, base):
        nblk = length // CONV_TILE
        mid = (2 * nblk - 1) * LANES
        ys = [None] * nblk
        for dd in range(-(nblk - 1), nblk):
            r0 = mid - CONV_TILE * dd
            w = jnp.concatenate([big_ref[o, r0:r0 + CONV_TILE, :],
                                 big_ref[o, r0 - LANES:r0 - LANES + CONV_TILE, :]], axis=1)
            js = list(range(max(0, -dd), min(nblk, nblk - dd)))
            lhs = jnp.concatenate([zz[:, base + j * CONV_TILE:base + (j + 1) * CONV_TILE] for j in js], axis=0)
            out = _dot(lhs.astype(BF16), w)
            nb = zz.shape[0]
            for kk, j in enumerate(js):
                piece = out[kk * nb:(kk + 1) * nb]
                ys[j + dd] = piece if ys[j + dd] is None else ys[j + dd] + piece
        return ys

    def chan(ci, carry):
        base = (c0 + ci) * 16
        z = short_conv(z_ref[ci], base, 0)
        gates = (short_conv(x1_ref[ci], base, 1), short_conv(x2_ref[ci], base, 2))
        for o in range(HY_ORDER):
            build(hml_ref, o, ci, big_l)
            build(hmc_ref, o, ci, big_c)
        for o in range(HY_ORDER):
            y = jnp.concatenate(long_conv(z, big_l, o, seq, 0) + long_conv(z, big_c, o, ctx, seq), axis=1)
            z = gates[o] * (y + sc_ref[base + 12 + o] * z)
        o_ref[ci] = z
        return carry

    lax.fori_loop(0, cb, chan, 0, unroll=4)


def _hyconv(sc_tab, u3, hm_l, hm_c, seq, ctx):
    _, b, tot = u3.shape
    cb = HY_CHANNELS_PER_STEP
    nblk = HY_WIDTH // cb
    slab = lambda part: pl.BlockSpec((cb, b, tot), lambda i: (part * nblk + i, 0, 0))
    return pl.pallas_call(
        functools.partial(_hyconv_body, seq=seq, ctx=ctx, cb=cb),
        grid=(nblk,),
        in_specs=[
            pl.BlockSpec(memory_space=pltpu.SMEM),
            slab(0), slab(1), slab(2),
            pl.BlockSpec((HY_ORDER, cb, hm_l.shape[2], LANES), lambda i: (0, i, 0, 0)),
            pl.BlockSpec((HY_ORDER, cb, hm_c.shape[2], LANES), lambda i: (0, i, 0, 0)),
        ],
        out_specs=pl.BlockSpec((cb, b, tot), lambda i: (i, 0, 0)),
        out_shape=jax.ShapeDtypeStruct((HY_WIDTH, b, tot), F32),
        scratch_shapes=[
            pltpu.VMEM((HY_ORDER, (hm_l.shape[2] - 1) * LANES, LANES), BF16),
            pltpu.VMEM((HY_ORDER, (hm_c.shape[2] - 1) * LANES, LANES), BF16),
        ],
        compiler_params=_params("parallel"),
        name="hyconv",
    )(sc_tab, u3, u3, u3, hm_l, hm_c)


def _attn_body(lam_ref, q_ref, k_ref, v_ref, sub_ref, o_ref, *, lam_init):
    tq = q_ref.shape[1]
    k = k_ref[0]
    v = v_ref[0]
    lane = lax.broadcasted_iota(jnp.int32, (1, 2 * DA_HEAD_DIM), 1)
    first = lane < DA_HEAD_DIM
    lp = lam_ref[...]
    lam = (jnp.exp(jnp.sum(lp[0:1] * lp[1:2], keepdims=True))
           - jnp.exp(jnp.sum(lp[2:3] * lp[3:4], keepdims=True)) + lam_init)
    sub = sub_ref[...] * (1.0 - lam_init)
    for r0 in range(0, tq, ATTN_ROWS):
        q = q_ref[0, r0:r0 + ATTN_ROWS, :]
        zero = jnp.zeros_like(q)
        s1 = _dot_nt(jnp.where(first, q, zero), k)
        s2 = _dot_nt(jnp.where(first, zero, q), k)
        p1 = jnp.exp2(s1 - jnp.max(s1, axis=-1, keepdims=True))
        p2 = jnp.exp2(s2 - jnp.max(s2, axis=-1, keepdims=True))
        w = p1 * (1.0 / jnp.sum(p1, axis=-1, keepdims=True)) - p2 * (lam / jnp.sum(p2, axis=-1, keepdims=True))
        o = _dot(w.astype(BF16), v)
        on = o * lax.rsqrt(jnp.mean(o * o, axis=-1, keepdims=True) + EPS) * sub
        o_ref[0, r0:r0 + ATTN_ROWS, :] = on.astype(BF16)


def _attention(lam_p, q, k, v, sub, lam_init, seq):
    b, tot, _ = q.shape
    ctx = tot - seq
    tq = TOK_TILE
    hw = 2 * DA_HEAD_DIM
    body = functools.partial(_attn_body, lam_init=lam_init)
    small = [pl.BlockSpec((4, DA_HEAD_DIM), lambda bi, h, j: (0, 0)), pl.BlockSpec((1, hw), lambda bi, h, j: (0, 0))]
    lat = pl.pallas_call(
        body,
        grid=(b, DA_HEADS, seq // tq),
        in_specs=[
            small[0],
            pl.BlockSpec((1, tq, hw), lambda bi, h, j: (bi, j, h)),
            pl.BlockSpec((1, tot, hw), lambda bi, h, j: (bi, 0, h)),
            pl.BlockSpec((1, tot, hw), lambda bi, h, j: (bi, 0, h)),
            small[1],
        ],
        out_specs=pl.BlockSpec((1, tq, hw), lambda bi, h, j: (bi, j, h)),
        out_shape=jax.ShapeDtypeStruct((b, seq, DA_HEADS * DA_V_DIM), BF16),
        compiler_params=_params("parallel", "parallel", "parallel"),
        name="diffattn",
    )(lam_p, q, k, v, sub)
    q0 = seq // tq
    k0 = seq // ctx
    ctx_out = pl.pallas_call(
        body,
        grid=(b, DA_HEADS, ctx // tq),
        in_specs=[
            small[0],
            pl.BlockSpec((1, tq, hw), lambda bi, h, j: (bi, q0 + j, h)),
            pl.BlockSpec((1, ctx, hw), lambda bi, h, j: (bi, k0, h)),
            pl.BlockSpec((1, ctx, hw), lambda bi, h, j: (bi, k0, h)),
            small[1],
        ],
        out_specs=pl.BlockSpec((1, tq, hw), lambda bi, h, j: (bi, j, h)),
        out_shape=jax.ShapeDtypeStruct((b, ctx, DA_HEADS * DA_V_DIM), BF16),
        compiler_params=_params("parallel", "parallel", "parallel"),
        name="diffattn_ctx",
    )(lam_p, q, k, v, sub)
    return jnp.concatenate([lat, ctx_out], axis=1)


def _merge_body(x_ref, mod_ref, g1_ref, wg0_ref, wg1_ref, wg2_ref, yh_ref, yd_ref, ys_ref, wb_ref, wo_ref, o_ref):
    x = x_ref[0]
    h = _norm_mod(x, g1_ref[...], mod_ref[0, 0:1, :], mod_ref[0, 1:2, :]).astype(BF16)
    acc = None
    for n, (wg_ref, y_ref) in enumerate(((wg0_ref, yh_ref), (wg1_ref, yd_ref), (wg2_ref, ys_ref))):
        gate = jax.nn.sigmoid(_dot(h, wg_ref[0]))
        term = gate * _dot(y_ref[0], wb_ref[0, n])
        acc = term if acc is None else acc + term
    o_ref[0] = x + mod_ref[0, 2:3, :] * _dot(acc.astype(BF16), wo_ref[0])


def _merge(xa, mod_i, g1, w_in, layer, y_hy, y_da, y_sg, w_br, w_out, n_lat_tiles, n_samples):
    b, t, d = xa.shape
    tm = TOK_TILE
    assert PRE_COLS % d == 0
    gate_spec = lambda n: pl.BlockSpec((1, d, d), lambda bi, j: (layer, 0, PRE_COLS // d + n))
    return pl.pallas_call(
        _merge_body,
        grid=(b, t // tm),
        in_specs=[
            _tok_spec(d, tm), _mod_spec(d, n_lat_tiles, n_samples), _const_spec((1, d)),
            gate_spec(0), gate_spec(1), gate_spec(2),
            _tok_spec(HY_WIDTH, tm), _tok_spec(HY_WIDTH, tm), _tok_spec(HY_WIDTH, tm),
            pl.BlockSpec((1, 3, HY_WIDTH, d), lambda bi, j: (layer, 0, 0, 0)),
            pl.BlockSpec((1, d, d), lambda bi, j: (layer, 0, 0)),
        ],
        out_specs=_tok_spec(d, tm),
        out_shape=jax.ShapeDtypeStruct((b, t, d), F32),
        compiler_params=_params("parallel", "parallel"),
        name="merge",
    )(xa, mod_i, g1, w_in, w_in, w_in, y_hy, y_da, y_sg, w_br, w_out)


def _ffn_body(x_ref, mod_ref, g2_ref, wg_ref, wu_ref, wd_ref, o_ref, *, chunk):
    x = x_ref[0]
    h = _norm_mod(x, g2_ref[...], mod_ref[0, 3:4, :], mod_ref[0, 4:5, :]).astype(BF16)
    ff = wg_ref.shape[2]
    acc = None
    for f0 in range(0, ff, chunk):
        f1 = min(ff, f0 + chunk)
        a = _dot(h, wg_ref[0, :, f0:f1])
        mid = (a * jax.nn.sigmoid(a) * _dot(h, wu_ref[0, :, f0:f1])).astype(BF16)
        term = _dot(mid, wd_ref[0, f0:f1, :])
        acc = term if acc is None else acc + term
    o_ref[0] = x + mod_ref[0, 5:6, :] * acc


def _ffn_dense(xa, mod_i, g2, wg, wu, wd, layer, n_lat_tiles, n_samples):
    b, t, d = xa.shape
    ff = wg.shape[2]
    tm = TOK_TILE
    return pl.pallas_call(
        functools.partial(_ffn_body, chunk=1024),
        grid=(b, t // tm),
        in_specs=[
            _tok_spec(d, tm), _mod_spec(d, n_lat_tiles, n_samples), _const_spec((1, d)),
            pl.BlockSpec((1, d, ff), lambda bi, j: (layer, 0, 0)),
            pl.BlockSpec((1, d, ff), lambda bi, j: (layer, 0, 0)),
            pl.BlockSpec((1, ff, d), lambda bi, j: (layer, 0, 0)),
        ],
        out_specs=_tok_spec(d, tm),
        out_shape=jax.ShapeDtypeStruct((b, t, d), F32),
        compiler_params=_params("parallel", "parallel"),
        name="ffn_dense",
    )(xa, mod_i, g2, wg, wu, wd)


def _router_body(x_ref, mod_ref, g2_ref, wr_ref, h_ref, r_ref):
    h = _norm_mod(x_ref[0], g2_ref[...], mod_ref[0, 3:4, :], mod_ref[0, 4:5, :])
    h_ref[0] = h
    h_hi, h_lo = _split_bf16(h)
    w_hi, w_lo = _split_bf16(wr_ref[...])
    logits = _dot_nt(w_hi, h_hi) + _dot_nt(w_hi, h_lo) + _dot_nt(w_lo, h_hi)
    eid = lax.broadcasted_iota(jnp.int32, logits.shape, 0)
    m1 = jnp.max(logits, axis=0, keepdims=True)
    i1 = jnp.min(jnp.where(logits == m1, eid, N_EXPERTS), axis=0, keepdims=True)
    rest = jnp.where(eid == i1, -jnp.inf, logits)
    m2 = jnp.max(rest, axis=0, keepdims=True)
    i2 = jnp.min(jnp.where(rest == m2, eid, N_EXPERTS), axis=0, keepdims=True)
    w1 = 1.0 / (1.0 + jnp.exp(m2 - m1))
    rows = lax.broadcasted_iota(jnp.int32, logits.shape, 0)
    out = jnp.where(rows == 0, i1.astype(F32), jnp.where(rows == 1, i2.astype(F32),
                    jnp.where(rows == 2, w1, jnp.where(rows == 3, 1.0 - w1, 0.0))))
    r_ref[0] = out


def _router(xa, mod_i, g2, wr_t, n_lat_tiles, n_samples):
    b, t, d = xa.shape
    tm = TOK_TILE
    return pl.pallas_call(
        _router_body,
        grid=(b, t // tm),
        in_specs=[
            _tok_spec(d, tm), _mod_spec(d, n_lat_tiles, n_samples), _const_spec((1, d)),
            _const_spec((N_EXPERTS, d)),
        ],
        out_specs=[_tok_spec(d, tm), pl.BlockSpec((1, N_EXPERTS, tm), lambda bi, j: (bi, 0, j))],
        out_shape=[jax.ShapeDtypeStruct((b, t, d), F32), jax.ShapeDtypeStruct((b, N_EXPERTS, t), F32)],
        compiler_params=_params("parallel", "parallel"),
        name="router",
    )(xa, mod_i, g2, wr_t)


def _row_copy(src_hbm, dst, sem, src_row, dst_row):
    return pltpu.make_async_copy(src_hbm.at[pl.ds(src_row, 1), :], dst.at[pl.ds(dst_row, 1), :], sem)


def _gather_rows(src_hbm, dst, sem, idx):
    rows = dst.shape[0]

    def start(g, c):
        for u in range(DMA_UNROLL):
            r = g * DMA_UNROLL + u
            _row_copy(src_hbm, dst, sem, idx(r), r).start(priority=u % 2)
        return c
    lax.fori_loop(0, rows // DMA_UNROLL, start, 0)


def _wait_rows(src_hbm, dst, sem):
    pltpu.make_async_copy(src_hbm.at[pl.ds(0, dst.shape[0]), :], dst, sem).wait()


def _expert_body(te_ref, nu_ref, rt_ref, rtn_ref, h_hbm, wg_ref, wu_ref, wd_ref, o_ref, xbuf, xbf, acc, sem, *, chunk):
    t = pl.program_id(0)
    f = pl.program_id(1)
    last_f = pl.num_programs(1) - 1
    n_used = nu_ref[0]
    used = t < n_used
    slot = t % 2

    @pl.when(jnp.logical_and(f == 0, jnp.logical_and(used, t == 0)))
    def _first_gather():
        _gather_rows(h_hbm, xbuf.at[0], sem.at[0], lambda r: rt_ref[0, 0, r])

    @pl.when(jnp.logical_and(used, f == 0))
    def _stage():
        _wait_rows(h_hbm, xbuf.at[slot], sem.at[slot])
        xbf[...] = xbuf[slot].astype(BF16)
        acc[...] = jnp.zeros_like(acc)

    @pl.when(used)
    def _compute():
        xb = xbf[...]
        a = _dot(xb, wg_ref[0, 0])
        b = _dot(xb, wu_ref[0, 0])
        for u in range(chunk):
            r = f * chunk + u
            _row_copy(h_hbm, xbuf.at[1 - slot], sem.at[1 - slot], rtn_ref[0, 0, r], r).start(priority=u % 2)
        mid = (a * jax.nn.sigmoid(a) * b).astype(BF16)
        acc[...] += _dot(mid, wd_ref[0, 0])

    @pl.when(f == last_f)
    def _store():
        o_ref[...] = jnp.where(used, acc[...], 0.0)

    @pl.when(jnp.logical_and(f == last_f, t == n_used - 1))
    def _drain():
        _wait_rows(h_hbm, xbuf.at[1 - slot], sem.at[1 - slot])


def _experts(tile_expert, n_used, row_token, h_flat, wg, wu, wd, layer):
    n_rows = row_token.shape[0]
    d = h_flat.shape[1]
    ff = wg.shape[3]
    nt = n_rows // ROW_TILE
    rt3 = row_token.reshape(nt, 1, ROW_TILE)
    grid_spec = pltpu.PrefetchScalarGridSpec(
        num_scalar_prefetch=2,
        grid=(nt, ff // FF_TILE),
        in_specs=[
            pl.BlockSpec((1, 1, ROW_TILE), lambda t, f, te, nu: (t, 0, 0), memory_space=pltpu.SMEM),
            pl.BlockSpec((1, 1, ROW_TILE), lambda t, f, te, nu: (jnp.minimum(t + 1, nt - 1), 0, 0),
                         memory_space=pltpu.SMEM),
            pl.BlockSpec(memory_space=pl.ANY),
            pl.BlockSpec((1, 1, d, FF_TILE), lambda t, f, te, nu: (layer, te[t], 0, f)),
            pl.BlockSpec((1, 1, d, FF_TILE), lambda t, f, te, nu: (layer, te[t], 0, f)),
            pl.BlockSpec((1, 1, FF_TILE, d), lambda t, f, te, nu: (layer, te[t], f, 0)),
        ],
        out_specs=pl.BlockSpec((ROW_TILE, d), lambda t, f, te, nu: (t, 0)),
        scratch_shapes=[
            pltpu.VMEM((2, ROW_TILE, d), F32),
            pltpu.VMEM((ROW_TILE, d), BF16),
            pltpu.VMEM((ROW_TILE, d), F32),
            pltpu.SemaphoreType.DMA((2,)),
        ],
    )
    assert ROW_TILE % (ff // FF_TILE) == 0
    return pl.pallas_call(
        functools.partial(_expert_body, chunk=ROW_TILE // (ff // FF_TILE)),
        grid_spec=grid_spec,
        out_shape=jax.ShapeDtypeStruct((n_rows, d), F32),
        compiler_params=_params("arbitrary", "arbitrary"),
        name="experts",
    )(tile_expert, n_used, rt3, rt3, h_flat, wg, wu, wd)


def _combine_body(pos_ref, posn_ref, r_ref, x_ref, mod_ref, ye_hbm, o_ref, buf, sem):
    tm = x_ref.shape[1]
    step = pl.program_id(0) * pl.num_programs(1) + pl.program_id(1)
    n_steps = pl.num_programs(0) * pl.num_programs(1)
    slot = step % 2

    def gather(p_ref, s):
        for kk in range(2):
            _gather_rows(ye_hbm, buf.at[s, kk], sem.at[s, kk], lambda r, kk=kk: p_ref[0, kk, r])

    @pl.when(step == 0)
    def _first():
        gather(pos_ref, 0)

    @pl.when(step + 1 < n_steps)
    def _next():
        gather(posn_ref, 1 - slot)

    eye = (lax.broadcasted_iota(jnp.int32, (tm, tm), 0) == lax.broadcasted_iota(jnp.int32, (tm, tm), 1)).astype(BF16)
    r_hi, r_lo = _split_bf16(r_ref[0])
    rcol = _dot_nt(eye, r_hi) + _dot_nt(eye, r_lo)
    for kk in range(2):
        _wait_rows(ye_hbm, buf.at[slot, kk], sem.at[slot, kk])
    mix = buf[slot, 0] * rcol[:, 2:3] + buf[slot, 1] * rcol[:, 3:4]
    o_ref[0] = x_ref[0] + mod_ref[0, 5:6, :] * mix


def _combine(pos, route, xa, mod_i, ye, n_lat_tiles, n_samples):
    b, t, d = xa.shape
    tm = TOK_TILE
    nj = t // tm
    return pl.pallas_call(
        _combine_body,
        grid=(b, nj),
        in_specs=[
            pl.BlockSpec((1, 2, tm), lambda bi, j: (bi * nj + j, 0, 0), memory_space=pltpu.SMEM),
            pl.BlockSpec((1, 2, tm), lambda bi, j: (jnp.minimum(bi * nj + j + 1, b * nj - 1), 0, 0),
                         memory_space=pltpu.SMEM),
            pl.BlockSpec((1, N_EXPERTS, tm), lambda bi, j: (bi, 0, j)),
            _tok_spec(d, tm), _mod_spec(d, n_lat_tiles, n_samples),
            pl.BlockSpec(memory_space=pl.ANY),
        ],
        out_specs=_tok_spec(d, tm),
        out_shape=jax.ShapeDtypeStruct((b, t, d), F32),
        scratch_shapes=[pltpu.VMEM((2, 2, tm, d), F32), pltpu.SemaphoreType.DMA((2, 2))],
        compiler_params=_params("arbitrary", "arbitrary"),
        name="moe_combine",
    )(pos, pos, route, xa, mod_i, ye)


def _moe(xa, mod_i, g2, wr_t, wg, wu, wd, layer, n_lat_tiles, n_samples):
    b, t, d = xa.shape
    h, route = _router(xa, mod_i, g2, wr_t, n_lat_tiles, n_samples)
    n_tok = b * t
    expert = route[:, 0:2, :].astype(jnp.int32)
    onehot = (expert.reshape(-1)[:, None] == jnp.arange(N_EXPERTS)[None, :]).astype(jnp.int32)
    counts = jnp.sum(onehot, axis=0)
    rank = jnp.sum((jnp.cumsum(onehot, axis=0) - onehot) * onehot, axis=1)
    padded = ((counts + ROW_TILE - 1) // ROW_TILE) * ROW_TILE
    ends = jnp.cumsum(padded)
    pos = ((ends - padded)[expert.reshape(-1)] + rank).reshape(b, 2, t)
    n_rows = ((2 * n_tok + N_EXPERTS * (ROW_TILE - 1)) // ROW_TILE + 1) * ROW_TILE
    token = jnp.broadcast_to(jnp.arange(b)[:, None, None] * t + jnp.arange(t)[None, None, :], (b, 2, t))
    row_token = jnp.zeros((n_rows,), jnp.int32).at[pos.reshape(-1)].set(token.reshape(-1))
    tile_start = jnp.arange(n_rows // ROW_TILE) * ROW_TILE
    tile_expert = jnp.minimum(jnp.sum(tile_start[:, None] >= ends[None, :], axis=1), N_EXPERTS - 1).astype(jnp.int32)
    n_used = (ends[-1:] // ROW_TILE).astype(jnp.int32)
    ye = _experts(tile_expert, n_used, row_token, h.reshape(n_tok, d), wg, wu, wd, layer)
    nj = t // TOK_TILE
    pos_tiles = pos.reshape(b, 2, nj, TOK_TILE).transpose(0, 2, 1, 3).reshape(b * nj, 2, TOK_TILE)
    return _combine(pos_tiles, route, xa, mod_i, ye, n_lat_tiles, n_samples)


def _rope_tables(seq, ctx):
    rows = seq // GRID_W
    row = jnp.repeat(jnp.arange(rows), GRID_W).astype(F32)
    col = jnp.tile(jnp.arange(GRID_W), rows).astype(F32)
    half = DA_HEAD_DIM // 2
    inv = ROPE_BASE ** (-jnp.arange(0, half, 2, dtype=F32) / half)
    ang = jnp.concatenate([row[:, None] * inv, row[:, None] * inv, col[:, None] * inv, col[:, None] * inv], axis=1)
    sign = jnp.tile(jnp.concatenate([-jnp.ones((half // 2,), F32), jnp.ones((half // 2,), F32)]), 2)
    cos = jnp.concatenate([jnp.cos(ang), jnp.ones((ctx, DA_HEAD_DIM), F32)], axis=0)
    sin = jnp.concatenate([jnp.sin(ang) * sign, jnp.zeros((ctx, DA_HEAD_DIM), F32)], axis=0)
    return jnp.tile(cos, (1, LANES // DA_HEAD_DIM)), jnp.tile(sin, (1, LANES // DA_HEAD_DIM))


def _filter_weights(w1, b1, w2, b2, w3, freq):
    col = lambda v: v.reshape(-1, 1)
    bands = jnp.linspace(1e-4, HY_BANDS - 1, HY_BANDS, dtype=F32)
    deltas = jnp.abs(jnp.linspace(HY_MIN_DECAY, HY_MAX_DECAY, HY_WIDTH, dtype=F32))
    half = HY_ORDER * HY_WIDTH
    return (w1[0:1].T, w1[1:1 + HY_BANDS].T, w1[1 + HY_BANDS:].T, col(b1), w2.T, col(b2), col(freq),
            w3[:, :half].T, w3[:, half:].T, col(bands), col(jnp.tile(deltas, HY_ORDER)))


def kernel(x, c, ctx, c_ctx, w_mod, b_mod, norm1_g, norm2_g, w_in, hy_conv_w, hy_conv_b, hy_f_w1, hy_f_b1, hy_f_w2, hy_f_b2, hy_f_w3, hy_f_freq, hy_skip, q_norm_g, k_norm_g, lam_q1, lam_k1, lam_q2, lam_k2, subln_g, sg_norm_g, sg_w, sg_b, w_branch, w_out, ffn_wg, ffn_wu, ffn_wd, router_w, moe_wg, moe_wu, moe_wd):
    n_samples, seq, d = x.shape
    n_ctx = ctx.shape[1]
    depth = w_mod.shape[0]
    assert seq % CONV_TILE == 0 and n_ctx % CONV_TILE == 0 and seq % GRID_W == 0 and seq % n_ctx == 0
    assert n_samples + 1 <= MOD_ROWS
    n_lat_tiles = seq // TOK_TILE
    tot = seq + n_ctx

    xa = jnp.concatenate([x, ctx], axis=1)
    cond = jnp.zeros((MOD_ROWS, d), F32).at[:n_samples].set(c).at[n_samples].set(c_ctx)
    mod = _modvec(cond, w_mod, b_mod).reshape(depth, MOD_ROWS, 6, d)

    cos_t, sin_t = _rope_tables(seq, n_ctx)
    group = jnp.arange(QK_COLS) // DA_HEAD_DIM
    gmat = (group[:, None] == group[None, :]).astype(BF16)
    w_in_b = w_in.astype(BF16)
    w_br_b = w_branch.astype(BF16)
    w_out_b = w_out.astype(BF16)
    ffn_b = [w.astype(BF16) for w in (ffn_wg, ffn_wu, ffn_wd)]
    moe_b = [w.astype(BF16) for w in (moe_wg, moe_wu, moe_wd)]

    for i in range(depth):
        lam_init = 0.8 - 0.6 * math.exp(-0.3 * i)
        g1 = norm1_g[i].reshape(1, d)
        g2 = norm2_g[i].reshape(1, d)
        hy, q, k, v, y_sg = _inproj(
            xa, mod[i], g1, w_in_b, i, cos_t, sin_t,
            jnp.tile(q_norm_g[i], QK_COLS // DA_HEAD_DIM).reshape(1, QK_COLS),
            jnp.tile(k_norm_g[i], QK_COLS // DA_HEAD_DIM).reshape(1, QK_COLS),
            gmat, sg_norm_g[i].reshape(1, SG_WIDTH), sg_w[i].astype(BF16),
            jnp.repeat(sg_b[i].T, SG_WIDTH // SG_GROUPS, axis=1), n_lat_tiles, n_samples)

        fw = _filter_weights(hy_f_w1[i], hy_f_b1[i], hy_f_w2[i], hy_f_b2[i], hy_f_w3[i], hy_f_freq[i])
        hm_l = _hyfilter(fw, seq).reshape(HY_ORDER, HY_WIDTH, 2 * seq // LANES, LANES)
        hm_c = _hyfilter(fw, n_ctx).reshape(HY_ORDER, HY_WIDTH, 2 * n_ctx // LANES, LANES)
        cw = hy_conv_w[i].reshape(3, 3, HY_WIDTH)
        cbias = hy_conv_b[i].reshape(3, HY_WIDTH)
        sc_tab = jnp.concatenate([cw.reshape(9, HY_WIDTH), cbias, hy_skip[i], jnp.zeros((2, HY_WIDTH), F32)],
                                 axis=0).T.reshape(-1)
        y_hy = _hyconv(sc_tab, jnp.transpose(hy, (2, 0, 1)), hm_l, hm_c, seq, n_ctx)
        y_hy = jnp.transpose(y_hy, (1, 2, 0)).astype(BF16)

        lam_p = jnp.stack([lam_q1[i], lam_k1[i], lam_q2[i], lam_k2[i]], axis=0)
        y_da = _attention(lam_p, q, k, v, subln_g[i].reshape(1, DA_V_DIM), lam_init, seq)

        xa = _merge(xa, mod[i], g1, w_in_b, i, y_hy, y_da, y_sg, w_br_b, w_out_b, n_lat_tiles, n_samples)

        j = i // 2
        if i % 2 == 0:
            xa = _ffn_dense(xa, mod[i], g2, *ffn_b, j, n_lat_tiles, n_samples)
        else:
            xa = _moe(xa, mod[i], g2, router_w[j].T, *moe_b, j, n_lat_tiles, n_samples)
    return xa[:, :seq]
```

```python
import functools
import math

import jax
import jax.numpy as jnp
from jax import lax
from jax.experimental import pallas as pl
from jax.experimental.pallas import tpu as pltpu

F32 = jnp.float32
BF16 = jnp.bfloat16
EPS = 1e-6

GRID_W = 64
HY_WIDTH = 512
HY_ORDER = 2
HY_BANDS = 16
HY_FFN = 64
HY_MAX_DECAY = math.log(1e-2) / 0.3
HY_MIN_DECAY = math.log(1e-2) / 1.5
DA_HEADS = 4
DA_HEAD_DIM = 64
DA_V_DIM = 2 * DA_HEAD_DIM
ROPE_BASE = 10000.0
SG_WIDTH = 512
SG_GROUPS = 4
SG_CHUNK = 128
N_EXPERTS = 8
HY_COLS = 3 * HY_WIDTH
QK_COLS = DA_HEADS * 2 * DA_HEAD_DIM
DA_COLS = 2 * QK_COLS + DA_HEADS * DA_V_DIM
SG_COLS = 2 * SG_WIDTH
PRE_COLS = HY_COLS + DA_COLS + SG_COLS

LANES = 128
TOK_TILE = 256
ATTN_ROWS = 256
CONV_TILE = 256
HY_CHANNELS_PER_STEP = 8
ROW_TILE = 896
FF_TILE = 512
DMA_UNROLL = 8
MOD_ROWS = 16
VMEM_LIMIT = 56 * 1024 * 1024


def _dot(a, b):
    return jnp.dot(a, b, preferred_element_type=F32)


def _dot_nt(a, b):
    return lax.dot_general(a, b, (((1,), (1,)), ((), ())), preferred_element_type=F32)


_hdot = functools.partial(jnp.dot, precision=lax.Precision.HIGHEST, preferred_element_type=F32)


def _split_bf16(a):
    hi = a.astype(BF16)
    return hi, (a - hi.astype(F32)).astype(BF16)


def _dot3(a, b):
    a_hi, a_lo = _split_bf16(a)
    b_hi, b_lo = _split_bf16(b)
    return _dot(a_hi, b_hi) + _dot(a_lo, b_hi) + _dot(a_hi, b_lo)


def _norm_mod(x, g, shift, scale):
    ms = jnp.mean(x * x, axis=-1, keepdims=True)
    return (x * lax.rsqrt(ms + EPS) * g) * (1.0 + scale) + shift


def _params(*sem):
    return pltpu.CompilerParams(dimension_semantics=sem, vmem_limit_bytes=VMEM_LIMIT)


def _modvec_body(cond_ref, w_ref, b_ref, o_ref):
    cnd = cond_ref[...]
    s = cnd * jax.nn.sigmoid(cnd)
    s_hi, s_lo = _split_bf16(s)
    w_hi, w_lo = _split_bf16(w_ref[0])
    o_ref[0] = _dot(s_hi, w_hi) + _dot(s_lo, w_hi) + _dot(s_hi, w_lo) + b_ref[0]


def _modvec(cond, w_mod, b_mod):
    depth, d, n = w_mod.shape
    tn = n // 4
    return pl.pallas_call(
        _modvec_body,
        grid=(depth, n // tn),
        in_specs=[
            pl.BlockSpec((MOD_ROWS, d), lambda i, j: (0, 0)),
            pl.BlockSpec((1, d, tn), lambda i, j: (i, 0, j)),
            pl.BlockSpec((1, 1, tn), lambda i, j: (i, 0, j)),
        ],
        out_specs=pl.BlockSpec((1, MOD_ROWS, tn), lambda i, j: (i, 0, j)),
        out_shape=jax.ShapeDtypeStruct((depth, MOD_ROWS, n), F32),
        compiler_params=_params("parallel", "parallel"),
        name="modvec",
    )(cond, w_mod, b_mod.reshape(depth, 1, n))


def _tok_spec(width, tm):
    return pl.BlockSpec((1, tm, width), lambda b, j: (b, j, 0))


def _const_spec(shape):
    nd = len(shape)
    return pl.BlockSpec(shape, lambda b, j: (0,) * nd)


def _mod_spec(d, n_lat_tiles, n_samples):
    return pl.BlockSpec((1, 6, d), lambda b, j: (jnp.where(j < n_lat_tiles, b, n_samples), 0, 0))


def _qk_norm_rope(a, g_tile, gmat, cos, sin_signed, first_half):
    ss = _dot((a * a).astype(BF16), gmat)
    an = a * lax.rsqrt(ss * (1.0 / DA_HEAD_DIM) + EPS) * g_tile
    outs = []
    for ci in range(QK_COLS // LANES):
        ch = an[:, ci * LANES:(ci + 1) * LANES]
        partner = jnp.where(first_half, pltpu.roll(ch, LANES - 16, 1), pltpu.roll(ch, 16, 1))
        outs.append(ch * cos + partner * sin_signed)
    return jnp.concatenate(outs, axis=1)


def _inproj_body(x_ref, mod_ref, g1_ref, w_ref, cos_ref, sin_ref, qg_ref, kg_ref, gmat_ref,
                 sgg_ref, sgw_ref, sgb_ref, hy_ref, q_ref, k_ref, v_ref, sg_ref):
    tm = x_ref.shape[1]
    h = _norm_mod(x_ref[0], g1_ref[...], mod_ref[0, 0:1, :], mod_ref[0, 1:2, :]).astype(BF16)
    hy_ref[0] = _dot(h, w_ref[0, :, 0:HY_COLS])

    lane = lax.broadcasted_iota(jnp.int32, (1, LANES), 1)
    first_half = (lane % 32) < 16
    cos = cos_ref[...]
    sin_signed = sin_ref[...]
    gmat = gmat_ref[...]
    o = HY_COLS
    q = _qk_norm_rope(_dot(h, w_ref[0, :, o:o + QK_COLS]), qg_ref[...], gmat, cos, sin_signed, first_half)
    q_ref[0] = (q * (DA_HEAD_DIM ** -0.5 * math.log2(math.e))).astype(BF16)
    o += QK_COLS
    k = _qk_norm_rope(_dot(h, w_ref[0, :, o:o + QK_COLS]), kg_ref[...], gmat, cos, sin_signed, first_half)
    k_ref[0] = k.astype(BF16)
    o += QK_COLS
    v_ref[0] = _dot(h, w_ref[0, :, o:o + DA_HEADS * DA_V_DIM]).astype(BF16)
    o += DA_HEADS * DA_V_DIM

    z = jax.nn.gelu(_dot(h, w_ref[0, :, o:o + SG_COLS]))
    u = z[:, :SG_WIDTH]
    vv = z[:, SG_WIDTH:]
    vn = (vv * lax.rsqrt(jnp.mean(vv * vv, axis=-1, keepdims=True) + EPS) * sgg_ref[...]).astype(BF16)
    gw = SG_WIDTH // SG_GROUPS
    for ch in range(tm // SG_CHUNK):
        r0 = ch * SG_CHUNK
        cols = []
        for g in range(SG_GROUPS):
            s = _dot(sgw_ref[g], vn[r0:r0 + SG_CHUNK, g * gw:(g + 1) * gw]) + sgb_ref[:, g * gw:(g + 1) * gw]
            cols.append(u[r0:r0 + SG_CHUNK, g * gw:(g + 1) * gw] * s)
        sg_ref[0, r0:r0 + SG_CHUNK, :] = jnp.concatenate(cols, axis=1).astype(BF16)


def _inproj(xa, mod_i, g1, w_in, layer, cos_t, sin_t, qg, kg, gmat, sgg, sgw, sgb, n_lat_tiles, n_samples):
    b, t, d = xa.shape
    tm = TOK_TILE
    outs = [jax.ShapeDtypeStruct((b, t, HY_COLS), F32)] + [jax.ShapeDtypeStruct((b, t, QK_COLS), BF16)] * 4
    return pl.pallas_call(
        _inproj_body,
        grid=(b, t // tm),
        in_specs=[
            _tok_spec(d, tm),
            _mod_spec(d, n_lat_tiles, n_samples),
            _const_spec((1, d)),
            pl.BlockSpec((1, d, PRE_COLS), lambda bi, j: (layer, 0, 0)),
            pl.BlockSpec((tm, LANES), lambda bi, j: (j, 0)),
            pl.BlockSpec((tm, LANES), lambda bi, j: (j, 0)),
            _const_spec((1, QK_COLS)),
            _const_spec((1, QK_COLS)),
            _const_spec((QK_COLS, QK_COLS)),
            _const_spec((1, SG_WIDTH)),
            _const_spec((SG_GROUPS, SG_CHUNK, SG_CHUNK)),
            _const_spec((SG_CHUNK, SG_WIDTH)),
        ],
        out_specs=[_tok_spec(HY_COLS, tm)] + [_tok_spec(QK_COLS, tm)] * 4,
        out_shape=outs,
        compiler_params=_params("parallel", "parallel"),
        name="inproj",
    )(xa, mod_i, g1, w_in, cos_t, sin_t, qg, kg, gmat, sgg, sgw, sgb)


def _hyfilter_body(w1t_ref, w1c_ref, w1s_ref, b1_ref, w2_ref, b2_ref, fr_ref, w3f_ref, w3b_ref,
                   bands_ref, dl_ref, o_ref, *, seq):
    n = 2 * seq
    xi = lax.broadcasted_iota(jnp.int32, (1, n), 1)
    lag = xi - (seq - 1)
    pos = jnp.abs(lag).astype(F32)
    t = pos / (seq - 1)
    ang = 2.0 * math.pi * pos * bands_ref[...] / seq
    fr = fr_ref[...]
    z1 = w1t_ref[...] * t + _hdot(w1c_ref[...], jnp.cos(ang)) + _hdot(w1s_ref[...], jnp.sin(ang)) + b1_ref[...]
    h1 = jnp.sin(fr * z1)
    h2 = jnp.sin(fr * (_hdot(w2_ref[...], h1) + b2_ref[...]))
    bwd = _dot3(w3b_ref[...], h2[:, :seq])
    fwd = _dot3(w3f_ref[...], h2[:, seq:])
    fwd0 = _dot3(w3f_ref[...], h2[:, seq - LANES:seq])
    at_zero = jnp.where(lag[:, seq - LANES:seq] == 0, fwd0, 0.0)
    k = jnp.concatenate([bwd[:, :seq - LANES], bwd[:, seq - LANES:] + at_zero, fwd], axis=1)
    k = jnp.where(xi < n - 1, k * jnp.exp(-t * dl_ref[...]), 0.0)
    o_ref[...] = k / jnp.sum(jnp.abs(k), axis=-1, keepdims=True)


def _hyfilter(fw, seq):
    rows = HY_ORDER * HY_WIDTH
    rb = 256
    n = 2 * seq
    small = lambda shape: pl.BlockSpec(shape, lambda i: (0, 0))
    return pl.pallas_call(
        functools.partial(_hyfilter_body, seq=seq),
        grid=(rows // rb,),
        in_specs=[
            small((HY_FFN, 1)), small((HY_FFN, HY_BANDS)), small((HY_FFN, HY_BANDS)), small((HY_FFN, 1)),
            small((HY_FFN, HY_FFN)), small((HY_FFN, 1)), small((HY_FFN, 1)),
            pl.BlockSpec((rb, HY_FFN), lambda i: (i, 0)),
            pl.BlockSpec((rb, HY_FFN), lambda i: (i, 0)),
            small((HY_BANDS, 1)),
            pl.BlockSpec((rb, 1), lambda i: (i, 0)),
        ],
        out_specs=pl.BlockSpec((rb, n), lambda i: (i, 0)),
        out_shape=jax.ShapeDtypeStruct((rows, n), F32),
        compiler_params=_params("parallel"),
        name="hyfilter",
    )(*fw)


def _hyconv_body(sc_ref, z_ref, x1_ref, x2_ref, hml_ref, hmc_ref, o_ref, big_l, big_c, *, seq, ctx, cb):
    tot = seq + ctx
    lane = lax.broadcasted_iota(jnp.int32, (1, tot), 1)
    has_prev = jnp.logical_and(lane != 0, lane != seq)
    has_next = jnp.logical_and(lane != seq - 1, lane != tot - 1)
    row = lax.broadcasted_iota(jnp.int32, (LANES, LANES), 0)
    col = lax.broadcasted_iota(jnp.int32, (LANES, LANES), 1)
    lower = col <= row
    c0 = pl.program_id(0) * cb

    def short_conv(p, base, part):
        prev = jnp.where(has_prev, pltpu.roll(p, 1, 1), 0.0)
        nxt = jnp.where(has_next, pltpu.roll(p, tot - 1, 1), 0.0)
        return (sc_ref[base + 9 + part] + sc_ref[base + part] * prev
                + sc_ref[base + 3 + part] * p + sc_ref[base + 6 + part] * nxt)

    def build(hm_ref, o, ci, big_ref):
        n_rows = hm_ref.shape[2]
        prev = None
        for rp in range(n_rows - 1, -1, -1):
            r = hm_ref[o, ci, rp:rp + 1, :]
            cur = pltpu.roll(jnp.broadcast_to(r, (LANES, LANES)), 1, 1, stride=1, stride_axis=0)
            if prev is not None:
                rho = n_rows - 2 - rp
                big_ref[o, rho * LANES:(rho + 1) * LANES, :] = jnp.where(lower, cur, prev).astype(BF16)
            prev = cur

    def long_conv(zz, big_ref, o, length, base):
        nblk = length // CONV_TILE
        mid = (2 * nblk - 1) * LANES
        ys = [None] * nblk
        for dd in range(-(nblk - 1), nblk):
            r0 = mid - CONV_TILE * dd
            w = jnp.concatenate([big_ref[o, r0:r0 + CONV_TILE, :],
                                 big_ref[o, r0 - LANES:r0 - LANES + CONV_TILE, :]], axis=1)
            js = list(range(max(0, -dd), min(nblk, nblk - dd)))
            lhs = jnp.concatenate([zz[:, base + j * CONV_TILE:base + (j + 1) * CONV_TILE] for j in js], axis=0)
            out = _dot(lhs.astype(BF16), w)
            nb = zz.shape[0]
            for kk, j in enumerate(js):
                piece = out[kk * nb:(kk + 1) * nb]
                ys[j + dd] = piece if ys[j + dd] is None else ys[j + dd] + piece
        return ys

    def chan(ci, carry):
        base = (c0 + ci) * 16
        z = short_conv(z_ref[ci], base, 0)
        gates = (short_conv(x1_ref[ci], base, 1), short_conv(x2_ref[ci], base, 2))
        for o in range(HY_ORDER):
            build(hml_ref, o, ci, big_l)
            build(hmc_ref, o, ci, big_c)
        for o in range(HY_ORDER):
            y = jnp.concatenate(long_conv(z, big_l, o, seq, 0) + long_conv(z, big_c, o, ctx, seq), axis=1)
            z = gates[o] * (y + sc_ref[base + 12 + o] * z)
        o_ref[ci] = z
        return carry

    lax.fori_loop(0, cb, chan, 0, unroll=4)


def _hyconv(sc_tab, u3, hm_l, hm_c, seq, ctx):
    _, b, tot = u3.shape
    cb = HY_CHANNELS_PER_STEP
    nblk = HY_WIDTH // cb
    slab = lambda part: pl.BlockSpec((cb, b, tot), lambda i: (part * nblk + i, 0, 0))
    return pl.pallas_call(
        functools.partial(_hyconv_body, seq=seq, ctx=ctx, cb=cb),
        grid=(nblk,),
        in_specs=[
            pl.BlockSpec(memory_space=pltpu.SMEM),
            slab(0), slab(1), slab(2),
            pl.BlockSpec((HY_ORDER, cb, hm_l.shape[2], LANES), lambda i: (0, i, 0, 0)),
            pl.BlockSpec((HY_ORDER, cb, hm_c.shape[2], LANES), lambda i: (0, i, 0, 0)),
        ],
        out_specs=pl.BlockSpec((cb, b, tot), lambda i: (i, 0, 0)),
        out_shape=jax.ShapeDtypeStruct((HY_WIDTH, b, tot), F32),
        scratch_shapes=[
            pltpu.VMEM((HY_ORDER, (hm_l.shape[2] - 1) * LANES, LANES), BF16),
            pltpu.VMEM((HY_ORDER, (hm_c.shape[2] - 1) * LANES, LANES), BF16),
        ],
        compiler_params=_params("parallel"),
        name="hyconv",
    )(sc_tab, u3, u3, u3, hm_l, hm_c)


def _attn_body(lam_ref, q_ref, k_ref, v_ref, sub_ref, o_ref, *, lam_init):
    tq = q_ref.shape[1]
    k = k_ref[0]
    v = v_ref[0]
    lane = lax.broadcasted_iota(jnp.int32, (1, 2 * DA_HEAD_DIM), 1)
    first = lane < DA_HEAD_DIM
    lp = lam_ref[...]
    lam = (jnp.exp(jnp.sum(lp[0:1] * lp[1:2], keepdims=True))
           - jnp.exp(jnp.sum(lp[2:3] * lp[3:4], keepdims=True)) + lam_init)
    sub = sub_ref[...] * (1.0 - lam_init)
    for r0 in range(0, tq, ATTN_ROWS):
        q = q_ref[0, r0:r0 + ATTN_ROWS, :]
        zero = jnp.zeros_like(q)
        s1 = _dot_nt(jnp.where(first, q, zero), k)
        s2 = _dot_nt(jnp.where(first, zero, q), k)
        p1 = jnp.exp2(s1 - jnp.max(s1, axis=-1, keepdims=True))
        p2 = jnp.exp2(s2 - jnp.max(s2, axis=-1, keepdims=True))
        w = p1 * (1.0 / jnp.sum(p1, axis=-1, keepdims=True)) - p2 * (lam / jnp.sum(p2, axis=-1, keepdims=True))
        o = _dot(w.astype(BF16), v)
        on = o * lax.rsqrt(jnp.mean(o * o, axis=-1, keepdims=True) + EPS) * sub
        o_ref[0, r0:r0 + ATTN_ROWS, :] = on.astype(BF16)


def _attention(lam_p, q, k, v, sub, lam_init, seq):
    b, tot, _ = q.shape
    ctx = tot - seq
    tq = TOK_TILE
    hw = 2 * DA_HEAD_DIM
    body = functools.partial(_attn_body, lam_init=lam_init)
    small = [pl.BlockSpec((4, DA_HEAD_DIM), lambda bi, h, j: (0, 0)), pl.BlockSpec((1, hw), lambda bi, h, j: (0, 0))]
    lat = pl.pallas_call(
        body,
        grid=(b, DA_HEADS, seq // tq),
        in_specs=[
            small[0],
            pl.BlockSpec((1, tq, hw), lambda bi, h, j: (bi, j, h)),
            pl.BlockSpec((1, tot, hw), lambda bi, h, j: (bi, 0, h)),
            pl.BlockSpec((1, tot, hw), lambda bi, h, j: (bi, 0, h)),
            small[1],
        ],
        out_specs=pl.BlockSpec((1, tq, hw), lambda bi, h, j: (bi, j, h)),
        out_shape=jax.ShapeDtypeStruct((b, seq, DA_HEADS * DA_V_DIM), BF16),
        compiler_params=_params("parallel", "parallel", "parallel"),
        name="diffattn",
    )(lam_p, q, k, v, sub)
    q0 = seq // tq
    k0 = seq // ctx
    ctx_out = pl.pallas_call(
        body,
        grid=(b, DA_HEADS, ctx // tq),
        in_specs=[
            small[0],
            pl.BlockSpec((1, tq, hw), lambda bi, h, j: (bi, q0 + j, h)),
            pl.BlockSpec((1, ctx, hw), lambda bi, h, j: (bi, k0, h)),
            pl.BlockSpec((1, ctx, hw), lambda bi, h, j: (bi, k0, h)),
            small[1],
        ],
        out_specs=pl.BlockSpec((1, tq, hw), lambda bi, h, j: (bi, j, h)),
        out_shape=jax.ShapeDtypeStruct((b, ctx, DA_HEADS * DA_V_DIM), BF16),
        compiler_params=_params("parallel", "parallel", "parallel"),
        name="diffattn_ctx",
    )(lam_p, q, k, v, sub)
    return jnp.concatenate([lat, ctx_out], axis=1)


def _merge_body(x_ref, mod_ref, g1_ref, wg0_ref, wg1_ref, wg2_ref, yh_ref, yd_ref, ys_ref, wb_ref, wo_ref, o_ref):
    x = x_ref[0]
    h = _norm_mod(x, g1_ref[...], mod_ref[0, 0:1, :], mod_ref[0, 1:2, :]).astype(BF16)
    acc = None
    for n, (wg_ref, y_ref) in enumerate(((wg0_ref, yh_ref), (wg1_ref, yd_ref), (wg2_ref, ys_ref))):
        gate = jax.nn.sigmoid(_dot(h, wg_ref[0]))
        term = gate * _dot(y_ref[0], wb_ref[0, n])
        acc = term if acc is None else acc + term
    o_ref[0] = x + mod_ref[0, 2:3, :] * _dot(acc.astype(BF16), wo_ref[0])


def _merge(xa, mod_i, g1, w_in, layer, y_hy, y_da, y_sg, w_br, w_out, rows, n_lat_tiles, n_samples):
    b, _, d = xa.shape
    t = rows
    tm = TOK_TILE
    assert PRE_COLS % d == 0
    gate_spec = lambda n: pl.BlockSpec((1, d, d), lambda bi, j: (layer, 0, PRE_COLS // d + n))
    return pl.pallas_call(
        _merge_body,
        grid=(b, t // tm),
        in_specs=[
            _tok_spec(d, tm), _mod_spec(d, n_lat_tiles, n_samples), _const_spec((1, d)),
            gate_spec(0), gate_spec(1), gate_spec(2),
            _tok_spec(HY_WIDTH, tm), _tok_spec(HY_WIDTH, tm), _tok_spec(HY_WIDTH, tm),
            pl.BlockSpec((1, 3, HY_WIDTH, d), lambda bi, j: (layer, 0, 0, 0)),
            pl.BlockSpec((1, d, d), lambda bi, j: (layer, 0, 0)),
        ],
        out_specs=_tok_spec(d, tm),
        out_shape=jax.ShapeDtypeStruct((b, t, d), F32),
        compiler_params=_params("parallel", "parallel"),
        name="merge",
    )(xa, mod_i, g1, w_in, w_in, w_in, y_hy, y_da, y_sg, w_br, w_out)


def _ffn_body(x_ref, mod_ref, g2_ref, wg_ref, wu_ref, wd_ref, o_ref, *, chunk):
    x = x_ref[0]
    h = _norm_mod(x, g2_ref[...], mod_ref[0, 3:4, :], mod_ref[0, 4:5, :]).astype(BF16)
    ff = wg_ref.shape[2]
    acc = None
    for f0 in range(0, ff, chunk):
        f1 = min(ff, f0 + chunk)
        a = _dot(h, wg_ref[0, :, f0:f1])
        mid = (a * jax.nn.sigmoid(a) * _dot(h, wu_ref[0, :, f0:f1])).astype(BF16)
        term = _dot(mid, wd_ref[0, f0:f1, :])
        acc = term if acc is None else acc + term
    o_ref[0] = x + mod_ref[0, 5:6, :] * acc


def _ffn_dense(xa, mod_i, g2, wg, wu, wd, layer, n_lat_tiles, n_samples):
    b, t, d = xa.shape
    ff = wg.shape[2]
    tm = TOK_TILE
    return pl.pallas_call(
        functools.partial(_ffn_body, chunk=1024),
        grid=(b, t // tm),
        in_specs=[
            _tok_spec(d, tm), _mod_spec(d, n_lat_tiles, n_samples), _const_spec((1, d)),
            pl.BlockSpec((1, d, ff), lambda bi, j: (layer, 0, 0)),
            pl.BlockSpec((1, d, ff), lambda bi, j: (layer, 0, 0)),
            pl.BlockSpec((1, ff, d), lambda bi, j: (layer, 0, 0)),
        ],
        out_specs=_tok_spec(d, tm),
        out_shape=jax.ShapeDtypeStruct((b, t, d), F32),
        compiler_params=_params("parallel", "parallel"),
        name="ffn_dense",
    )(xa, mod_i, g2, wg, wu, wd)


def _router_body(x_ref, mod_ref, g2_ref, wr_ref, h_ref, r_ref):
    h = _norm_mod(x_ref[0], g2_ref[...], mod_ref[0, 3:4, :], mod_ref[0, 4:5, :])
    h_ref[0] = h
    h_hi, h_lo = _split_bf16(h)
    w_hi, w_lo = _split_bf16(wr_ref[...])
    logits = _dot_nt(w_hi, h_hi) + _dot_nt(w_hi, h_lo) + _dot_nt(w_lo, h_hi)
    eid = lax.broadcasted_iota(jnp.int32, logits.shape, 0)
    m1 = jnp.max(logits, axis=0, keepdims=True)
    i1 = jnp.min(jnp.where(logits == m1, eid, N_EXPERTS), axis=0, keepdims=True)
    rest = jnp.where(eid == i1, -jnp.inf, logits)
    m2 = jnp.max(rest, axis=0, keepdims=True)
    i2 = jnp.min(jnp.where(rest == m2, eid, N_EXPERTS), axis=0, keepdims=True)
    w1 = 1.0 / (1.0 + jnp.exp(m2 - m1))
    rows = lax.broadcasted_iota(jnp.int32, logits.shape, 0)
    out = jnp.where(rows == 0, i1.astype(F32), jnp.where(rows == 1, i2.astype(F32),
                    jnp.where(rows == 2, w1, jnp.where(rows == 3, 1.0 - w1, 0.0))))
    r_ref[0] = out


def _router(xa, mod_i, g2, wr_t, n_lat_tiles, n_samples):
    b, t, d = xa.shape
    tm = TOK_TILE
    return pl.pallas_call(
        _router_body,
        grid=(b, t // tm),
        in_specs=[
            _tok_spec(d, tm), _mod_spec(d, n_lat_tiles, n_samples), _const_spec((1, d)),
            _const_spec((N_EXPERTS, d)),
        ],
        out_specs=[_tok_spec(d, tm), pl.BlockSpec((1, N_EXPERTS, tm), lambda bi, j: (bi, 0, j))],
        out_shape=[jax.ShapeDtypeStruct((b, t, d), F32), jax.ShapeDtypeStruct((b, N_EXPERTS, t), F32)],
        compiler_params=_params("parallel", "parallel"),
        name="router",
    )(xa, mod_i, g2, wr_t)


def _row_copy(src_hbm, dst, sem, src_row, dst_row):
    return pltpu.make_async_copy(src_hbm.at[pl.ds(src_row, 1), :], dst.at[pl.ds(dst_row, 1), :], sem)


def _gather_rows(src_hbm, dst, sem, idx):
    rows = dst.shape[0]

    def start(g, c):
        for u in range(DMA_UNROLL):
            r = g * DMA_UNROLL + u
            _row_copy(src_hbm, dst, sem, idx(r), r).start(priority=u % 2)
        return c
    lax.fori_loop(0, rows // DMA_UNROLL, start, 0)


def _wait_rows(src_hbm, dst, sem):
    pltpu.make_async_copy(src_hbm.at[pl.ds(0, dst.shape[0]), :], dst, sem).wait()


def _expert_body(te_ref, nu_ref, rt_ref, rtn_ref, h_hbm, wg_ref, wu_ref, wd_ref, o_ref, xbuf, xbf, acc, sem, *, chunk):
    t = pl.program_id(0)
    f = pl.program_id(1)
    last_f = pl.num_programs(1) - 1
    n_used = nu_ref[0]
    used = t < n_used
    slot = t % 2

    @pl.when(jnp.logical_and(f == 0, jnp.logical_and(used, t == 0)))
    def _first_gather():
        _gather_rows(h_hbm, xbuf.at[0], sem.at[0], lambda r: rt_ref[0, 0, r])

    @pl.when(jnp.logical_and(used, f == 0))
    def _stage():
        _wait_rows(h_hbm, xbuf.at[slot], sem.at[slot])
        xbf[...] = xbuf[slot].astype(BF16)
        acc[...] = jnp.zeros_like(acc)

    @pl.when(used)
    def _compute():
        xb = xbf[...]
        a = _dot(xb, wg_ref[0, 0].astype(BF16))
        b = _dot(xb, wu_ref[0, 0].astype(BF16))
        for u in range(chunk):
            r = f * chunk + u
            _row_copy(h_hbm, xbuf.at[1 - slot], sem.at[1 - slot], rtn_ref[0, 0, r], r).start(priority=u % 2)
        mid = (a * jax.nn.sigmoid(a) * b).astype(BF16)
        acc[...] += _dot(mid, wd_ref[0, 0].astype(BF16))

    @pl.when(f == last_f)
    def _store():
        o_ref[...] = jnp.where(used, acc[...], 0.0)

    @pl.when(jnp.logical_and(f == last_f, t == n_used - 1))
    def _drain():
        _wait_rows(h_hbm, xbuf.at[1 - slot], sem.at[1 - slot])


def _experts(tile_expert, n_used, row_token, h_flat, wg, wu, wd, layer):
    n_rows = row_token.shape[0]
    d = h_flat.shape[1]
    ff = wg.shape[3]
    nt = n_rows // ROW_TILE
    rt3 = row_token.reshape(nt, 1, ROW_TILE)
    grid_spec = pltpu.PrefetchScalarGridSpec(
        num_scalar_prefetch=2,
        grid=(nt, ff // FF_TILE),
        in_specs=[
            pl.BlockSpec((1, 1, ROW_TILE), lambda t, f, te, nu: (t, 0, 0), memory_space=pltpu.SMEM),
            pl.BlockSpec((1, 1, ROW_TILE), lambda t, f, te, nu: (jnp.minimum(t + 1, nt - 1), 0, 0),
                         memory_space=pltpu.SMEM),
            pl.BlockSpec(memory_space=pl.ANY),
            pl.BlockSpec((1, 1, d, FF_TILE), lambda t, f, te, nu: (layer, te[t], 0, f)),
            pl.BlockSpec((1, 1, d, FF_TILE), lambda t, f, te, nu: (layer, te[t], 0, f)),
            pl.BlockSpec((1, 1, FF_TILE, d), lambda t, f, te, nu: (layer, te[t], f, 0)),
        ],
        out_specs=pl.BlockSpec((ROW_TILE, d), lambda t, f, te, nu: (t, 0)),
        scratch_shapes=[
            pltpu.VMEM((2, ROW_TILE, d), F32),
            pltpu.VMEM((ROW_TILE, d), BF16),
            pltpu.VMEM((ROW_TILE, d), F32),
            pltpu.SemaphoreType.DMA((2,)),
        ],
    )
    assert ROW_TILE % (ff // FF_TILE) == 0
    return pl.pallas_call(
        functools.partial(_expert_body, chunk=ROW_TILE // (ff // FF_TILE)),
        grid_spec=grid_spec,
        out_shape=jax.ShapeDtypeStruct((n_rows, d), F32),
        compiler_params=_params("arbitrary", "arbitrary"),
        name="experts",
    )(tile_expert, n_used, rt3, rt3, h_flat, wg, wu, wd)


def _combine_body(pos_ref, posn_ref, r_ref, x_ref, mod_ref, ye_hbm, o_ref, buf, sem):
    tm = x_ref.shape[1]
    step = pl.program_id(0) * pl.num_programs(1) + pl.program_id(1)
    n_steps = pl.num_programs(0) * pl.num_programs(1)
    slot = step % 2

    def gather(p_ref, s):
        for kk in range(2):
            _gather_rows(ye_hbm, buf.at[s, kk], sem.at[s, kk], lambda r, kk=kk: p_ref[0, kk, r])

    @pl.when(step == 0)
    def _first():
        gather(pos_ref, 0)

    @pl.when(step + 1 < n_steps)
    def _next():
        gather(posn_ref, 1 - slot)

    eye = (lax.broadcasted_iota(jnp.int32, (tm, tm), 0) == lax.broadcasted_iota(jnp.int32, (tm, tm), 1)).astype(BF16)
    r_hi, r_lo = _split_bf16(r_ref[0])
    rcol = _dot_nt(eye, r_hi) + _dot_nt(eye, r_lo)
    for kk in range(2):
        _wait_rows(ye_hbm, buf.at[slot, kk], sem.at[slot, kk])
    mix = buf[slot, 0] * rcol[:, 2:3] + buf[slot, 1] * rcol[:, 3:4]
    o_ref[0] = x_ref[0] + mod_ref[0, 5:6, :] * mix


def _combine(pos, route, xa, mod_i, ye, n_lat_tiles, n_samples):
    b, t, d = xa.shape
    tm = TOK_TILE
    nj = t // tm
    return pl.pallas_call(
        _combine_body,
        grid=(b, nj),
        in_specs=[
            pl.BlockSpec((1, 2, tm), lambda bi, j: (bi * nj + j, 0, 0), memory_space=pltpu.SMEM),
            pl.BlockSpec((1, 2, tm), lambda bi, j: (jnp.minimum(bi * nj + j + 1, b * nj - 1), 0, 0),
                         memory_space=pltpu.SMEM),
            pl.BlockSpec((1, N_EXPERTS, tm), lambda bi, j: (bi, 0, j)),
            _tok_spec(d, tm), _mod_spec(d, n_lat_tiles, n_samples),
            pl.BlockSpec(memory_space=pl.ANY),
        ],
        out_specs=_tok_spec(d, tm),
        out_shape=jax.ShapeDtypeStruct((b, t, d), F32),
        scratch_shapes=[pltpu.VMEM((2, 2, tm, d), F32), pltpu.SemaphoreType.DMA((2, 2))],
        compiler_params=_params("arbitrary", "arbitrary"),
        name="moe_combine",
    )(pos, pos, route, xa, mod_i, ye)


def _moe(xa, mod_i, g2, wr_t, wg, wu, wd, layer, n_lat_tiles, n_samples):
    b, t, d = xa.shape
    h, route = _router(xa, mod_i, g2, wr_t, n_lat_tiles, n_samples)
    n_tok = b * t
    expert = route[:, 0:2, :].astype(jnp.int32)
    onehot = (expert.reshape(-1)[:, None] == jnp.arange(N_EXPERTS)[None, :]).astype(jnp.int32)
    counts = jnp.sum(onehot, axis=0)
    rank = jnp.sum((jnp.cumsum(onehot, axis=0) - onehot) * onehot, axis=1)
    padded = ((counts + ROW_TILE - 1) // ROW_TILE) * ROW_TILE
    ends = jnp.cumsum(padded)
    pos = ((ends - padded)[expert.reshape(-1)] + rank).reshape(b, 2, t)
    n_rows = ((2 * n_tok + N_EXPERTS * (ROW_TILE - 1)) // ROW_TILE + 1) * ROW_TILE
    token = jnp.broadcast_to(jnp.arange(b)[:, None, None] * t + jnp.arange(t)[None, None, :], (b, 2, t))
    row_token = jnp.zeros((n_rows,), jnp.int32).at[pos.reshape(-1)].set(token.reshape(-1))
    tile_start = jnp.arange(n_rows // ROW_TILE) * ROW_TILE
    tile_expert = jnp.minimum(jnp.sum(tile_start[:, None] >= ends[None, :], axis=1), N_EXPERTS - 1).astype(jnp.int32)
    n_used = (ends[-1:] // ROW_TILE).astype(jnp.int32)
    ye = _experts(tile_expert, n_used, row_token, h.reshape(n_tok, d), wg, wu, wd, layer)
    nj = t // TOK_TILE
    pos_tiles = pos.reshape(b, 2, nj, TOK_TILE).transpose(0, 2, 1, 3).reshape(b * nj, 2, TOK_TILE)
    return _combine(pos_tiles, route, xa, mod_i, ye, n_lat_tiles, n_samples)


def _rope_tables(seq, ctx):
    rows = seq // GRID_W
    row = jnp.repeat(jnp.arange(rows), GRID_W).astype(F32)
    col = jnp.tile(jnp.arange(GRID_W), rows).astype(F32)
    half = DA_HEAD_DIM // 2
    inv = ROPE_BASE ** (-jnp.arange(0, half, 2, dtype=F32) / half)
    ang = jnp.concatenate([row[:, None] * inv, row[:, None] * inv, col[:, None] * inv, col[:, None] * inv], axis=1)
    sign = jnp.tile(jnp.concatenate([-jnp.ones((half // 2,), F32), jnp.ones((half // 2,), F32)]), 2)
    cos = jnp.concatenate([jnp.cos(ang), jnp.ones((ctx, DA_HEAD_DIM), F32)], axis=0)
    sin = jnp.concatenate([jnp.sin(ang) * sign, jnp.zeros((ctx, DA_HEAD_DIM), F32)], axis=0)
    return jnp.tile(cos, (1, LANES // DA_HEAD_DIM)), jnp.tile(sin, (1, LANES // DA_HEAD_DIM))


def _filter_weights(w1, b1, w2, b2, w3, freq):
    col = lambda v: v.reshape(-1, 1)
    bands = jnp.linspace(1e-4, HY_BANDS - 1, HY_BANDS, dtype=F32)
    deltas = jnp.abs(jnp.linspace(HY_MIN_DECAY, HY_MAX_DECAY, HY_WIDTH, dtype=F32))
    half = HY_ORDER * HY_WIDTH
    return (w1[0:1].T, w1[1:1 + HY_BANDS].T, w1[1 + HY_BANDS:].T, col(b1), w2.T, col(b2), col(freq),
            w3[:, :half].T, w3[:, half:].T, col(bands), col(jnp.tile(deltas, HY_ORDER)))


def kernel(x, c, ctx, c_ctx, w_mod, b_mod, norm1_g, norm2_g, w_in, hy_conv_w, hy_conv_b, hy_f_w1, hy_f_b1, hy_f_w2, hy_f_b2, hy_f_w3, hy_f_freq, hy_skip, q_norm_g, k_norm_g, lam_q1, lam_k1, lam_q2, lam_k2, subln_g, sg_norm_g, sg_w, sg_b, w_branch, w_out, ffn_wg, ffn_wu, ffn_wd, router_w, moe_wg, moe_wu, moe_wd):
    n_samples, seq, d = x.shape
    n_ctx = ctx.shape[1]
    depth = w_mod.shape[0]
    assert seq % CONV_TILE == 0 and n_ctx % CONV_TILE == 0 and seq % GRID_W == 0 and seq % n_ctx == 0
    assert n_samples + 1 <= MOD_ROWS
    n_lat_tiles = seq // TOK_TILE
    tot = seq + n_ctx

    xa = jnp.concatenate([x, ctx], axis=1)
    cond = jnp.zeros((MOD_ROWS, d), F32).at[:n_samples].set(c).at[n_samples].set(c_ctx)
    mod = _modvec(cond, w_mod, b_mod).reshape(depth, MOD_ROWS, 6, d)

    cos_t, sin_t = _rope_tables(seq, n_ctx)
    group = jnp.arange(QK_COLS) // DA_HEAD_DIM
    gmat = (group[:, None] == group[None, :]).astype(BF16)
    w_in_b = w_in.astype(BF16)
    w_br_b = w_branch.astype(BF16)
    w_out_b = w_out.astype(BF16)
    ffn_b = [w.astype(BF16) for w in (ffn_wg, ffn_wu, ffn_wd)]

    for i in range(depth):
        lam_init = 0.8 - 0.6 * math.exp(-0.3 * i)
        g1 = norm1_g[i].reshape(1, d)
        g2 = norm2_g[i].reshape(1, d)
        hy, q, k, v, y_sg = _inproj(
            xa, mod[i], g1, w_in_b, i, cos_t, sin_t,
            jnp.tile(q_norm_g[i], QK_COLS // DA_HEAD_DIM).reshape(1, QK_COLS),
            jnp.tile(k_norm_g[i], QK_COLS // DA_HEAD_DIM).reshape(1, QK_COLS),
            gmat, sg_norm_g[i].reshape(1, SG_WIDTH), sg_w[i].astype(BF16),
            jnp.repeat(sg_b[i].T, SG_WIDTH // SG_GROUPS, axis=1), n_lat_tiles, n_samples)

        fw = _filter_weights(hy_f_w1[i], hy_f_b1[i], hy_f_w2[i], hy_f_b2[i], hy_f_w3[i], hy_f_freq[i])
        hm_l = _hyfilter(fw, seq).reshape(HY_ORDER, HY_WIDTH, 2 * seq // LANES, LANES)
        hm_c = _hyfilter(fw, n_ctx).reshape(HY_ORDER, HY_WIDTH, 2 * n_ctx // LANES, LANES)
        cw = hy_conv_w[i].reshape(3, 3, HY_WIDTH)
        cbias = hy_conv_b[i].reshape(3, HY_WIDTH)
        sc_tab = jnp.concatenate([cw.reshape(9, HY_WIDTH), cbias, hy_skip[i], jnp.zeros((2, HY_WIDTH), F32)],
                                 axis=0).T.reshape(-1)
        y_hy = _hyconv(sc_tab, jnp.transpose(hy, (2, 0, 1)), hm_l, hm_c, seq, n_ctx)
        y_hy = jnp.transpose(y_hy, (1, 2, 0)).astype(BF16)

        lam_p = jnp.stack([lam_q1[i], lam_k1[i], lam_q2[i], lam_k2[i]], axis=0)
        y_da = _attention(lam_p, q, k, v, subln_g[i].reshape(1, DA_V_DIM), lam_init, seq)

        xa = _merge(xa, mod[i], g1, w_in_b, i, y_hy, y_da, y_sg, w_br_b, w_out_b,
                    seq if i == depth - 1 else tot, n_lat_tiles, n_samples)

        j = i // 2
        if i % 2 == 0:
            xa = _ffn_dense(xa, mod[i], g2, *ffn_b, j, n_lat_tiles, n_samples)
        else:
            xa = _moe(xa, mod[i], g2, router_w[j].T, moe_wg, moe_wu, moe_wd, j, n_lat_tiles, n_samples)
    return xa[:, :seq]
```

```python
import functools
import math

import jax
import jax.numpy as jnp
from jax import lax
from jax.experimental import pallas as pl
from jax.experimental.pallas import tpu as pltpu

F32 = jnp.float32
BF16 = jnp.bfloat16
EPS = 1e-6

GRID_W = 64
HY_WIDTH = 512
HY_ORDER = 2
HY_BANDS = 16
HY_FFN = 64
HY_MAX_DECAY = math.log(1e-2) / 0.3
HY_MIN_DECAY = math.log(1e-2) / 1.5
DA_HEADS = 4
DA_HEAD_DIM = 64
DA_V_DIM = 2 * DA_HEAD_DIM
ROPE_BASE = 10000.0
SG_WIDTH = 512
SG_GROUPS = 4
SG_CHUNK = 128
N_EXPERTS = 8
HY_COLS = 3 * HY_WIDTH
QK_COLS = DA_HEADS * 2 * DA_HEAD_DIM
DA_COLS = 2 * QK_COLS + DA_HEADS * DA_V_DIM
SG_COLS = 2 * SG_WIDTH
PRE_COLS = HY_COLS + DA_COLS + SG_COLS

LANES = 128
TOK_TILE = 256
CONV_TILE = 256
HY_CHANNELS_PER_STEP = 8
ROW_TILE = 896
FF_TILE = 512
DMA_UNROLL = 8
MOD_ROWS = 16
VMEM_LIMIT = 56 * 1024 * 1024


def _dot(a, b):
    return jnp.dot(a, b, preferred_element_type=F32)


def _dot_nt(a, b):
    return lax.dot_general(a, b, (((1,), (1,)), ((), ())), preferred_element_type=F32)


_hdot = functools.partial(jnp.dot, precision=lax.Precision.HIGHEST, preferred_element_type=F32)


def _split_bf16(a):
    hi = a.astype(BF16)
    return hi, (a - hi.astype(F32)).astype(BF16)


def _dot3(a, b):
    a_hi, a_lo = _split_bf16(a)
    b_hi, b_lo = _split_bf16(b)
    return _dot(a_hi, b_hi) + _dot(a_lo, b_hi) + _dot(a_hi, b_lo)


def _norm_mod(x, g, shift, scale):
    ms = jnp.mean(x * x, axis=-1, keepdims=True)
    return (x * lax.rsqrt(ms + EPS) * g) * (1.0 + scale) + shift


def _params(*sem):
    return pltpu.CompilerParams(dimension_semantics=sem, vmem_limit_bytes=VMEM_LIMIT)


def _modvec_body(cond_ref, w_ref, b_ref, o_ref):
    cnd = cond_ref[...]
    s = cnd * jax.nn.sigmoid(cnd)
    s_hi, s_lo = _split_bf16(s)
    w_hi, w_lo = _split_bf16(w_ref[0])
    o_ref[0] = _dot(s_hi, w_hi) + _dot(s_lo, w_hi) + _dot(s_hi, w_lo) + b_ref[0]


def _modvec(cond, w_mod, b_mod):
    depth, d, n = w_mod.shape
    tn = n // 4
    return pl.pallas_call(
        _modvec_body,
        grid=(depth, n // tn),
        in_specs=[
            pl.BlockSpec((MOD_ROWS, d), lambda i, j: (0, 0)),
            pl.BlockSpec((1, d, tn), lambda i, j: (i, 0, j)),
            pl.BlockSpec((1, 1, tn), lambda i, j: (i, 0, j)),
        ],
        out_specs=pl.BlockSpec((1, MOD_ROWS, tn), lambda i, j: (i, 0, j)),
        out_shape=jax.ShapeDtypeStruct((depth, MOD_ROWS, n), F32),
        compiler_params=_params("parallel", "parallel"),
        name="modvec",
    )(cond, w_mod, b_mod.reshape(depth, 1, n))


def _tok_spec(width, tm):
    return pl.BlockSpec((1, tm, width), lambda b, j: (b, j, 0))


def _const_spec(shape):
    nd = len(shape)
    return pl.BlockSpec(shape, lambda b, j: (0,) * nd)


def _mod_spec(d, n_lat_tiles, n_samples):
    return pl.BlockSpec((1, 6, d), lambda b, j: (jnp.where(j < n_lat_tiles, b, n_samples), 0, 0))


def _qk_norm_rope(a, g_tile, gmat, cos, sin_signed, first_half):
    ss = _dot((a * a).astype(BF16), gmat)
    an = a * lax.rsqrt(ss * (1.0 / DA_HEAD_DIM) + EPS) * g_tile
    outs = []
    for ci in range(QK_COLS // LANES):
        ch = an[:, ci * LANES:(ci + 1) * LANES]
        partner = jnp.where(first_half, pltpu.roll(ch, LANES - 16, 1), pltpu.roll(ch, 16, 1))
        outs.append(ch * cos + partner * sin_signed)
    return jnp.concatenate(outs, axis=1)


def _inproj_body(x_ref, mod_ref, g1_ref, w_ref, cos_ref, sin_ref, qg_ref, kg_ref, gmat_ref,
                 sgg_ref, sgw_ref, sgb_ref, hy_ref, q_ref, k_ref, v_ref, sg_ref):
    tm = x_ref.shape[1]
    h = _norm_mod(x_ref[0], g1_ref[...], mod_ref[0, 0:1, :], mod_ref[0, 1:2, :]).astype(BF16)
    hy_ref[0] = _dot(h, w_ref[0, :, 0:HY_COLS])

    lane = lax.broadcasted_iota(jnp.int32, (1, LANES), 1)
    first_half = (lane % 32) < 16
    cos = cos_ref[...]
    sin_signed = sin_ref[...]
    gmat = gmat_ref[...]
    o = HY_COLS
    q = _qk_norm_rope(_dot(h, w_ref[0, :, o:o + QK_COLS]), qg_ref[...], gmat, cos, sin_signed, first_half)
    q_ref[0] = (q * (DA_HEAD_DIM ** -0.5 * math.log2(math.e))).astype(BF16)
    o += QK_COLS
    k = _qk_norm_rope(_dot(h, w_ref[0, :, o:o + QK_COLS]), kg_ref[...], gmat, cos, sin_signed, first_half)
    k_ref[0] = k.astype(BF16)
    o += QK_COLS
    v_ref[0] = _dot(h, w_ref[0, :, o:o + DA_HEADS * DA_V_DIM]).astype(BF16)
    o += DA_HEADS * DA_V_DIM

    z = jax.nn.gelu(_dot(h, w_ref[0, :, o:o + SG_COLS]))
    u = z[:, :SG_WIDTH]
    vv = z[:, SG_WIDTH:]
    vn = (vv * lax.rsqrt(jnp.mean(vv * vv, axis=-1, keepdims=True) + EPS) * sgg_ref[...]).astype(BF16)
    gw = SG_WIDTH // SG_GROUPS
    for ch in range(tm // SG_CHUNK):
        r0 = ch * SG_CHUNK
        cols = []
        for g in range(SG_GROUPS):
            s = _dot(sgw_ref[g], vn[r0:r0 + SG_CHUNK, g * gw:(g + 1) * gw]) + sgb_ref[:, g * gw:(g + 1) * gw]
            cols.append(u[r0:r0 + SG_CHUNK, g * gw:(g + 1) * gw] * s)
        sg_ref[0, r0:r0 + SG_CHUNK, :] = jnp.concatenate(cols, axis=1).astype(BF16)


def _inproj(xa, mod_i, g1, w_in, layer, cos_t, sin_t, qg, kg, gmat, sgg, sgw, sgb, n_lat_tiles, n_samples):
    b, t, d = xa.shape
    tm = TOK_TILE
    outs = [jax.ShapeDtypeStruct((b, t, HY_COLS), F32)] + [jax.ShapeDtypeStruct((b, t, QK_COLS), BF16)] * 4
    return pl.pallas_call(
        _inproj_body,
        grid=(b, t // tm),
        in_specs=[
            _tok_spec(d, tm),
            _mod_spec(d, n_lat_tiles, n_samples),
            _const_spec((1, d)),
            pl.BlockSpec((1, d, PRE_COLS), lambda bi, j: (layer, 0, 0)),
            pl.BlockSpec((tm, LANES), lambda bi, j: (j, 0)),
            pl.BlockSpec((tm, LANES), lambda bi, j: (j, 0)),
            _const_spec((1, QK_COLS)),
            _const_spec((1, QK_COLS)),
            _const_spec((QK_COLS, QK_COLS)),
            _const_spec((1, SG_WIDTH)),
            _const_spec((SG_GROUPS, SG_CHUNK, SG_CHUNK)),
            _const_spec((SG_CHUNK, SG_WIDTH)),
        ],
        out_specs=[_tok_spec(HY_COLS, tm)] + [_tok_spec(QK_COLS, tm)] * 4,
        out_shape=outs,
        compiler_params=_params("parallel", "parallel"),
        name="inproj",
    )(xa, mod_i, g1, w_in, cos_t, sin_t, qg, kg, gmat, sgg, sgw, sgb)


def _hyfilter_body(w1t_ref, w1c_ref, w1s_ref, b1_ref, w2_ref, b2_ref, fr_ref, w3f_ref, w3b_ref,
                   bands_ref, dl_ref, o_ref, *, seq):
    n = 2 * seq
    xi = lax.broadcasted_iota(jnp.int32, (1, n), 1)
    lag = xi - (seq - 1)
    pos = jnp.abs(lag).astype(F32)
    t = pos / (seq - 1)
    ang = 2.0 * math.pi * pos * bands_ref[...] / seq
    fr = fr_ref[...]
    z1 = w1t_ref[...] * t + _hdot(w1c_ref[...], jnp.cos(ang)) + _hdot(w1s_ref[...], jnp.sin(ang)) + b1_ref[...]
    h1 = jnp.sin(fr * z1)
    h2 = jnp.sin(fr * (_hdot(w2_ref[...], h1) + b2_ref[...]))
    bwd = _dot3(w3b_ref[...], h2[:, :seq])
    fwd = _dot3(w3f_ref[...], h2[:, seq:])
    fwd0 = _dot3(w3f_ref[...], h2[:, seq - LANES:seq])
    at_zero = jnp.where(lag[:, seq - LANES:seq] == 0, fwd0, 0.0)
    k = jnp.concatenate([bwd[:, :seq - LANES], bwd[:, seq - LANES:] + at_zero, fwd], axis=1)
    k = jnp.where(xi < n - 1, k * jnp.exp(-t * dl_ref[...]), 0.0)
    o_ref[...] = k / jnp.sum(jnp.abs(k), axis=-1, keepdims=True)


def _hyfilter(fw, seq):
    rows = HY_ORDER * HY_WIDTH
    rb = 256
    n = 2 * seq
    small = lambda shape: pl.BlockSpec(shape, lambda i: (0, 0))
    return pl.pallas_call(
        functools.partial(_hyfilter_body, seq=seq),
        grid=(rows // rb,),
        in_specs=[
            small((HY_FFN, 1)), small((HY_FFN, HY_BANDS)), small((HY_FFN, HY_BANDS)), small((HY_FFN, 1)),
            small((HY_FFN, HY_FFN)), small((HY_FFN, 1)), small((HY_FFN, 1)),
            pl.BlockSpec((rb, HY_FFN), lambda i: (i, 0)),
            pl.BlockSpec((rb, HY_FFN), lambda i: (i, 0)),
            small((HY_BANDS, 1)),
            pl.BlockSpec((rb, 1), lambda i: (i, 0)),
        ],
        out_specs=pl.BlockSpec((rb, n), lambda i: (i, 0)),
        out_shape=jax.ShapeDtypeStruct((rows, n), F32),
        compiler_params=_params("parallel"),
        name="hyfilter",
    )(*fw)


def _hyconv_body(sc_ref, z_ref, x1_ref, x2_ref, hml_ref, hmc_ref, o_ref, big_l, big_c, *, seq, ctx, cb):
    tot = seq + ctx
    lane = lax.broadcasted_iota(jnp.int32, (1, tot), 1)
    has_prev = jnp.logical_and(lane != 0, lane != seq)
    has_next = jnp.logical_and(lane != seq - 1, lane != tot - 1)
    row = lax.broadcasted_iota(jnp.int32, (LANES, LANES), 0)
    col = lax.broadcasted_iota(jnp.int32, (LANES, LANES), 1)
    lower = col <= row
    c0 = pl.program_id(0) * cb

    def short_conv(p, base, part):
        prev = jnp.where(has_prev, pltpu.roll(p, 1, 1), 0.0)
        nxt = jnp.where(has_next, pltpu.roll(p, tot - 1, 1), 0.0)
        return (sc_ref[base + 9 + part] + sc_ref[base + part] * prev
                + sc_ref[base + 3 + part] * p + sc_ref[base + 6 + part] * nxt)

    def build(hm_ref, o, ci, big_ref):
        n_rows = hm_ref.shape[2]
        prev = None
        for rp in range(n_rows - 1, -1, -1):
            r = hm_ref[o, ci, rp:rp + 1, :]
            cur = pltpu.roll(jnp.broadcast_to(r, (LANES, LANES)), 1, 1, stride=1, stride_axis=0)
            if prev is not None:
                rho = n_rows - 2 - rp
                big_ref[o, rho * LANES:(rho + 1) * LANES, :] = jnp.where(lower, cur, prev).astype(BF16)
            prev = cur

    def long_conv(zz, big_ref, o, length, base):
        nblk = length // CONV_TILE
        mid = (2 * nblk - 1) * LANES
        ys = [None] * nblk
        for dd in range(-(nblk - 1), nblk):
            r0 = mid - CONV_TILE * dd
            w = jnp.concatenate([big_ref[o, r0:r0 + CONV_TILE, :],
                                 big_ref[o, r0 - LANES:r0 - LANES + CONV_TILE, :]], axis=1)
            js = list(range(max(0, -dd), min(nblk, nblk - dd)))
            lhs = jnp.concatenate([zz[:, base + j * CONV_TILE:base + (j + 1) * CONV_TILE] for j in js], axis=0)
            out = _dot(lhs.astype(BF16), w)
            nb = zz.shape[0]
            for kk, j in enumerate(js):
                piece = out[kk * nb:(kk + 1) * nb]
                ys[j + dd] = piece if ys[j + dd] is None else ys[j + dd] + piece
        return ys

    def chan(ci, carry):
        base = (c0 + ci) * 16
        z = short_conv(z_ref[ci], base, 0)
        gates = (short_conv(x1_ref[ci], base, 1), short_conv(x2_ref[ci], base, 2))
        for o in range(HY_ORDER):
            build(hml_ref, o, ci, big_l)
            build(hmc_ref, o, ci, big_c)
        for o in range(HY_ORDER):
            y = jnp.concatenate(long_conv(z, big_l, o, seq, 0) + long_conv(z, big_c, o, ctx, seq), axis=1)
            z = gates[o] * (y + sc_ref[base + 12 + o] * z)
        o_ref[ci] = z
        return carry

    lax.fori_loop(0, cb, chan, 0, unroll=8)


def _hyconv(sc_tab, u3, hm_l, hm_c, seq, ctx):
    _, b, tot = u3.shape
    cb = HY_CHANNELS_PER_STEP
    nblk = HY_WIDTH // cb
    slab = lambda part: pl.BlockSpec((cb, b, tot), lambda i: (part * nblk + i, 0, 0))
    return pl.pallas_call(
        functools.partial(_hyconv_body, seq=seq, ctx=ctx, cb=cb),
        grid=(nblk,),
        in_specs=[
            pl.BlockSpec(memory_space=pltpu.SMEM),
            slab(0), slab(1), slab(2),
            pl.BlockSpec((HY_ORDER, cb, hm_l.shape[2], LANES), lambda i: (0, i, 0, 0)),
            pl.BlockSpec((HY_ORDER, cb, hm_c.shape[2], LANES), lambda i: (0, i, 0, 0)),
        ],
        out_specs=pl.BlockSpec((cb, b, tot), lambda i: (i, 0, 0)),
        out_shape=jax.ShapeDtypeStruct((HY_WIDTH, b, tot), F32),
        scratch_shapes=[
            pltpu.VMEM((HY_ORDER, (hm_l.shape[2] - 1) * LANES, LANES), BF16),
            pltpu.VMEM((HY_ORDER, (hm_c.shape[2] - 1) * LANES, LANES), BF16),
        ],
        compiler_params=_params("parallel"),
        name="hyconv",
    )(sc_tab, u3, u3, u3, hm_l, hm_c)


def _zero_after(x):
    bits = lax.bitcast_convert_type(x[-8:, -LANES:], jnp.uint32)
    z = lax.shift_right_logical(lax.shift_right_logical(bits, jnp.uint32(16)), jnp.uint32(16))
    return z[0:1].astype(F32).astype(BF16)


def _attn_body(lam_ref, q_ref, k_ref, v_ref, sub_ref, o_ref, *, lam_init, heads):
    hw = 2 * DA_HEAD_DIM
    lane = lax.broadcasted_iota(jnp.int32, (1, hw), 1)
    first = lane < DA_HEAD_DIM
    lp = lam_ref[...]
    lam = (jnp.exp(jnp.sum(lp[0:1] * lp[1:2], keepdims=True))
           - jnp.exp(jnp.sum(lp[2:3] * lp[3:4], keepdims=True)) + lam_init)
    sub = sub_ref[...] * (1.0 - lam_init)

    def scores(h, after):
        q = q_ref[0, :, h * hw:(h + 1) * hw]
        if after is not None:
            q = q + after
        k = k_ref[0, :, h * hw:(h + 1) * hw]
        zero = jnp.zeros_like(q)
        return _dot_nt(jnp.where(first, q, zero), k), _dot_nt(jnp.where(first, zero, q), k)

    def weights(s1, s2):
        p1 = jnp.exp2(s1 - jnp.max(s1, axis=-1, keepdims=True))
        p2 = jnp.exp2(s2 - jnp.max(s2, axis=-1, keepdims=True))
        l1 = jnp.sum(p1, axis=-1, keepdims=True)
        l2 = jnp.sum(p2, axis=-1, keepdims=True)
        return (p1 - p2 * (lam * l1 / l2)).astype(BF16), 1.0 / l1

    def values(h, w):
        o = _dot(w[0], v_ref[0, :, h * hw:(h + 1) * hw]) * w[1]
        on = o * lax.rsqrt(jnp.mean(o * o, axis=-1, keepdims=True) + EPS) * sub
        o_ref[0, :, h * hw:(h + 1) * hw] = on.astype(BF16)

    s = [None] * heads
    w = [None] * heads
    after = None
    for step in range(heads + 2):
        if step < heads:
            s[step] = scores(step, after)
            after = _zero_after(s[step][1])
        if 0 <= step - 1 < heads:
            w[step - 1] = weights(*s[step - 1])
            s[step - 1] = None
        if 0 <= step - 2 < heads:
            values(step - 2, w[step - 2])


def _attention(lam_p, q, k, v, sub, lam_init, seq):
    b, tot, _ = q.shape
    ctx = tot - seq
    tq = TOK_TILE
    hw = 2 * DA_HEAD_DIM
    width = DA_HEADS * hw
    lat = pl.pallas_call(
        functools.partial(_attn_body, lam_init=lam_init, heads=DA_HEADS),
        grid=(b, seq // tq),
        in_specs=[
            pl.BlockSpec((4, DA_HEAD_DIM), lambda bi, j: (0, 0)),
            pl.BlockSpec((1, tq, width), lambda bi, j: (bi, j, 0)),
            pl.BlockSpec((1, tot, width), lambda bi, j: (bi, 0, 0)),
            pl.BlockSpec((1, tot, width), lambda bi, j: (bi, 0, 0)),
            pl.BlockSpec((1, hw), lambda bi, j: (0, 0)),
        ],
        out_specs=pl.BlockSpec((1, tq, width), lambda bi, j: (bi, j, 0)),
        out_shape=jax.ShapeDtypeStruct((b, seq, width), BF16),
        compiler_params=_params("parallel", "parallel"),
        name="diffattn",
    )(lam_p, q, k, v, sub)
    body = functools.partial(_attn_body, lam_init=lam_init, heads=1)
    small = [pl.BlockSpec((4, DA_HEAD_DIM), lambda bi, h, j: (0, 0)), pl.BlockSpec((1, hw), lambda bi, h, j: (0, 0))]
    q0 = seq // tq
    k0 = seq // ctx
    ctx_out = pl.pallas_call(
        body,
        grid=(b, DA_HEADS, ctx // tq),
        in_specs=[
            small[0],
            pl.BlockSpec((1, tq, hw), lambda bi, h, j: (bi, q0 + j, h)),
            pl.BlockSpec((1, ctx, hw), lambda bi, h, j: (bi, k0, h)),
            pl.BlockSpec((1, ctx, hw), lambda bi, h, j: (bi, k0, h)),
            small[1],
        ],
        out_specs=pl.BlockSpec((1, tq, hw), lambda bi, h, j: (bi, j, h)),
        out_shape=jax.ShapeDtypeStruct((b, ctx, DA_HEADS * DA_V_DIM), BF16),
        compiler_params=_params("parallel", "parallel", "parallel"),
        name="diffattn_ctx",
    )(lam_p, q, k, v, sub)
    return jnp.concatenate([lat, ctx_out], axis=1)


def _merge_body(x_ref, mod_ref, g1_ref, wg0_ref, wg1_ref, wg2_ref, yh_ref, yd_ref, ys_ref, wb_ref, wo_ref, o_ref):
    x = x_ref[0]
    h = _norm_mod(x, g1_ref[...], mod_ref[0, 0:1, :], mod_ref[0, 1:2, :]).astype(BF16)
    acc = None
    for n, (wg_ref, y_ref) in enumerate(((wg0_ref, yh_ref), (wg1_ref, yd_ref), (wg2_ref, ys_ref))):
        gate = jax.nn.sigmoid(_dot(h, wg_ref[0]))
        term = gate * _dot(y_ref[0], wb_ref[0, n])
        acc = term if acc is None else acc + term
    o_ref[0] = x + mod_ref[0, 2:3, :] * _dot(acc.astype(BF16), wo_ref[0])


def _merge(xa, mod_i, g1, w_in, layer, y_hy, y_da, y_sg, w_br, w_out, rows, n_lat_tiles, n_samples):
    b, _, d = xa.shape
    t = rows
    tm = TOK_TILE
    assert PRE_COLS % d == 0
    gate_spec = lambda n: pl.BlockSpec((1, d, d), lambda bi, j: (layer, 0, PRE_COLS // d + n))
    return pl.pallas_call(
        _merge_body,
        grid=(b, t // tm),
        in_specs=[
            _tok_spec(d, tm), _mod_spec(d, n_lat_tiles, n_samples), _const_spec((1, d)),
            gate_spec(0), gate_spec(1), gate_spec(2),
            _tok_spec(HY_WIDTH, tm), _tok_spec(HY_WIDTH, tm), _tok_spec(HY_WIDTH, tm),
            pl.BlockSpec((1, 3, HY_WIDTH, d), lambda bi, j: (layer, 0, 0, 0)),
            pl.BlockSpec((1, d, d), lambda bi, j: (layer, 0, 0)),
        ],
        out_specs=_tok_spec(d, tm),
        out_shape=jax.ShapeDtypeStruct((b, t, d), F32),
        compiler_params=_params("parallel", "parallel"),
        name="merge",
    )(xa, mod_i, g1, w_in, w_in, w_in, y_hy, y_da, y_sg, w_br, w_out)


def _ffn_body(x_ref, mod_ref, g2_ref, wg_ref, wu_ref, wd_ref, o_ref, *, chunk):
    x = x_ref[0]
    h = _norm_mod(x, g2_ref[...], mod_ref[0, 3:4, :], mod_ref[0, 4:5, :]).astype(BF16)
    ff = wg_ref.shape[2]
    acc = None
    for f0 in range(0, ff, chunk):
        f1 = min(ff, f0 + chunk)
        a = _dot(h, wg_ref[0, :, f0:f1])
        mid = (a * jax.nn.sigmoid(a) * _dot(h, wu_ref[0, :, f0:f1])).astype(BF16)
        term = _dot(mid, wd_ref[0, f0:f1, :])
        acc = term if acc is None else acc + term
    o_ref[0] = x + mod_ref[0, 5:6, :] * acc


def _ffn_dense(xa, mod_i, g2, wg, wu, wd, layer, n_lat_tiles, n_samples):
    b, t, d = xa.shape
    ff = wg.shape[2]
    tm = TOK_TILE
    return pl.pallas_call(
        functools.partial(_ffn_body, chunk=1024),
        grid=(b, t // tm),
        in_specs=[
            _tok_spec(d, tm), _mod_spec(d, n_lat_tiles, n_samples), _const_spec((1, d)),
            pl.BlockSpec((1, d, ff), lambda bi, j: (layer, 0, 0)),
            pl.BlockSpec((1, d, ff), lambda bi, j: (layer, 0, 0)),
            pl.BlockSpec((1, ff, d), lambda bi, j: (layer, 0, 0)),
        ],
        out_specs=_tok_spec(d, tm),
        out_shape=jax.ShapeDtypeStruct((b, t, d), F32),
        compiler_params=_params("parallel", "parallel"),
        name="ffn_dense",
    )(xa, mod_i, g2, wg, wu, wd)


def _router_body(x_ref, mod_ref, g2_ref, wr_ref, h_ref, r_ref):
    h = _norm_mod(x_ref[0], g2_ref[...], mod_ref[0, 3:4, :], mod_ref[0, 4:5, :])
    h_ref[0] = h
    h_hi, h_lo = _split_bf16(h)
    w_hi, w_lo = _split_bf16(wr_ref[...])
    logits = _dot_nt(w_hi, h_hi) + _dot_nt(w_hi, h_lo) + _dot_nt(w_lo, h_hi)
    eid = lax.broadcasted_iota(jnp.int32, logits.shape, 0)
    m1 = jnp.max(logits, axis=0, keepdims=True)
    i1 = jnp.min(jnp.where(logits == m1, eid, N_EXPERTS), axis=0, keepdims=True)
    rest = jnp.where(eid == i1, -jnp.inf, logits)
    m2 = jnp.max(rest, axis=0, keepdims=True)
    i2 = jnp.min(jnp.where(rest == m2, eid, N_EXPERTS), axis=0, keepdims=True)
    w1 = 1.0 / (1.0 + jnp.exp(m2 - m1))
    rows = lax.broadcasted_iota(jnp.int32, logits.shape, 0)
    out = jnp.where(rows == 0, i1.astype(F32), jnp.where(rows == 1, i2.astype(F32),
                    jnp.where(rows == 2, w1, jnp.where(rows == 3, 1.0 - w1, 0.0))))
    r_ref[0] = out


def _router(xa, mod_i, g2, wr_t, n_lat_tiles, n_samples):
    b, t, d = xa.shape
    tm = TOK_TILE
    return pl.pallas_call(
        _router_body,
        grid=(b, t // tm),
        in_specs=[
            _tok_spec(d, tm), _mod_spec(d, n_lat_tiles, n_samples), _const_spec((1, d)),
            _const_spec((N_EXPERTS, d)),
        ],
        out_specs=[_tok_spec(d, tm), pl.BlockSpec((1, N_EXPERTS, tm), lambda bi, j: (bi, 0, j))],
        out_shape=[jax.ShapeDtypeStruct((b, t, d), F32), jax.ShapeDtypeStruct((b, N_EXPERTS, t), F32)],
        compiler_params=_params("parallel", "parallel"),
        name="router",
    )(xa, mod_i, g2, wr_t)


def _row_copy(src_hbm, dst, sem, src_row, dst_row):
    return pltpu.make_async_copy(src_hbm.at[pl.ds(src_row, 1), :], dst.at[pl.ds(dst_row, 1), :], sem)


def _gather_rows(src_hbm, dst, sem, idx):
    rows = dst.shape[0]

    def start(g, c):
        for u in range(DMA_UNROLL):
            r = g * DMA_UNROLL + u
            _row_copy(src_hbm, dst, sem, idx(r), r).start(priority=u % 2)
        return c
    lax.fori_loop(0, rows // DMA_UNROLL, start, 0)


def _wait_rows(src_hbm, dst, sem):
    pltpu.make_async_copy(src_hbm.at[pl.ds(0, dst.shape[0]), :], dst, sem).wait()


def _expert_body(te_ref, nu_ref, rt_ref, rtn_ref, h_hbm, wg_ref, wu_ref, wd_ref, o_ref, xbuf, xbf, acc, sem, *, chunk):
    t = pl.program_id(0)
    f = pl.program_id(1)
    last_f = pl.num_programs(1) - 1
    n_used = nu_ref[0]
    used = t < n_used
    slot = t % 2

    @pl.when(jnp.logical_and(f == 0, jnp.logical_and(used, t == 0)))
    def _first_gather():
        _gather_rows(h_hbm, xbuf.at[0], sem.at[0], lambda r: rt_ref[0, 0, r])

    @pl.when(jnp.logical_and(used, f == 0))
    def _stage():
        _wait_rows(h_hbm, xbuf.at[slot], sem.at[slot])
        xbf[...] = xbuf[slot].astype(BF16)
        acc[...] = jnp.zeros_like(acc)

    @pl.when(used)
    def _compute():
        xb = xbf[...]
        a = _dot(xb, wg_ref[0, 0].astype(BF16))
        b = _dot(xb, wu_ref[0, 0].astype(BF16))
        for u in range(chunk):
            r = f * chunk + u
            _row_copy(h_hbm, xbuf.at[1 - slot], sem.at[1 - slot], rtn_ref[0, 0, r], r).start(priority=u % 2)
        mid = (a * jax.nn.sigmoid(a) * b).astype(BF16)
        acc[...] += _dot(mid, wd_ref[0, 0].astype(BF16))

    @pl.when(f == last_f)
    def _store():
        o_ref[...] = jnp.where(used, acc[...], 0.0)

    @pl.when(jnp.logical_and(f == last_f, t == n_used - 1))
    def _drain():
        _wait_rows(h_hbm, xbuf.at[1 - slot], sem.at[1 - slot])


def _experts(tile_expert, n_used, row_token, h_flat, wg, wu, wd, layer):
    n_rows = row_token.shape[0]
    d = h_flat.shape[1]
    ff = wg.shape[3]
    nt = n_rows // ROW_TILE
    rt3 = row_token.reshape(nt, 1, ROW_TILE)
    grid_spec = pltpu.PrefetchScalarGridSpec(
        num_scalar_prefetch=2,
        grid=(nt, ff // FF_TILE),
        in_specs=[
            pl.BlockSpec((1, 1, ROW_TILE), lambda t, f, te, nu: (t, 0, 0), memory_space=pltpu.SMEM),
            pl.BlockSpec((1, 1, ROW_TILE), lambda t, f, te, nu: (jnp.minimum(t + 1, nt - 1), 0, 0),
                         memory_space=pltpu.SMEM),
            pl.BlockSpec(memory_space=pl.ANY),
            pl.BlockSpec((1, 1, d, FF_TILE), lambda t, f, te, nu: (layer, te[t], 0, f)),
            pl.BlockSpec((1, 1, d, FF_TILE), lambda t, f, te, nu: (layer, te[t], 0, f)),
            pl.BlockSpec((1, 1, FF_TILE, d), lambda t, f, te, nu: (layer, te[t], f, 0)),
        ],
        out_specs=pl.BlockSpec((ROW_TILE, d), lambda t, f, te, nu: (t, 0)),
        scratch_shapes=[
            pltpu.VMEM((2, ROW_TILE, d), F32),
            pltpu.VMEM((ROW_TILE, d), BF16),
            pltpu.VMEM((ROW_TILE, d), F32),
            pltpu.SemaphoreType.DMA((2,)),
        ],
    )
    assert ROW_TILE % (ff // FF_TILE) == 0
    return pl.pallas_call(
        functools.partial(_expert_body, chunk=ROW_TILE // (ff // FF_TILE)),
        grid_spec=grid_spec,
        out_shape=jax.ShapeDtypeStruct((n_rows, d), F32),
        compiler_params=_params("arbitrary", "arbitrary"),
        name="experts",
    )(tile_expert, n_used, rt3, rt3, h_flat, wg, wu, wd)


def _combine_body(pos_ref, posn_ref, r_ref, x_ref, mod_ref, ye_hbm, o_ref, buf, sem):
    tm = x_ref.shape[1]
    step = pl.program_id(0) * pl.num_programs(1) + pl.program_id(1)
    n_steps = pl.num_programs(0) * pl.num_programs(1)
    slot = step % 2

    def gather(p_ref, s):
        for kk in range(2):
            _gather_rows(ye_hbm, buf.at[s, kk], sem.at[s, kk], lambda r, kk=kk: p_ref[0, kk, r])

    @pl.when(step == 0)
    def _first():
        gather(pos_ref, 0)

    @pl.when(step + 1 < n_steps)
    def _next():
        gather(posn_ref, 1 - slot)

    eye = (lax.broadcasted_iota(jnp.int32, (tm, tm), 0) == lax.broadcasted_iota(jnp.int32, (tm, tm), 1)).astype(BF16)
    r_hi, r_lo = _split_bf16(r_ref[0])
    rcol = _dot_nt(eye, r_hi) + _dot_nt(eye, r_lo)
    for kk in range(2):
        _wait_rows(ye_hbm, buf.at[slot, kk], sem.at[slot, kk])
    mix = buf[slot, 0] * rcol[:, 2:3] + buf[slot, 1] * rcol[:, 3:4]
    o_ref[0] = x_ref[0] + mod_ref[0, 5:6, :] * mix


def _combine(pos, route, xa, mod_i, ye, n_lat_tiles, n_samples):
    b, t, d = xa.shape
    tm = TOK_TILE
    nj = t // tm
    return pl.pallas_call(
        _combine_body,
        grid=(b, nj),
        in_specs=[
            pl.BlockSpec((1, 2, tm), lambda bi, j: (bi * nj + j, 0, 0), memory_space=pltpu.SMEM),
            pl.BlockSpec((1, 2, tm), lambda bi, j: (jnp.minimum(bi * nj + j + 1, b * nj - 1), 0, 0),
                         memory_space=pltpu.SMEM),
            pl.BlockSpec((1, N_EXPERTS, tm), lambda bi, j: (bi, 0, j)),
            _tok_spec(d, tm), _mod_spec(d, n_lat_tiles, n_samples),
            pl.BlockSpec(memory_space=pl.ANY),
        ],
        out_specs=_tok_spec(d, tm),
        out_shape=jax.ShapeDtypeStruct((b, t, d), F32),
        scratch_shapes=[pltpu.VMEM((2, 2, tm, d), F32), pltpu.SemaphoreType.DMA((2, 2))],
        compiler_params=_params("arbitrary", "arbitrary"),
        name="moe_combine",
    )(pos, pos, route, xa, mod_i, ye)


def _moe(xa, mod_i, g2, wr_t, wg, wu, wd, layer, n_lat_tiles, n_samples):
    b, t, d = xa.shape
    h, route = _router(xa, mod_i, g2, wr_t, n_lat_tiles, n_samples)
    n_tok = b * t
    expert = route[:, 0:2, :].astype(jnp.int32)
    onehot = (expert.reshape(-1)[:, None] == jnp.arange(N_EXPERTS)[None, :]).astype(jnp.int32)
    counts = jnp.sum(onehot, axis=0)
    rank = jnp.sum((jnp.cumsum(onehot, axis=0) - onehot) * onehot, axis=1)
    padded = ((counts + ROW_TILE - 1) // ROW_TILE) * ROW_TILE
    ends = jnp.cumsum(padded)
    pos = ((ends - padded)[expert.reshape(-1)] + rank).reshape(b, 2, t)
    n_rows = ((2 * n_tok + N_EXPERTS * (ROW_TILE - 1)) // ROW_TILE + 1) * ROW_TILE
    token = jnp.broadcast_to(jnp.arange(b)[:, None, None] * t + jnp.arange(t)[None, None, :], (b, 2, t))
    row_token = jnp.zeros((n_rows,), jnp.int32).at[pos.reshape(-1)].set(token.reshape(-1))
    tile_start = jnp.arange(n_rows // ROW_TILE) * ROW_TILE
    tile_expert = jnp.minimum(jnp.sum(tile_start[:, None] >= ends[None, :], axis=1), N_EXPERTS - 1).astype(jnp.int32)
    n_used = (ends[-1:] // ROW_TILE).astype(jnp.int32)
    ye = _experts(tile_expert, n_used, row_token, h.reshape(n_tok, d), wg, wu, wd, layer)
    nj = t // TOK_TILE
    pos_tiles = pos.reshape(b, 2, nj, TOK_TILE).transpose(0, 2, 1, 3).reshape(b * nj, 2, TOK_TILE)
    return _combine(pos_tiles, route, xa, mod_i, ye, n_lat_tiles, n_samples)


def _rope_tables(seq, ctx):
    rows = seq // GRID_W
    row = jnp.repeat(jnp.arange(rows), GRID_W).astype(F32)
    col = jnp.tile(jnp.arange(GRID_W), rows).astype(F32)
    half = DA_HEAD_DIM // 2
    inv = ROPE_BASE ** (-jnp.arange(0, half, 2, dtype=F32) / half)
    ang = jnp.concatenate([row[:, None] * inv, row[:, None] * inv, col[:, None] * inv, col[:, None] * inv], axis=1)
    sign = jnp.tile(jnp.concatenate([-jnp.ones((half // 2,), F32), jnp.ones((half // 2,), F32)]), 2)
    cos = jnp.concatenate([jnp.cos(ang), jnp.ones((ctx, DA_HEAD_DIM), F32)], axis=0)
    sin = jnp.concatenate([jnp.sin(ang) * sign, jnp.zeros((ctx, DA_HEAD_DIM), F32)], axis=0)
    return jnp.tile(cos, (1, LANES // DA_HEAD_DIM)), jnp.tile(sin, (1, LANES // DA_HEAD_DIM))


def _filter_weights(w1, b1, w2, b2, w3, freq):
    col = lambda v: v.reshape(-1, 1)
    bands = jnp.linspace(1e-4, HY_BANDS - 1, HY_BANDS, dtype=F32)
    deltas = jnp.abs(jnp.linspace(HY_MIN_DECAY, HY_MAX_DECAY, HY_WIDTH, dtype=F32))
    half = HY_ORDER * HY_WIDTH
    return (w1[0:1].T, w1[1:1 + HY_BANDS].T, w1[1 + HY_BANDS:].T, col(b1), w2.T, col(b2), col(freq),
            w3[:, :half].T, w3[:, half:].T, col(bands), col(jnp.tile(deltas, HY_ORDER)))


def kernel(x, c, ctx, c_ctx, w_mod, b_mod, norm1_g, norm2_g, w_in, hy_conv_w, hy_conv_b, hy_f_w1, hy_f_b1, hy_f_w2, hy_f_b2, hy_f_w3, hy_f_freq, hy_skip, q_norm_g, k_norm_g, lam_q1, lam_k1, lam_q2, lam_k2, subln_g, sg_norm_g, sg_w, sg_b, w_branch, w_out, ffn_wg, ffn_wu, ffn_wd, router_w, moe_wg, moe_wu, moe_wd):
    n_samples, seq, d = x.shape
    n_ctx = ctx.shape[1]
    depth = w_mod.shape[0]
    assert seq % CONV_TILE == 0 and n_ctx % CONV_TILE == 0 and seq % GRID_W == 0 and seq % n_ctx == 0
    assert n_samples + 1 <= MOD_ROWS
    n_lat_tiles = seq // TOK_TILE
    tot = seq + n_ctx

    xa = jnp.concatenate([x, ctx], axis=1)
    cond = jnp.zeros((MOD_ROWS, d), F32).at[:n_samples].set(c).at[n_samples].set(c_ctx)
    mod = _modvec(cond, w_mod, b_mod).reshape(depth, MOD_ROWS, 6, d)

    cos_t, sin_t = _rope_tables(seq, n_ctx)
    group = jnp.arange(QK_COLS) // DA_HEAD_DIM
    gmat = (group[:, None] == group[None, :]).astype(BF16)
    w_in_b = w_in.astype(BF16)
    w_br_b = w_branch.astype(BF16)
    w_out_b = w_out.astype(BF16)
    ffn_b = [w.astype(BF16) for w in (ffn_wg, ffn_wu, ffn_wd)]

    for i in range(depth):
        lam_init = 0.8 - 0.6 * math.exp(-0.3 * i)
        g1 = norm1_g[i].reshape(1, d)
        g2 = norm2_g[i].reshape(1, d)
        hy, q, k, v, y_sg = _inproj(
            xa, mod[i], g1, w_in_b, i, cos_t, sin_t,
            jnp.tile(q_norm_g[i], QK_COLS // DA_HEAD_DIM).reshape(1, QK_COLS),
            jnp.tile(k_norm_g[i], QK_COLS // DA_HEAD_DIM).reshape(1, QK_COLS),
            gmat, sg_norm_g[i].reshape(1, SG_WIDTH), sg_w[i].astype(BF16),
            jnp.repeat(sg_b[i].T, SG_WIDTH // SG_GROUPS, axis=1), n_lat_tiles, n_samples)

        fw = _filter_weights(hy_f_w1[i], hy_f_b1[i], hy_f_w2[i], hy_f_b2[i], hy_f_w3[i], hy_f_freq[i])
        hm_l = _hyfilter(fw, seq).reshape(HY_ORDER, HY_WIDTH, 2 * seq // LANES, LANES)
        hm_c = _hyfilter(fw, n_ctx).reshape(HY_ORDER, HY_WIDTH, 2 * n_ctx // LANES, LANES)
        cw = hy_conv_w[i].reshape(3, 3, HY_WIDTH)
        cbias = hy_conv_b[i].reshape(3, HY_WIDTH)
        sc_tab = jnp.concatenate([cw.reshape(9, HY_WIDTH), cbias, hy_skip[i], jnp.zeros((2, HY_WIDTH), F32)],
                                 axis=0).T.reshape(-1)
        y_hy = _hyconv(sc_tab, jnp.transpose(hy, (2, 0, 1)), hm_l, hm_c, seq, n_ctx)
        y_hy = jnp.transpose(y_hy, (1, 2, 0)).astype(BF16)

        lam_p = jnp.stack([lam_q1[i], lam_k1[i], lam_q2[i], lam_k2[i]], axis=0)
        y_da = _attention(lam_p, q, k, v, subln_g[i].reshape(1, DA_V_DIM), lam_init, seq)

        xa = _merge(xa, mod[i], g1, w_in_b, i, y_hy, y_da, y_sg, w_br_b, w_out_b,
                    seq if i == depth - 1 else tot, n_lat_tiles, n_samples)

        j = i // 2
        if i % 2 == 0:
            xa = _ffn_dense(xa, mod[i], g2, *ffn_b, j, n_lat_tiles, n_samples)
        else:
            xa = _moe(xa, mod[i], g2, router_w[j].T, moe_wg, moe_wu, moe_wd, j, n_lat_tiles, n_samples)
    return xa[:, :seq]
```

```python
import functools
import math

import jax
import jax.numpy as jnp
from jax import lax
from jax.experimental import pallas as pl
from jax.experimental.pallas import tpu as pltpu

F32 = jnp.float32
BF16 = jnp.bfloat16
EPS = 1e-6

GRID_W = 64
HY_WIDTH = 512
HY_ORDER = 2
HY_BANDS = 16
HY_FFN = 64
HY_MAX_DECAY = math.log(1e-2) / 0.3
HY_MIN_DECAY = math.log(1e-2) / 1.5
DA_HEADS = 4
DA_HEAD_DIM = 64
DA_V_DIM = 2 * DA_HEAD_DIM
ROPE_BASE = 10000.0
SG_WIDTH = 512
SG_GROUPS = 4
SG_CHUNK = 128
N_EXPERTS = 8
HY_COLS = 3 * HY_WIDTH
QK_COLS = DA_HEADS * 2 * DA_HEAD_DIM
DA_COLS = 2 * QK_COLS + DA_HEADS * DA_V_DIM
SG_COLS = 2 * SG_WIDTH
PRE_COLS = HY_COLS + DA_COLS + SG_COLS

LANES = 128
TOK_TILE = 256
CONV_TILE = 256
HY_CHANNELS_PER_STEP = 16
ROW_TILE = 896
FF_TILE = 512
DMA_UNROLL = 8
MOD_ROWS = 16
VMEM_LIMIT = 56 * 1024 * 1024


def _dot(a, b):
    return jnp.dot(a, b, preferred_element_type=F32)


def _dot_nt(a, b):
    return lax.dot_general(a, b, (((1,), (1,)), ((), ())), preferred_element_type=F32)


_hdot = functools.partial(jnp.dot, precision=lax.Precision.HIGHEST, preferred_element_type=F32)


def _split_bf16(a):
    hi = a.astype(BF16)
    return hi, (a - hi.astype(F32)).astype(BF16)


def _dot3(a, b):
    a_hi, a_lo = _split_bf16(a)
    b_hi, b_lo = _split_bf16(b)
    return _dot(a_hi, b_hi) + _dot(a_lo, b_hi) + _dot(a_hi, b_lo)


def _norm_mod(x, g, shift, scale):
    ms = jnp.mean(x * x, axis=-1, keepdims=True)
    return (x * lax.rsqrt(ms + EPS) * g) * (1.0 + scale) + shift


def _params(*sem):
    return pltpu.CompilerParams(dimension_semantics=sem, vmem_limit_bytes=VMEM_LIMIT)


def _modvec_body(cond_ref, w_ref, b_ref, o_ref):
    cnd = cond_ref[...]
    s = cnd * jax.nn.sigmoid(cnd)
    s_hi, s_lo = _split_bf16(s)
    w_hi, w_lo = _split_bf16(w_ref[0])
    o_ref[0] = _dot(s_hi, w_hi) + _dot(s_lo, w_hi) + _dot(s_hi, w_lo) + b_ref[0]


def _modvec(cond, w_mod, b_mod):
    depth, d, n = w_mod.shape
    tn = n // 4
    return pl.pallas_call(
        _modvec_body,
        grid=(depth, n // tn),
        in_specs=[
            pl.BlockSpec((MOD_ROWS, d), lambda i, j: (0, 0)),
            pl.BlockSpec((1, d, tn), lambda i, j: (i, 0, j)),
            pl.BlockSpec((1, 1, tn), lambda i, j: (i, 0, j)),
        ],
        out_specs=pl.BlockSpec((1, MOD_ROWS, tn), lambda i, j: (i, 0, j)),
        out_shape=jax.ShapeDtypeStruct((depth, MOD_ROWS, n), F32),
        compiler_params=_params("parallel", "parallel"),
        name="modvec",
    )(cond, w_mod, b_mod.reshape(depth, 1, n))


def _tok_spec(width, tm):
    return pl.BlockSpec((1, tm, width), lambda b, j: (b, j, 0))


def _const_spec(shape):
    nd = len(shape)
    return pl.BlockSpec(shape, lambda b, j: (0,) * nd)


def _mod_spec(d, n_lat_tiles, n_samples):
    return pl.BlockSpec((1, 6, d), lambda b, j: (jnp.where(j < n_lat_tiles, b, n_samples), 0, 0))


def _qk_norm_rope(a, g_tile, gmat, cos, sin_signed, first_half):
    ss = _dot((a * a).astype(BF16), gmat)
    an = a * lax.rsqrt(ss * (1.0 / DA_HEAD_DIM) + EPS) * g_tile
    outs = []
    for ci in range(QK_COLS // LANES):
        ch = an[:, ci * LANES:(ci + 1) * LANES]
        partner = jnp.where(first_half, pltpu.roll(ch, LANES - 16, 1), pltpu.roll(ch, 16, 1))
        outs.append(ch * cos + partner * sin_signed)
    return jnp.concatenate(outs, axis=1)


def _inproj_body(x_ref, mod_ref, g1_ref, w_ref, cos_ref, sin_ref, qg_ref, kg_ref, gmat_ref,
                 sgg_ref, sgw_ref, sgb_ref, hy_ref, q_ref, k_ref, v_ref, sg_ref):
    tm = x_ref.shape[1]
    h = _norm_mod(x_ref[0], g1_ref[...], mod_ref[0, 0:1, :], mod_ref[0, 1:2, :]).astype(BF16)
    hy_ref[0] = _dot(h, w_ref[0, :, 0:HY_COLS])

    lane = lax.broadcasted_iota(jnp.int32, (1, LANES), 1)
    first_half = (lane % 32) < 16
    cos = cos_ref[...]
    sin_signed = sin_ref[...]
    gmat = gmat_ref[...]
    o = HY_COLS
    q = _qk_norm_rope(_dot(h, w_ref[0, :, o:o + QK_COLS]), qg_ref[...], gmat, cos, sin_signed, first_half)
    q_ref[0] = (q * (DA_HEAD_DIM ** -0.5 * math.log2(math.e))).astype(BF16)
    o += QK_COLS
    k = _qk_norm_rope(_dot(h, w_ref[0, :, o:o + QK_COLS]), kg_ref[...], gmat, cos, sin_signed, first_half)
    k_ref[0] = k.astype(BF16)
    o += QK_COLS
    v_ref[0] = _dot(h, w_ref[0, :, o:o + DA_HEADS * DA_V_DIM]).astype(BF16)
    o += DA_HEADS * DA_V_DIM

    z = jax.nn.gelu(_dot(h, w_ref[0, :, o:o + SG_COLS]))
    u = z[:, :SG_WIDTH]
    vv = z[:, SG_WIDTH:]
    vn = (vv * lax.rsqrt(jnp.mean(vv * vv, axis=-1, keepdims=True) + EPS) * sgg_ref[...]).astype(BF16)
    gw = SG_WIDTH // SG_GROUPS
    for ch in range(tm // SG_CHUNK):
        r0 = ch * SG_CHUNK
        cols = []
        for g in range(SG_GROUPS):
            s = _dot(sgw_ref[g], vn[r0:r0 + SG_CHUNK, g * gw:(g + 1) * gw]) + sgb_ref[:, g * gw:(g + 1) * gw]
            cols.append(u[r0:r0 + SG_CHUNK, g * gw:(g + 1) * gw] * s)
        sg_ref[0, r0:r0 + SG_CHUNK, :] = jnp.concatenate(cols, axis=1).astype(BF16)


def _inproj(xa, mod_i, g1, w_in, layer, cos_t, sin_t, qg, kg, gmat, sgg, sgw, sgb, n_lat_tiles, n_samples):
    b, t, d = xa.shape
    tm = TOK_TILE
    outs = [jax.ShapeDtypeStruct((b, t, HY_COLS), F32)] + [jax.ShapeDtypeStruct((b, t, QK_COLS), BF16)] * 4
    return pl.pallas_call(
        _inproj_body,
        grid=(b, t // tm),
        in_specs=[
            _tok_spec(d, tm),
            _mod_spec(d, n_lat_tiles, n_samples),
            _const_spec((1, d)),
            pl.BlockSpec((1, d, PRE_COLS), lambda bi, j: (layer, 0, 0)),
            pl.BlockSpec((tm, LANES), lambda bi, j: (j, 0)),
            pl.BlockSpec((tm, LANES), lambda bi, j: (j, 0)),
            _const_spec((1, QK_COLS)),
            _const_spec((1, QK_COLS)),
            _const_spec((QK_COLS, QK_COLS)),
            _const_spec((1, SG_WIDTH)),
            _const_spec((SG_GROUPS, SG_CHUNK, SG_CHUNK)),
            _const_spec((SG_CHUNK, SG_WIDTH)),
        ],
        out_specs=[_tok_spec(HY_COLS, tm)] + [_tok_spec(QK_COLS, tm)] * 4,
        out_shape=outs,
        compiler_params=_params("parallel", "parallel"),
        name="inproj",
    )(xa, mod_i, g1, w_in, cos_t, sin_t, qg, kg, gmat, sgg, sgw, sgb)


def _hyfilter_body(w1t_ref, w1c_ref, w1s_ref, b1_ref, w2_ref, b2_ref, fr_ref, w3f_ref, w3b_ref,
                   bands_ref, dl_ref, o_ref, *, seq):
    n = 2 * seq
    xi = lax.broadcasted_iota(jnp.int32, (1, n), 1)
    lag = xi - (seq - 1)
    pos = jnp.abs(lag).astype(F32)
    t = pos / (seq - 1)
    ang = 2.0 * math.pi * pos * bands_ref[...] / seq
    fr = fr_ref[...]
    z1 = w1t_ref[...] * t + _hdot(w1c_ref[...], jnp.cos(ang)) + _hdot(w1s_ref[...], jnp.sin(ang)) + b1_ref[...]
    h1 = jnp.sin(fr * z1)
    h2 = jnp.sin(fr * (_hdot(w2_ref[...], h1) + b2_ref[...]))
    bwd = _dot3(w3b_ref[...], h2[:, :seq])
    fwd = _dot3(w3f_ref[...], h2[:, seq:])
    fwd0 = _dot3(w3f_ref[...], h2[:, seq - LANES:seq])
    at_zero = jnp.where(lag[:, seq - LANES:seq] == 0, fwd0, 0.0)
    k = jnp.concatenate([bwd[:, :seq - LANES], bwd[:, seq - LANES:] + at_zero, fwd], axis=1)
    k = jnp.where(xi < n - 1, k * jnp.exp(-t * dl_ref[...]), 0.0)
    o_ref[...] = k / jnp.sum(jnp.abs(k), axis=-1, keepdims=True)


def _hyfilter(fw, seq):
    rows = HY_ORDER * HY_WIDTH
    rb = 256
    n = 2 * seq
    small = lambda shape: pl.BlockSpec(shape, lambda i: (0, 0))
    return pl.pallas_call(
        functools.partial(_hyfilter_body, seq=seq),
        grid=(rows // rb,),
        in_specs=[
            small((HY_FFN, 1)), small((HY_FFN, HY_BANDS)), small((HY_FFN, HY_BANDS)), small((HY_FFN, 1)),
            small((HY_FFN, HY_FFN)), small((HY_FFN, 1)), small((HY_FFN, 1)),
            pl.BlockSpec((rb, HY_FFN), lambda i: (i, 0)),
            pl.BlockSpec((rb, HY_FFN), lambda i: (i, 0)),
            small((HY_BANDS, 1)),
            pl.BlockSpec((rb, 1), lambda i: (i, 0)),
        ],
        out_specs=pl.BlockSpec((rb, n), lambda i: (i, 0)),
        out_shape=jax.ShapeDtypeStruct((rows, n), F32),
        compiler_params=_params("parallel"),
        name="hyfilter",
    )(*fw)


def _hyconv_body(sc_ref, z_ref, x1_ref, x2_ref, hml_ref, hmc_ref, o_ref, big_l, big_c, *, seq, ctx, cb):
    tot = seq + ctx
    lane = lax.broadcasted_iota(jnp.int32, (1, tot), 1)
    has_prev = jnp.logical_and(lane != 0, lane != seq)
    has_next = jnp.logical_and(lane != seq - 1, lane != tot - 1)
    row = lax.broadcasted_iota(jnp.int32, (LANES, LANES), 0)
    col = lax.broadcasted_iota(jnp.int32, (LANES, LANES), 1)
    lower = col <= row
    c0 = pl.program_id(0) * cb

    def short_conv(p, base, part):
        prev = jnp.where(has_prev, pltpu.roll(p, 1, 1), 0.0)
        nxt = jnp.where(has_next, pltpu.roll(p, tot - 1, 1), 0.0)
        return (sc_ref[base + 9 + part] + sc_ref[base + part] * prev
                + sc_ref[base + 3 + part] * p + sc_ref[base + 6 + part] * nxt)

    def build(hm_ref, o, ci, big_ref):
        n_rows = hm_ref.shape[2]
        prev = None
        for rp in range(n_rows - 1, -1, -1):
            r = hm_ref[o, ci, rp:rp + 1, :]
            cur = pltpu.roll(jnp.broadcast_to(r, (LANES, LANES)), 1, 1, stride=1, stride_axis=0)
            if prev is not None:
                rho = n_rows - 2 - rp
                big_ref[o, rho * LANES:(rho + 1) * LANES, :] = jnp.where(lower, cur, prev).astype(BF16)
            prev = cur

    def long_conv(zz, big_ref, o, length, base):
        nblk = length // CONV_TILE
        mid = (2 * nblk - 1) * LANES
        ys = [None] * nblk
        for dd in range(-(nblk - 1), nblk):
            r0 = mid - CONV_TILE * dd
            w = jnp.concatenate([big_ref[o, r0:r0 + CONV_TILE, :],
                                 big_ref[o, r0 - LANES:r0 - LANES + CONV_TILE, :]], axis=1)
            js = list(range(max(0, -dd), min(nblk, nblk - dd)))
            lhs = jnp.concatenate([zz[:, base + j * CONV_TILE:base + (j + 1) * CONV_TILE] for j in js], axis=0)
            out = _dot(lhs.astype(BF16), w)
            nb = zz.shape[0]
            for kk, j in enumerate(js):
                piece = out[kk * nb:(kk + 1) * nb]
                ys[j + dd] = piece if ys[j + dd] is None else ys[j + dd] + piece
        return ys

    def chan(ci, carry):
        base = (c0 + ci) * 16
        z = short_conv(z_ref[ci], base, 0)
        gates = (short_conv(x1_ref[ci], base, 1), short_conv(x2_ref[ci], base, 2))
        for o in range(HY_ORDER):
            build(hml_ref, o, ci, big_l)
            build(hmc_ref, o, ci, big_c)
        for o in range(HY_ORDER):
            y = jnp.concatenate(long_conv(z, big_l, o, seq, 0) + long_conv(z, big_c, o, ctx, seq), axis=1)
            z = gates[o] * (y + sc_ref[base + 12 + o] * z)
        o_ref[ci] = z
        return carry

    lax.fori_loop(0, cb, chan, 0, unroll=8)


def _hyconv(sc_tab, u3, hm_l, hm_c, seq, ctx):
    _, b, tot = u3.shape
    cb = HY_CHANNELS_PER_STEP
    nblk = HY_WIDTH // cb
    slab = lambda part: pl.BlockSpec((cb, b, tot), lambda i: (part * nblk + i, 0, 0))
    return pl.pallas_call(
        functools.partial(_hyconv_body, seq=seq, ctx=ctx, cb=cb),
        grid=(nblk,),
        in_specs=[
            pl.BlockSpec(memory_space=pltpu.SMEM),
            slab(0), slab(1), slab(2),
            pl.BlockSpec((HY_ORDER, cb, hm_l.shape[2], LANES), lambda i: (0, i, 0, 0)),
            pl.BlockSpec((HY_ORDER, cb, hm_c.shape[2], LANES), lambda i: (0, i, 0, 0)),
        ],
        out_specs=pl.BlockSpec((cb, b, tot), lambda i: (i, 0, 0)),
        out_shape=jax.ShapeDtypeStruct((HY_WIDTH, b, tot), F32),
        scratch_shapes=[
            pltpu.VMEM((HY_ORDER, (hm_l.shape[2] - 1) * LANES, LANES), BF16),
            pltpu.VMEM((HY_ORDER, (hm_c.shape[2] - 1) * LANES, LANES), BF16),
        ],
        compiler_params=_params("parallel"),
        name="hyconv",
    )(sc_tab, u3, u3, u3, hm_l, hm_c)


def _zero_after(x):
    bits = lax.bitcast_convert_type(x[-8:, -LANES:], jnp.uint32)
    z = lax.shift_right_logical(lax.shift_right_logical(bits, jnp.uint32(16)), jnp.uint32(16))
    return z[0:1].astype(F32).astype(BF16)


def _attn_body(lam_ref, q_ref, k_ref, v_ref, sub_ref, o_ref, *, lam_init, heads):
    hw = 2 * DA_HEAD_DIM
    lane = lax.broadcasted_iota(jnp.int32, (1, hw), 1)
    first = lane < DA_HEAD_DIM
    lp = lam_ref[...]
    lam = (jnp.exp(jnp.sum(lp[0:1] * lp[1:2], keepdims=True))
           - jnp.exp(jnp.sum(lp[2:3] * lp[3:4], keepdims=True)) + lam_init)
    sub = sub_ref[...] * (1.0 - lam_init)

    def scores(h, after):
        q = q_ref[0, :, h * hw:(h + 1) * hw]
        if after is not None:
            q = q + after
        k = k_ref[0, :, h * hw:(h + 1) * hw]
        zero = jnp.zeros_like(q)
        return _dot_nt(jnp.where(first, q, zero), k), _dot_nt(jnp.where(first, zero, q), k)

    def weights(s1, s2):
        p1 = jnp.exp2(s1 - jnp.max(s1, axis=-1, keepdims=True))
        p2 = jnp.exp2(s2 - jnp.max(s2, axis=-1, keepdims=True))
        l1 = jnp.sum(p1, axis=-1, keepdims=True)
        l2 = jnp.sum(p2, axis=-1, keepdims=True)
        return (p1 - p2 * (lam * l1 / l2)).astype(BF16), 1.0 / l1

    def values(h, w):
        o = _dot(w[0], v_ref[0, :, h * hw:(h + 1) * hw]) * w[1]
        on = o * lax.rsqrt(jnp.mean(o * o, axis=-1, keepdims=True) + EPS) * sub
        o_ref[0, :, h * hw:(h + 1) * hw] = on.astype(BF16)

    s = [None] * heads
    w = [None] * heads
    after = None
    for step in range(heads + 2):
        if step < heads:
            s[step] = scores(step, after)
            after = _zero_after(s[step][1])
        if 0 <= step - 1 < heads:
            w[step - 1] = weights(*s[step - 1])
            s[step - 1] = None
        if 0 <= step - 2 < heads:
            values(step - 2, w[step - 2])


def _attention(lam_p, q, k, v, sub, lam_init, seq):
    b, tot, _ = q.shape
    ctx = tot - seq
    tq = TOK_TILE
    hw = 2 * DA_HEAD_DIM
    width = DA_HEADS * hw

    def call(q0, n_q, keys, k0, name):
        return pl.pallas_call(
            functools.partial(_attn_body, lam_init=lam_init, heads=DA_HEADS),
            grid=(b, n_q),
            in_specs=[
                pl.BlockSpec((4, DA_HEAD_DIM), lambda bi, j: (0, 0)),
                pl.BlockSpec((1, tq, width), lambda bi, j: (bi, q0 + j, 0)),
                pl.BlockSpec((1, keys, width), lambda bi, j: (bi, k0, 0)),
                pl.BlockSpec((1, keys, width), lambda bi, j: (bi, k0, 0)),
                pl.BlockSpec((1, hw), lambda bi, j: (0, 0)),
            ],
            out_specs=pl.BlockSpec((1, tq, width), lambda bi, j: (bi, j, 0)),
            out_shape=jax.ShapeDtypeStruct((b, n_q * tq, width), BF16),
            compiler_params=_params("parallel", "parallel"),
            name=name,
        )(lam_p, q, k, v, sub)

    return call(0, seq // tq, tot, 0, "diffattn"), call(seq // tq, ctx // tq, ctx, seq // ctx, "diffattn_ctx")


def _merge_body(x_ref, mod_ref, g1_ref, wg0_ref, wg1_ref, wg2_ref, yh_ref, ydl_ref, ydc_ref, ys_ref, wb_ref, wo_ref,
                o_ref, *, n_lat_tiles):
    x = x_ref[0]
    h = _norm_mod(x, g1_ref[...], mod_ref[0, 0:1, :], mod_ref[0, 1:2, :]).astype(BF16)
    y_da = jnp.where(pl.program_id(1) < n_lat_tiles, ydl_ref[0], ydc_ref[0])
    acc = None
    for n, (wg_ref, y) in enumerate(((wg0_ref, yh_ref[0]), (wg1_ref, y_da), (wg2_ref, ys_ref[0]))):
        gate = jax.nn.sigmoid(_dot(h, wg_ref[0]))
        term = gate * _dot(y, wb_ref[0, n])
        acc = term if acc is None else acc + term
    o_ref[0] = x + mod_ref[0, 2:3, :] * _dot(acc.astype(BF16), wo_ref[0])


def _merge(xa, mod_i, g1, w_in, layer, y_hy, y_da, y_sg, w_br, w_out, rows, n_lat_tiles, n_samples):
    b, _, d = xa.shape
    t = rows
    tm = TOK_TILE
    assert PRE_COLS % d == 0
    gate_spec = lambda n: pl.BlockSpec((1, d, d), lambda bi, j: (layer, 0, PRE_COLS // d + n))
    y_lat, y_ctx = y_da
    n_ctx_tiles = y_ctx.shape[1] // tm
    return pl.pallas_call(
        functools.partial(_merge_body, n_lat_tiles=n_lat_tiles),
        grid=(b, t // tm),
        in_specs=[
            _tok_spec(d, tm), _mod_spec(d, n_lat_tiles, n_samples), _const_spec((1, d)),
            gate_spec(0), gate_spec(1), gate_spec(2),
            _tok_spec(HY_WIDTH, tm),
            pl.BlockSpec((1, tm, HY_WIDTH), lambda bi, j: (bi, jnp.minimum(j, n_lat_tiles - 1), 0)),
            pl.BlockSpec((1, tm, HY_WIDTH), lambda bi, j: (bi, jnp.clip(j - n_lat_tiles, 0, n_ctx_tiles - 1), 0)),
            _tok_spec(HY_WIDTH, tm),
            pl.BlockSpec((1, 3, HY_WIDTH, d), lambda bi, j: (layer, 0, 0, 0)),
            pl.BlockSpec((1, d, d), lambda bi, j: (layer, 0, 0)),
        ],
        out_specs=_tok_spec(d, tm),
        out_shape=jax.ShapeDtypeStruct((b, t, d), F32),
        compiler_params=_params("parallel", "parallel"),
        name="merge",
    )(xa, mod_i, g1, w_in, w_in, w_in, y_hy, y_lat, y_ctx, y_sg, w_br, w_out)


def _ffn_body(x_ref, mod_ref, g2_ref, wg_ref, wu_ref, wd_ref, o_ref, *, chunk):
    x = x_ref[0]
    h = _norm_mod(x, g2_ref[...], mod_ref[0, 3:4, :], mod_ref[0, 4:5, :]).astype(BF16)
    ff = wg_ref.shape[2]
    acc = None
    for f0 in range(0, ff, chunk):
        f1 = min(ff, f0 + chunk)
        a = _dot(h, wg_ref[0, :, f0:f1])
        mid = (a * jax.nn.sigmoid(a) * _dot(h, wu_ref[0, :, f0:f1])).astype(BF16)
        term = _dot(mid, wd_ref[0, f0:f1, :])
        acc = term if acc is None else acc + term
    o_ref[0] = x + mod_ref[0, 5:6, :] * acc


def _ffn_dense(xa, mod_i, g2, wg, wu, wd, layer, n_lat_tiles, n_samples):
    b, t, d = xa.shape
    ff = wg.shape[2]
    tm = TOK_TILE
    return pl.pallas_call(
        functools.partial(_ffn_body, chunk=1024),
        grid=(b, t // tm),
        in_specs=[
            _tok_spec(d, tm), _mod_spec(d, n_lat_tiles, n_samples), _const_spec((1, d)),
            pl.BlockSpec((1, d, ff), lambda bi, j: (layer, 0, 0)),
            pl.BlockSpec((1, d, ff), lambda bi, j: (layer, 0, 0)),
            pl.BlockSpec((1, ff, d), lambda bi, j: (layer, 0, 0)),
        ],
        out_specs=_tok_spec(d, tm),
        out_shape=jax.ShapeDtypeStruct((b, t, d), F32),
        compiler_params=_params("parallel", "parallel"),
        name="ffn_dense",
    )(xa, mod_i, g2, wg, wu, wd)


def _router_body(x_ref, mod_ref, g2_ref, wr_ref, h_ref, r_ref):
    h = _norm_mod(x_ref[0], g2_ref[...], mod_ref[0, 3:4, :], mod_ref[0, 4:5, :])
    h_ref[0] = h
    h_hi, h_lo = _split_bf16(h)
    w_hi, w_lo = _split_bf16(wr_ref[...])
    logits = _dot_nt(w_hi, h_hi) + _dot_nt(w_hi, h_lo) + _dot_nt(w_lo, h_hi)
    eid = lax.broadcasted_iota(jnp.int32, logits.shape, 0)
    m1 = jnp.max(logits, axis=0, keepdims=True)
    i1 = jnp.min(jnp.where(logits == m1, eid, N_EXPERTS), axis=0, keepdims=True)
    rest = jnp.where(eid == i1, -jnp.inf, logits)
    m2 = jnp.max(rest, axis=0, keepdims=True)
    i2 = jnp.min(jnp.where(rest == m2, eid, N_EXPERTS), axis=0, keepdims=True)
    w1 = 1.0 / (1.0 + jnp.exp(m2 - m1))
    rows = lax.broadcasted_iota(jnp.int32, logits.shape, 0)
    out = jnp.where(rows == 0, i1.astype(F32), jnp.where(rows == 1, i2.astype(F32),
                    jnp.where(rows == 2, w1, jnp.where(rows == 3, 1.0 - w1, 0.0))))
    r_ref[0] = out


def _router(xa, mod_i, g2, wr_t, n_lat_tiles, n_samples):
    b, t, d = xa.shape
    tm = TOK_TILE
    return pl.pallas_call(
        _router_body,
        grid=(b, t // tm),
        in_specs=[
            _tok_spec(d, tm), _mod_spec(d, n_lat_tiles, n_samples), _const_spec((1, d)),
            _const_spec((N_EXPERTS, d)),
        ],
        out_specs=[_tok_spec(d, tm), pl.BlockSpec((1, N_EXPERTS, tm), lambda bi, j: (bi, 0, j))],
        out_shape=[jax.ShapeDtypeStruct((b, t, d), F32), jax.ShapeDtypeStruct((b, N_EXPERTS, t), F32)],
        compiler_params=_params("parallel", "parallel"),
        name="router",
    )(xa, mod_i, g2, wr_t)


def _row_copy(src_hbm, dst, sem, src_row, dst_row):
    return pltpu.make_async_copy(src_hbm.at[pl.ds(src_row, 1), :], dst.at[pl.ds(dst_row, 1), :], sem)


def _gather_rows(src_hbm, dst, sem, idx):
    rows = dst.shape[0]

    def start(g, c):
        for u in range(DMA_UNROLL):
            r = g * DMA_UNROLL + u
            _row_copy(src_hbm, dst, sem, idx(r), r).start(priority=u % 2)
        return c
    lax.fori_loop(0, rows // DMA_UNROLL, start, 0)


def _wait_rows(src_hbm, dst, sem):
    pltpu.make_async_copy(src_hbm.at[pl.ds(0, dst.shape[0]), :], dst, sem).wait()


def _expert_body(te_ref, nu_ref, rt_ref, rtn_ref, h_hbm, wg_ref, wu_ref, wd_ref, o_ref, xbuf, xbf, acc, sem, *, chunk):
    t = pl.program_id(0)
    f = pl.program_id(1)
    last_f = pl.num_programs(1) - 1
    n_used = nu_ref[0]
    used = t < n_used
    slot = t % 2

    @pl.when(jnp.logical_and(f == 0, jnp.logical_and(used, t == 0)))
    def _first_gather():
        _gather_rows(h_hbm, xbuf.at[0], sem.at[0], lambda r: rt_ref[0, 0, r])

    @pl.when(jnp.logical_and(used, f == 0))
    def _stage():
        _wait_rows(h_hbm, xbuf.at[slot], sem.at[slot])
        xbf[...] = xbuf[slot].astype(BF16)
        acc[...] = jnp.zeros_like(acc)

    @pl.when(used)
    def _compute():
        xb = xbf[...]
        a = _dot(xb, wg_ref[0, 0].astype(BF16))
        b = _dot(xb, wu_ref[0, 0].astype(BF16))
        for u in range(chunk):
            r = f * chunk + u
            _row_copy(h_hbm, xbuf.at[1 - slot], sem.at[1 - slot], rtn_ref[0, 0, r], r).start(priority=u % 2)
        mid = (a * jax.nn.sigmoid(a) * b).astype(BF16)
        acc[...] += _dot(mid, wd_ref[0, 0].astype(BF16))

    @pl.when(f == last_f)
    def _store():
        o_ref[...] = jnp.where(used, acc[...], 0.0)

    @pl.when(jnp.logical_and(f == last_f, t == n_used - 1))
    def _drain():
        _wait_rows(h_hbm, xbuf.at[1 - slot], sem.at[1 - slot])


def _experts(tile_expert, n_used, row_token, h_flat, wg, wu, wd, layer):
    n_rows = row_token.shape[0]
    d = h_flat.shape[1]
    ff = wg.shape[3]
    nt = n_rows // ROW_TILE
    rt3 = row_token.reshape(nt, 1, ROW_TILE)
    grid_spec = pltpu.PrefetchScalarGridSpec(
        num_scalar_prefetch=2,
        grid=(nt, ff // FF_TILE),
        in_specs=[
            pl.BlockSpec((1, 1, ROW_TILE), lambda t, f, te, nu: (t, 0, 0), memory_space=pltpu.SMEM),
            pl.BlockSpec((1, 1, ROW_TILE), lambda t, f, te, nu: (jnp.minimum(t + 1, nt - 1), 0, 0),
                         memory_space=pltpu.SMEM),
            pl.BlockSpec(memory_space=pl.ANY),
            pl.BlockSpec((1, 1, d, FF_TILE), lambda t, f, te, nu: (layer, te[t], 0, f)),
            pl.BlockSpec((1, 1, d, FF_TILE), lambda t, f, te, nu: (layer, te[t], 0, f)),
            pl.BlockSpec((1, 1, FF_TILE, d), lambda t, f, te, nu: (layer, te[t], f, 0)),
        ],
        out_specs=pl.BlockSpec((ROW_TILE, d), lambda t, f, te, nu: (t, 0)),
        scratch_shapes=[
            pltpu.VMEM((2, ROW_TILE, d), F32),
            pltpu.VMEM((ROW_TILE, d), BF16),
            pltpu.VMEM((ROW_TILE, d), F32),
            pltpu.SemaphoreType.DMA((2,)),
        ],
    )
    assert ROW_TILE % (ff // FF_TILE) == 0
    return pl.pallas_call(
        functools.partial(_expert_body, chunk=ROW_TILE // (ff // FF_TILE)),
        grid_spec=grid_spec,
        out_shape=jax.ShapeDtypeStruct((n_rows, d), F32),
        compiler_params=_params("arbitrary", "arbitrary"),
        name="experts",
    )(tile_expert, n_used, rt3, rt3, h_flat, wg, wu, wd)


def _combine_body(pos_ref, posn_ref, r_ref, x_ref, mod_ref, ye_hbm, o_ref, buf, sem):
    tm = x_ref.shape[1]
    step = pl.program_id(0) * pl.num_programs(1) + pl.program_id(1)
    n_steps = pl.num_programs(0) * pl.num_programs(1)
    slot = step % 2

    def gather(p_ref, s):
        for kk in range(2):
            _gather_rows(ye_hbm, buf.at[s, kk], sem.at[s, kk], lambda r, kk=kk: p_ref[0, kk, r])

    @pl.when(step == 0)
    def _first():
        gather(pos_ref, 0)

    @pl.when(step + 1 < n_steps)
    def _next():
        gather(posn_ref, 1 - slot)

    eye = (lax.broadcasted_iota(jnp.int32, (tm, tm), 0) == lax.broadcasted_iota(jnp.int32, (tm, tm), 1)).astype(BF16)
    r_hi, r_lo = _split_bf16(r_ref[0])
    rcol = _dot_nt(eye, r_hi) + _dot_nt(eye, r_lo)
    for kk in range(2):
        _wait_rows(ye_hbm, buf.at[slot, kk], sem.at[slot, kk])
    mix = buf[slot, 0] * rcol[:, 2:3] + buf[slot, 1] * rcol[:, 3:4]
    o_ref[0] = x_ref[0] + mod_ref[0, 5:6, :] * mix


def _combine(pos, route, xa, mod_i, ye, n_lat_tiles, n_samples):
    b, t, d = xa.shape
    tm = TOK_TILE
    nj = t // tm
    return pl.pallas_call(
        _combine_body,
        grid=(b, nj),
        in_specs=[
            pl.BlockSpec((1, 2, tm), lambda bi, j: (bi * nj + j, 0, 0), memory_space=pltpu.SMEM),
            pl.BlockSpec((1, 2, tm), lambda bi, j: (jnp.minimum(bi * nj + j + 1, b * nj - 1), 0, 0),
                         memory_space=pltpu.SMEM),
            pl.BlockSpec((1, N_EXPERTS, tm), lambda bi, j: (bi, 0, j)),
            _tok_spec(d, tm), _mod_spec(d, n_lat_tiles, n_samples),
            pl.BlockSpec(memory_space=pl.ANY),
        ],
        out_specs=_tok_spec(d, tm),
        out_shape=jax.ShapeDtypeStruct((b, t, d), F32),
        scratch_shapes=[pltpu.VMEM((2, 2, tm, d), F32), pltpu.SemaphoreType.DMA((2, 2))],
        compiler_params=_params("arbitrary", "arbitrary"),
        name="moe_combine",
    )(pos, pos, route, xa, mod_i, ye)


def _moe(xa, mod_i, g2, wr_t, wg, wu, wd, layer, n_lat_tiles, n_samples):
    b, t, d = xa.shape
    h, route = _router(xa, mod_i, g2, wr_t, n_lat_tiles, n_samples)
    n_tok = b * t
    expert = route[:, 0:2, :].astype(jnp.int32)
    onehot = (expert.reshape(-1)[:, None] == jnp.arange(N_EXPERTS)[None, :]).astype(jnp.int32)
    counts = jnp.sum(onehot, axis=0)
    rank = jnp.sum((jnp.cumsum(onehot, axis=0) - onehot) * onehot, axis=1)
    padded = ((counts + ROW_TILE - 1) // ROW_TILE) * ROW_TILE
    ends = jnp.cumsum(padded)
    pos = ((ends - padded)[expert.reshape(-1)] + rank).reshape(b, 2, t)
    n_rows = ((2 * n_tok + N_EXPERTS * (ROW_TILE - 1)) // ROW_TILE + 1) * ROW_TILE
    token = jnp.broadcast_to(jnp.arange(b)[:, None, None] * t + jnp.arange(t)[None, None, :], (b, 2, t))
    row_token = jnp.zeros((n_rows,), jnp.int32).at[pos.reshape(-1)].set(token.reshape(-1))
    tile_start = jnp.arange(n_rows // ROW_TILE) * ROW_TILE
    tile_expert = jnp.minimum(jnp.sum(tile_start[:, None] >= ends[None, :], axis=1), N_EXPERTS - 1).astype(jnp.int32)
    n_used = (ends[-1:] // ROW_TILE).astype(jnp.int32)
    ye = _experts(tile_expert, n_used, row_token, h.reshape(n_tok, d), wg, wu, wd, layer)
    nj = t // TOK_TILE
    pos_tiles = pos.reshape(b, 2, nj, TOK_TILE).transpose(0, 2, 1, 3).reshape(b * nj, 2, TOK_TILE)
    return _combine(pos_tiles, route, xa, mod_i, ye, n_lat_tiles, n_samples)


def _rope_tables(seq, ctx):
    rows = seq // GRID_W
    row = jnp.repeat(jnp.arange(rows), GRID_W).astype(F32)
    col = jnp.tile(jnp.arange(GRID_W), rows).astype(F32)
    half = DA_HEAD_DIM // 2
    inv = ROPE_BASE ** (-jnp.arange(0, half, 2, dtype=F32) / half)
    ang = jnp.concatenate([row[:, None] * inv, row[:, None] * inv, col[:, None] * inv, col[:, None] * inv], axis=1)
    sign = jnp.tile(jnp.concatenate([-jnp.ones((half // 2,), F32), jnp.ones((half // 2,), F32)]), 2)
    cos = jnp.concatenate([jnp.cos(ang), jnp.ones((ctx, DA_HEAD_DIM), F32)], axis=0)
    sin = jnp.concatenate([jnp.sin(ang) * sign, jnp.zeros((ctx, DA_HEAD_DIM), F32)], axis=0)
    return jnp.tile(cos, (1, LANES // DA_HEAD_DIM)), jnp.tile(sin, (1, LANES // DA_HEAD_DIM))


def _filter_weights(w1, b1, w2, b2, w3, freq):
    col = lambda v: v.reshape(-1, 1)
    bands = jnp.linspace(1e-4, HY_BANDS - 1, HY_BANDS, dtype=F32)
    deltas = jnp.abs(jnp.linspace(HY_MIN_DECAY, HY_MAX_DECAY, HY_WIDTH, dtype=F32))
    half = HY_ORDER * HY_WIDTH
    return (w1[0:1].T, w1[1:1 + HY_BANDS].T, w1[1 + HY_BANDS:].T, col(b1), w2.T, col(b2), col(freq),
            w3[:, :half].T, w3[:, half:].T, col(bands), col(jnp.tile(deltas, HY_ORDER)))


def kernel(x, c, ctx, c_ctx, w_mod, b_mod, norm1_g, norm2_g, w_in, hy_conv_w, hy_conv_b, hy_f_w1, hy_f_b1, hy_f_w2, hy_f_b2, hy_f_w3, hy_f_freq, hy_skip, q_norm_g, k_norm_g, lam_q1, lam_k1, lam_q2, lam_k2, subln_g, sg_norm_g, sg_w, sg_b, w_branch, w_out, ffn_wg, ffn_wu, ffn_wd, router_w, moe_wg, moe_wu, moe_wd):
    n_samples, seq, d = x.shape
    n_ctx = ctx.shape[1]
    depth = w_mod.shape[0]
    assert seq % CONV_TILE == 0 and n_ctx % CONV_TILE == 0 and seq % GRID_W == 0 and seq % n_ctx == 0
    assert n_samples + 1 <= MOD_ROWS
    n_lat_tiles = seq // TOK_TILE
    tot = seq + n_ctx

    xa = jnp.concatenate([x, ctx], axis=1)
    cond = jnp.zeros((MOD_ROWS, d), F32).at[:n_samples].set(c).at[n_samples].set(c_ctx)
    mod = _modvec(cond, w_mod, b_mod).reshape(depth, MOD_ROWS, 6, d)

    cos_t, sin_t = _rope_tables(seq, n_ctx)
    group = jnp.arange(QK_COLS) // DA_HEAD_DIM
    gmat = (group[:, None] == group[None, :]).astype(BF16)
    w_in_b = w_in.astype(BF16)
    w_br_b = w_branch.astype(BF16)
    w_out_b = w_out.astype(BF16)
    ffn_b = [w.astype(BF16) for w in (ffn_wg, ffn_wu, ffn_wd)]

    for i in range(depth):
        lam_init = 0.8 - 0.6 * math.exp(-0.3 * i)
        g1 = norm1_g[i].reshape(1, d)
        g2 = norm2_g[i].reshape(1, d)
        hy, q, k, v, y_sg = _inproj(
            xa, mod[i], g1, w_in_b, i, cos_t, sin_t,
            jnp.tile(q_norm_g[i], QK_COLS // DA_HEAD_DIM).reshape(1, QK_COLS),
            jnp.tile(k_norm_g[i], QK_COLS // DA_HEAD_DIM).reshape(1, QK_COLS),
            gmat, sg_norm_g[i].reshape(1, SG_WIDTH), sg_w[i].astype(BF16),
            jnp.repeat(sg_b[i].T, SG_WIDTH // SG_GROUPS, axis=1), n_lat_tiles, n_samples)

        fw = _filter_weights(hy_f_w1[i], hy_f_b1[i], hy_f_w2[i], hy_f_b2[i], hy_f_w3[i], hy_f_freq[i])
        hm_l = _hyfilter(fw, seq).reshape(HY_ORDER, HY_WIDTH, 2 * seq // LANES, LANES)
        hm_c = _hyfilter(fw, n_ctx).reshape(HY_ORDER, HY_WIDTH, 2 * n_ctx // LANES, LANES)
        cw = hy_conv_w[i].reshape(3, 3, HY_WIDTH)
        cbias = hy_conv_b[i].reshape(3, HY_WIDTH)
        sc_tab = jnp.concatenate([cw.reshape(9, HY_WIDTH), cbias, hy_skip[i], jnp.zeros((2, HY_WIDTH), F32)],
                                 axis=0).T.reshape(-1)
        y_hy = _hyconv(sc_tab, jnp.transpose(hy, (2, 0, 1)), hm_l, hm_c, seq, n_ctx)
        y_hy = jnp.transpose(y_hy, (1, 2, 0)).astype(BF16)

        lam_p = jnp.stack([lam_q1[i], lam_k1[i], lam_q2[i], lam_k2[i]], axis=0)
        y_da = _attention(lam_p, q, k, v, subln_g[i].reshape(1, DA_V_DIM), lam_init, seq)

        xa = _merge(xa, mod[i], g1, w_in_b, i, y_hy, y_da, y_sg, w_br_b, w_out_b,
                    seq if i == depth - 1 else tot, n_lat_tiles, n_samples)

        j = i // 2
        if i % 2 == 0:
            xa = _ffn_dense(xa, mod[i], g2, *ffn_b, j, n_lat_tiles, n_samples)
        else:
            xa = _moe(xa, mod[i], g2, router_w[j].T, moe_wg, moe_wu, moe_wd, j, n_lat_tiles, n_samples)
    return xa[:, :seq]
```

```python
import functools
import math

import jax
import jax.numpy as jnp
from jax import lax
from jax.experimental import pallas as pl
from jax.experimental.pallas import tpu as pltpu

F32 = jnp.float32
BF16 = jnp.bfloat16
EPS = 1e-6

GRID_W = 64
HY_WIDTH = 512
HY_ORDER = 2
HY_BANDS = 16
HY_FFN = 64
HY_MAX_DECAY = math.log(1e-2) / 0.3
HY_MIN_DECAY = math.log(1e-2) / 1.5
DA_HEADS = 4
DA_HEAD_DIM = 64
DA_V_DIM = 2 * DA_HEAD_DIM
ROPE_BASE = 10000.0
SG_WIDTH = 512
SG_GROUPS = 4
SG_CHUNK = 128
N_EXPERTS = 8
HY_COLS = 3 * HY_WIDTH
QK_COLS = DA_HEADS * 2 * DA_HEAD_DIM
DA_COLS = 2 * QK_COLS + DA_HEADS * DA_V_DIM
SG_COLS = 2 * SG_WIDTH
PRE_COLS = HY_COLS + DA_COLS + SG_COLS

LANES = 128
TOK_TILE = 256
CONV_TILE = 256
HY_CHANNELS_PER_STEP = 16
HY_SCALARS = 16
ROW_TILE = 896
FF_TILE = 512
DMA_UNROLL = 8
MOD_ROWS = 16
VMEM_LIMIT = 56 * 1024 * 1024


def _dot(a, b):
    return jnp.dot(a, b, preferred_element_type=F32)


def _dot_nt(a, b):
    return lax.dot_general(a, b, (((1,), (1,)), ((), ())), preferred_element_type=F32)


_hdot = functools.partial(jnp.dot, precision=lax.Precision.HIGHEST, preferred_element_type=F32)


def _split_bf16(a):
    hi = a.astype(BF16)
    return hi, (a - hi.astype(F32)).astype(BF16)


def _dot3(a, b):
    a_hi, a_lo = _split_bf16(a)
    b_hi, b_lo = _split_bf16(b)
    return _dot(a_hi, b_hi) + _dot(a_lo, b_hi) + _dot(a_hi, b_lo)


def _norm_mod(x, g, shift, scale):
    ms = jnp.mean(x * x, axis=-1, keepdims=True)
    return (x * lax.rsqrt(ms + EPS) * g) * (1.0 + scale) + shift


def _params(*sem):
    return pltpu.CompilerParams(dimension_semantics=sem, vmem_limit_bytes=VMEM_LIMIT)


def _modvec_body(cond_ref, w_ref, b_ref, o_ref):
    cnd = cond_ref[...]
    s = cnd * jax.nn.sigmoid(cnd)
    s_hi, s_lo = _split_bf16(s)
    w_hi, w_lo = _split_bf16(w_ref[0])
    o_ref[0] = _dot(s_hi, w_hi) + _dot(s_lo, w_hi) + _dot(s_hi, w_lo) + b_ref[0]


def _modvec(cond, w_mod, b_mod):
    depth, d, n = w_mod.shape
    tn = n // 4
    return pl.pallas_call(
        _modvec_body,
        grid=(depth, n // tn),
        in_specs=[
            pl.BlockSpec((MOD_ROWS, d), lambda i, j: (0, 0)),
            pl.BlockSpec((1, d, tn), lambda i, j: (i, 0, j)),
            pl.BlockSpec((1, 1, tn), lambda i, j: (i, 0, j)),
        ],
        out_specs=pl.BlockSpec((1, MOD_ROWS, tn), lambda i, j: (i, 0, j)),
        out_shape=jax.ShapeDtypeStruct((depth, MOD_ROWS, n), F32),
        compiler_params=_params("parallel", "parallel"),
        name="modvec",
    )(cond, w_mod, b_mod.reshape(depth, 1, n))


def _tok_spec(width, tm):
    return pl.BlockSpec((1, tm, width), lambda b, j: (b, j, 0))


def _const_spec(shape):
    nd = len(shape)
    return pl.BlockSpec(shape, lambda b, j: (0,) * nd)


def _mod_spec(d, n_lat_tiles, n_samples):
    return pl.BlockSpec((1, 6, d), lambda b, j: (jnp.where(j < n_lat_tiles, b, n_samples), 0, 0))


def _qk_norm_rope(a, g_tile, gmat, cos, sin_signed, first_half):
    ss = _dot((a * a).astype(BF16), gmat)
    an = a * lax.rsqrt(ss * (1.0 / DA_HEAD_DIM) + EPS) * g_tile
    outs = []
    for ci in range(QK_COLS // LANES):
        ch = an[:, ci * LANES:(ci + 1) * LANES]
        partner = jnp.where(first_half, pltpu.roll(ch, LANES - 16, 1), pltpu.roll(ch, 16, 1))
        outs.append(ch * cos + partner * sin_signed)
    return jnp.concatenate(outs, axis=1)


def _inproj_body(x_ref, mod_ref, g1_ref, w_ref, cos_ref, sin_ref, qg_ref, kg_ref, gmat_ref,
                 sgg_ref, sgw_ref, sgb_ref, hy_ref, q_ref, k_ref, v_ref, sg_ref):
    tm = x_ref.shape[1]
    h = _norm_mod(x_ref[0], g1_ref[...], mod_ref[0, 0:1, :], mod_ref[0, 1:2, :]).astype(BF16)
    hy_ref[0] = _dot(h, w_ref[0, :, 0:HY_COLS])

    lane = lax.broadcasted_iota(jnp.int32, (1, LANES), 1)
    first_half = (lane % 32) < 16
    cos = cos_ref[...]
    sin_signed = sin_ref[...]
    gmat = gmat_ref[...]
    o = HY_COLS
    q = _qk_norm_rope(_dot(h, w_ref[0, :, o:o + QK_COLS]), qg_ref[...], gmat, cos, sin_signed, first_half)
    q_ref[0] = (q * (DA_HEAD_DIM ** -0.5 * math.log2(math.e))).astype(BF16)
    o += QK_COLS
    k = _qk_norm_rope(_dot(h, w_ref[0, :, o:o + QK_COLS]), kg_ref[...], gmat, cos, sin_signed, first_half)
    k_ref[0] = k.astype(BF16)
    o += QK_COLS
    v_ref[0] = _dot(h, w_ref[0, :, o:o + DA_HEADS * DA_V_DIM]).astype(BF16)
    o += DA_HEADS * DA_V_DIM

    z = jax.nn.gelu(_dot(h, w_ref[0, :, o:o + SG_COLS]))
    u = z[:, :SG_WIDTH]
    vv = z[:, SG_WIDTH:]
    vn = (vv * lax.rsqrt(jnp.mean(vv * vv, axis=-1, keepdims=True) + EPS) * sgg_ref[...]).astype(BF16)
    gw = SG_WIDTH // SG_GROUPS
    for ch in range(tm // SG_CHUNK):
        r0 = ch * SG_CHUNK
        cols = []
        for g in range(SG_GROUPS):
            s = _dot(sgw_ref[g], vn[r0:r0 + SG_CHUNK, g * gw:(g + 1) * gw]) + sgb_ref[:, g * gw:(g + 1) * gw]
            cols.append(u[r0:r0 + SG_CHUNK, g * gw:(g + 1) * gw] * s)
        sg_ref[0, r0:r0 + SG_CHUNK, :] = jnp.concatenate(cols, axis=1).astype(BF16)


def _inproj(xa, mod_i, g1, w_in, layer, cos_t, sin_t, qg, kg, gmat, sgg, sgw, sgb, n_lat_tiles, n_samples):
    b, t, d = xa.shape
    tm = TOK_TILE
    outs = [jax.ShapeDtypeStruct((b, t, HY_COLS), F32)] + [jax.ShapeDtypeStruct((b, t, QK_COLS), BF16)] * 4
    return pl.pallas_call(
        _inproj_body,
        grid=(b, t // tm),
        in_specs=[
            _tok_spec(d, tm),
            _mod_spec(d, n_lat_tiles, n_samples),
            _const_spec((1, d)),
            pl.BlockSpec((1, d, PRE_COLS), lambda bi, j: (layer, 0, 0)),
            pl.BlockSpec((tm, LANES), lambda bi, j: (j, 0)),
            pl.BlockSpec((tm, LANES), lambda bi, j: (j, 0)),
            _const_spec((1, QK_COLS)),
            _const_spec((1, QK_COLS)),
            _const_spec((QK_COLS, QK_COLS)),
            _const_spec((1, SG_WIDTH)),
            _const_spec((SG_GROUPS, SG_CHUNK, SG_CHUNK)),
            _const_spec((SG_CHUNK, SG_WIDTH)),
        ],
        out_specs=[_tok_spec(HY_COLS, tm)] + [_tok_spec(QK_COLS, tm)] * 4,
        out_shape=outs,
        compiler_params=_params("parallel", "parallel"),
        name="inproj",
    )(xa, mod_i, g1, w_in, cos_t, sin_t, qg, kg, gmat, sgg, sgw, sgb)


def _hyfilter_body(w1t_ref, w1c_ref, w1s_ref, b1_ref, w2_ref, b2_ref, fr_ref, w3f_ref, w3b_ref,
                   bands_ref, dl_ref, o_ref, *, seq):
    n = 2 * seq
    xi = lax.broadcasted_iota(jnp.int32, (1, n), 1)
    lag = xi - (seq - 1)
    pos = jnp.abs(lag).astype(F32)
    t = pos / (seq - 1)
    ang = 2.0 * math.pi * pos * bands_ref[...] / seq
    fr = fr_ref[...]
    z1 = w1t_ref[...] * t + _hdot(w1c_ref[...], jnp.cos(ang)) + _hdot(w1s_ref[...], jnp.sin(ang)) + b1_ref[...]
    h1 = jnp.sin(fr * z1)
    h2 = jnp.sin(fr * (_hdot(w2_ref[...], h1) + b2_ref[...]))
    bwd = _dot3(w3b_ref[...], h2[:, :seq])
    fwd = _dot3(w3f_ref[...], h2[:, seq:])
    fwd0 = _dot3(w3f_ref[...], h2[:, seq - LANES:seq])
    at_zero = jnp.where(lag[:, seq - LANES:seq] == 0, fwd0, 0.0)
    k = jnp.concatenate([bwd[:, :seq - LANES], bwd[:, seq - LANES:] + at_zero, fwd], axis=1)
    k = jnp.where(xi < n - 1, k * jnp.exp(-t * dl_ref[...]), 0.0)
    o_ref[...] = k / jnp.sum(jnp.abs(k), axis=-1, keepdims=True)


def _hyfilter(fw, seq):
    rows = HY_ORDER * HY_WIDTH
    rb = 256
    n = 2 * seq
    small = lambda shape: pl.BlockSpec(shape, lambda i: (0, 0))
    return pl.pallas_call(
        functools.partial(_hyfilter_body, seq=seq),
        grid=(rows // rb,),
        in_specs=[
            small((HY_FFN, 1)), small((HY_FFN, HY_BANDS)), small((HY_FFN, HY_BANDS)), small((HY_FFN, 1)),
            small((HY_FFN, HY_FFN)), small((HY_FFN, 1)), small((HY_FFN, 1)),
            pl.BlockSpec((rb, HY_FFN), lambda i: (i, 0)),
            pl.BlockSpec((rb, HY_FFN), lambda i: (i, 0)),
            small((HY_BANDS, 1)),
            pl.BlockSpec((rb, 1), lambda i: (i, 0)),
        ],
        out_specs=pl.BlockSpec((rb, n), lambda i: (i, 0)),
        out_shape=jax.ShapeDtypeStruct((rows, n), F32),
        compiler_params=_params("parallel"),
        name="hyfilter",
    )(*fw)


def _hyconv_body(sc_ref, z_ref, x1_ref, x2_ref, hml_ref, hmc_ref, o_ref, big_l, big_c, *, seq, ctx, cb):
    tot = seq + ctx
    lane = lax.broadcasted_iota(jnp.int32, (1, tot), 1)
    has_prev = jnp.logical_and(lane != 0, lane != seq)
    has_next = jnp.logical_and(lane != seq - 1, lane != tot - 1)
    row = lax.broadcasted_iota(jnp.int32, (LANES, LANES), 0)
    col = lax.broadcasted_iota(jnp.int32, (LANES, LANES), 1)
    lower = col <= row
    c0 = pl.program_id(0) * cb

    def short_conv(p, base, part):
        prev = jnp.where(has_prev, pltpu.roll(p, 1, 1), 0.0)
        nxt = jnp.where(has_next, pltpu.roll(p, tot - 1, 1), 0.0)
        return (sc_ref[base + 9 + part] + sc_ref[base + part] * prev
                + sc_ref[base + 3 + part] * p + sc_ref[base + 6 + part] * nxt)

    def build(hm_ref, o, ci, big_ref):
        n_rows = hm_ref.shape[2]
        prev = None
        for rp in range(n_rows - 1, -1, -1):
            r = hm_ref[o, ci, rp:rp + 1, :]
            cur = pltpu.roll(jnp.broadcast_to(r, (LANES, LANES)), 1, 1, stride=1, stride_axis=0)
            if prev is not None:
                rho = n_rows - 2 - rp
                big_ref[o, rho * LANES:(rho + 1) * LANES, :] = jnp.where(lower, cur, prev).astype(BF16)
            prev = cur

    def long_conv(zz, big_ref, o, length, base):
        nblk = length // CONV_TILE
        mid = (2 * nblk - 1) * LANES
        ys = [None] * nblk
        for dd in range(-(nblk - 1), nblk):
            r0 = mid - CONV_TILE * dd
            w = jnp.concatenate([big_ref[o, r0:r0 + CONV_TILE, :],
                                 big_ref[o, r0 - LANES:r0 - LANES + CONV_TILE, :]], axis=1)
            js = list(range(max(0, -dd), min(nblk, nblk - dd)))
            lhs = jnp.concatenate([zz[:, base + j * CONV_TILE:base + (j + 1) * CONV_TILE] for j in js], axis=0)
            out = _dot(lhs.astype(BF16), w)
            nb = zz.shape[0]
            for kk, j in enumerate(js):
                piece = out[kk * nb:(kk + 1) * nb]
                ys[j + dd] = piece if ys[j + dd] is None else ys[j + dd] + piece
        return ys

    def chan(ci, carry):
        base = (c0 + ci) * HY_SCALARS
        z = short_conv(z_ref[ci], base, 0)
        gates = (short_conv(x1_ref[ci], base, 1), short_conv(x2_ref[ci], base, 2))
        for o in range(HY_ORDER):
            build(hml_ref, o, ci, big_l)
            build(hmc_ref, o, ci, big_c)
        for o in range(HY_ORDER):
            y = jnp.concatenate(long_conv(z, big_l, o, seq, 0) + long_conv(z, big_c, o, ctx, seq), axis=1)
            z = gates[o] * (y + sc_ref[base + 12 + o] * z)
        o_ref[ci] = z
        return carry

    lax.fori_loop(0, cb, chan, 0, unroll=8)


def _hyconv(sc_tab, u3, hm_l, hm_c, seq, ctx):
    _, b, tot = u3.shape
    cb = HY_CHANNELS_PER_STEP
    nblk = HY_WIDTH // cb
    slab = lambda part: pl.BlockSpec((cb, b, tot), lambda i: (part * nblk + i, 0, 0))
    return pl.pallas_call(
        functools.partial(_hyconv_body, seq=seq, ctx=ctx, cb=cb),
        grid=(nblk,),
        in_specs=[
            pl.BlockSpec(memory_space=pltpu.SMEM),
            slab(0), slab(1), slab(2),
            pl.BlockSpec((HY_ORDER, cb, hm_l.shape[2], LANES), lambda i: (0, i, 0, 0)),
            pl.BlockSpec((HY_ORDER, cb, hm_c.shape[2], LANES), lambda i: (0, i, 0, 0)),
        ],
        out_specs=pl.BlockSpec((cb, b, tot), lambda i: (i, 0, 0)),
        out_shape=jax.ShapeDtypeStruct((HY_WIDTH, b, tot), F32),
        scratch_shapes=[
            pltpu.VMEM((HY_ORDER, (hm_l.shape[2] - 1) * LANES, LANES), BF16),
            pltpu.VMEM((HY_ORDER, (hm_c.shape[2] - 1) * LANES, LANES), BF16),
        ],
        compiler_params=_params("parallel"),
        name="hyconv",
    )(sc_tab, u3, u3, u3, hm_l, hm_c)


def _zero_after(x):
    bits = lax.bitcast_convert_type(x[-8:, -LANES:], jnp.uint32)
    z = lax.shift_right_logical(lax.shift_right_logical(bits, jnp.uint32(16)), jnp.uint32(16))
    return z[0:1].astype(F32).astype(BF16)


def _attn_body(lam_ref, q_ref, k_ref, v_ref, sub_ref, o_ref, *, lam_init, heads):
    hw = 2 * DA_HEAD_DIM
    lane = lax.broadcasted_iota(jnp.int32, (1, hw), 1)
    first = lane < DA_HEAD_DIM
    lp = lam_ref[...]
    lam = (jnp.exp(jnp.sum(lp[0:1] * lp[1:2], keepdims=True))
           - jnp.exp(jnp.sum(lp[2:3] * lp[3:4], keepdims=True)) + lam_init)
    sub = sub_ref[...] * (1.0 - lam_init)

    def scores(h, after):
        q = q_ref[0, :, h * hw:(h + 1) * hw]
        if after is not None:
            q = q + after
        k = k_ref[0, :, h * hw:(h + 1) * hw]
        zero = jnp.zeros_like(q)
        return _dot_nt(jnp.where(first, q, zero), k), _dot_nt(jnp.where(first, zero, q), k)

    def weights(s1, s2):
        p1 = jnp.exp2(s1 - jnp.max(s1, axis=-1, keepdims=True))
        p2 = jnp.exp2(s2 - jnp.max(s2, axis=-1, keepdims=True))
        l1 = jnp.sum(p1, axis=-1, keepdims=True)
        l2 = jnp.sum(p2, axis=-1, keepdims=True)
        return (p1 - p2 * (lam * l1 / l2)).astype(BF16), 1.0 / l1

    def values(h, w):
        o = _dot(w[0], v_ref[0, :, h * hw:(h + 1) * hw]) * w[1]
        on = o * lax.rsqrt(jnp.mean(o * o, axis=-1, keepdims=True) + EPS) * sub
        o_ref[0, :, h * hw:(h + 1) * hw] = on.astype(BF16)

    s = [None] * heads
    w = [None] * heads
    after = None
    for step in range(heads + 2):
        if step < heads:
            s[step] = scores(step, after)
            after = _zero_after(s[step][1])
        if 0 <= step - 1 < heads:
            w[step - 1] = weights(*s[step - 1])
            s[step - 1] = None
        if 0 <= step - 2 < heads:
            values(step - 2, w[step - 2])


def _attention(lam_p, q, k, v, sub, lam_init, seq):
    b, tot, _ = q.shape
    ctx = tot - seq
    tq = TOK_TILE
    hw = 2 * DA_HEAD_DIM
    width = DA_HEADS * hw

    def call(q0, n_q, keys, k0, name):
        return pl.pallas_call(
            functools.partial(_attn_body, lam_init=lam_init, heads=DA_HEADS),
            grid=(b, n_q),
            in_specs=[
                pl.BlockSpec((4, DA_HEAD_DIM), lambda bi, j: (0, 0)),
                pl.BlockSpec((1, tq, width), lambda bi, j: (bi, q0 + j, 0)),
                pl.BlockSpec((1, keys, width), lambda bi, j: (bi, k0, 0)),
                pl.BlockSpec((1, keys, width), lambda bi, j: (bi, k0, 0)),
                pl.BlockSpec((1, hw), lambda bi, j: (0, 0)),
            ],
            out_specs=pl.BlockSpec((1, tq, width), lambda bi, j: (bi, j, 0)),
            out_shape=jax.ShapeDtypeStruct((b, n_q * tq, width), BF16),
            compiler_params=_params("parallel", "parallel"),
            name=name,
        )(lam_p, q, k, v, sub)

    return call(0, seq // tq, tot, 0, "diffattn"), call(seq // tq, ctx // tq, ctx, seq // ctx, "diffattn_ctx")


def _merge_body(x_ref, mod_ref, g1_ref, wg0_ref, wg1_ref, wg2_ref, yh_ref, ydl_ref, ydc_ref, ys_ref, wb_ref, wo_ref,
                o_ref, *, n_lat_tiles):
    x = x_ref[0]
    h = _norm_mod(x, g1_ref[...], mod_ref[0, 0:1, :], mod_ref[0, 1:2, :]).astype(BF16)
    y_da = jnp.where(pl.program_id(1) < n_lat_tiles, ydl_ref[0], ydc_ref[0])
    acc = None
    for n, (wg_ref, y) in enumerate(((wg0_ref, yh_ref[0]), (wg1_ref, y_da), (wg2_ref, ys_ref[0]))):
        gate = jax.nn.sigmoid(_dot(h, wg_ref[0]))
        term = gate * _dot(y, wb_ref[0, n])
        acc = term if acc is None else acc + term
    o_ref[0] = x + mod_ref[0, 2:3, :] * _dot(acc.astype(BF16), wo_ref[0])


def _merge(xa, mod_i, g1, w_in, layer, y_hy, y_da, y_sg, w_br, w_out, rows, n_lat_tiles, n_samples):
    b, _, d = xa.shape
    t = rows
    tm = TOK_TILE
    assert PRE_COLS % d == 0
    gate_spec = lambda n: pl.BlockSpec((1, d, d), lambda bi, j: (layer, 0, PRE_COLS // d + n))
    y_lat, y_ctx = y_da
    n_ctx_tiles = y_ctx.shape[1] // tm
    return pl.pallas_call(
        functools.partial(_merge_body, n_lat_tiles=n_lat_tiles),
        grid=(b, t // tm),
        in_specs=[
            _tok_spec(d, tm), _mod_spec(d, n_lat_tiles, n_samples), _const_spec((1, d)),
            gate_spec(0), gate_spec(1), gate_spec(2),
            _tok_spec(HY_WIDTH, tm),
            pl.BlockSpec((1, tm, HY_WIDTH), lambda bi, j: (bi, jnp.minimum(j, n_lat_tiles - 1), 0)),
            pl.BlockSpec((1, tm, HY_WIDTH), lambda bi, j: (bi, jnp.clip(j - n_lat_tiles, 0, n_ctx_tiles - 1), 0)),
            _tok_spec(HY_WIDTH, tm),
            pl.BlockSpec((1, 3, HY_WIDTH, d), lambda bi, j: (layer, 0, 0, 0)),
            pl.BlockSpec((1, d, d), lambda bi, j: (layer, 0, 0)),
        ],
        out_specs=_tok_spec(d, tm),
        out_shape=jax.ShapeDtypeStruct((b, t, d), F32),
        compiler_params=_params("parallel", "parallel"),
        name="merge",
    )(xa, mod_i, g1, w_in, w_in, w_in, y_hy, y_lat, y_ctx, y_sg, w_br, w_out)


def _ffn_body(x_ref, mod_ref, g2_ref, wg_ref, wu_ref, wd_ref, o_ref, *, chunk):
    x = x_ref[0]
    h = _norm_mod(x, g2_ref[...], mod_ref[0, 3:4, :], mod_ref[0, 4:5, :]).astype(BF16)
    ff = wg_ref.shape[2]
    acc = None
    for f0 in range(0, ff, chunk):
        f1 = min(ff, f0 + chunk)
        a = _dot(h, wg_ref[0, :, f0:f1])
        mid = (a * jax.nn.sigmoid(a) * _dot(h, wu_ref[0, :, f0:f1])).astype(BF16)
        term = _dot(mid, wd_ref[0, f0:f1, :])
        acc = term if acc is None else acc + term
    o_ref[0] = x + mod_ref[0, 5:6, :] * acc


def _ffn_dense(xa, mod_i, g2, wg, wu, wd, layer, n_lat_tiles, n_samples):
    b, t, d = xa.shape
    ff = wg.shape[2]
    tm = TOK_TILE
    return pl.pallas_call(
        functools.partial(_ffn_body, chunk=1024),
        grid=(b, t // tm),
        in_specs=[
            _tok_spec(d, tm), _mod_spec(d, n_lat_tiles, n_samples), _const_spec((1, d)),
            pl.BlockSpec((1, d, ff), lambda bi, j: (layer, 0, 0)),
            pl.BlockSpec((1, d, ff), lambda bi, j: (layer, 0, 0)),
            pl.BlockSpec((1, ff, d), lambda bi, j: (layer, 0, 0)),
        ],
        out_specs=_tok_spec(d, tm),
        out_shape=jax.ShapeDtypeStruct((b, t, d), F32),
        compiler_params=_params("parallel", "parallel"),
        name="ffn_dense",
    )(xa, mod_i, g2, wg, wu, wd)


def _router_body(x_ref, mod_ref, g2_ref, wr_ref, h_ref, r_ref):
    h = _norm_mod(x_ref[0], g2_ref[...], mod_ref[0, 3:4, :], mod_ref[0, 4:5, :])
    h_ref[0] = h
    h_hi, h_lo = _split_bf16(h)
    w_hi, w_lo = _split_bf16(wr_ref[...])
    logits = _dot_nt(w_hi, h_hi) + _dot_nt(w_hi, h_lo) + _dot_nt(w_lo, h_hi)
    eid = lax.broadcasted_iota(jnp.int32, logits.shape, 0)
    m1 = jnp.max(logits, axis=0, keepdims=True)
    i1 = jnp.min(jnp.where(logits == m1, eid, N_EXPERTS), axis=0, keepdims=True)
    rest = jnp.where(eid == i1, -jnp.inf, logits)
    m2 = jnp.max(rest, axis=0, keepdims=True)
    i2 = jnp.min(jnp.where(rest == m2, eid, N_EXPERTS), axis=0, keepdims=True)
    w1 = 1.0 / (1.0 + jnp.exp(m2 - m1))
    rows = lax.broadcasted_iota(jnp.int32, logits.shape, 0)
    out = jnp.where(rows == 0, i1.astype(F32), jnp.where(rows == 1, i2.astype(F32),
                    jnp.where(rows == 2, w1, jnp.where(rows == 3, 1.0 - w1, 0.0))))
    r_ref[0] = out


def _router(xa, mod_i, g2, wr_t, n_lat_tiles, n_samples):
    b, t, d = xa.shape
    tm = TOK_TILE
    return pl.pallas_call(
        _router_body,
        grid=(b, t // tm),
        in_specs=[
            _tok_spec(d, tm), _mod_spec(d, n_lat_tiles, n_samples), _const_spec((1, d)),
            _const_spec((N_EXPERTS, d)),
        ],
        out_specs=[_tok_spec(d, tm), pl.BlockSpec((1, N_EXPERTS, tm), lambda bi, j: (bi, 0, j))],
        out_shape=[jax.ShapeDtypeStruct((b, t, d), F32), jax.ShapeDtypeStruct((b, N_EXPERTS, t), F32)],
        compiler_params=_params("parallel", "parallel"),
        name="router",
    )(xa, mod_i, g2, wr_t)


def _row_copy(src_hbm, dst, sem, src_row, dst_row):
    return pltpu.make_async_copy(src_hbm.at[pl.ds(src_row, 1), :], dst.at[pl.ds(dst_row, 1), :], sem)


def _gather_rows(src_hbm, dst, sem, idx):
    rows = dst.shape[0]

    def start(g, c):
        for u in range(DMA_UNROLL):
            r = g * DMA_UNROLL + u
            _row_copy(src_hbm, dst, sem, idx(r), r).start(priority=u % 2)
        return c
    lax.fori_loop(0, rows // DMA_UNROLL, start, 0)


def _wait_rows(src_hbm, dst, sem):
    pltpu.make_async_copy(src_hbm.at[pl.ds(0, dst.shape[0]), :], dst, sem).wait()


def _expert_body(te_ref, nu_ref, rt_ref, rtn_ref, h_hbm, wg_ref, wu_ref, wd_ref, o_ref, xbuf, xbf, sem, *, chunk):
    t = pl.program_id(0)
    f = pl.program_id(1)
    last_f = pl.num_programs(1) - 1
    n_used = nu_ref[0]
    used = t < n_used
    slot = t % 2

    @pl.when(jnp.logical_and(f == 0, jnp.logical_and(used, t == 0)))
    def _first_gather():
        _gather_rows(h_hbm, xbuf.at[0], sem.at[0], lambda r: rt_ref[0, 0, r])

    @pl.when(f == 0)
    def _zero():
        o_ref[...] = jnp.zeros_like(o_ref)

    @pl.when(jnp.logical_and(used, f == 0))
    def _stage():
        _wait_rows(h_hbm, xbuf.at[slot], sem.at[slot])
        xbf[...] = xbuf[slot].astype(BF16)

    @pl.when(used)
    def _compute():
        xb = xbf[...]
        a = _dot(xb, wg_ref[0, 0].astype(BF16))
        b = _dot(xb, wu_ref[0, 0].astype(BF16))
        for u in range(chunk):
            r = f * chunk + u
            _row_copy(h_hbm, xbuf.at[1 - slot], sem.at[1 - slot], rtn_ref[0, 0, r], r).start(priority=u % 2)
        mid = (a * jax.nn.sigmoid(a) * b).astype(BF16)
        o_ref[...] += _dot(mid, wd_ref[0, 0].astype(BF16))

    @pl.when(jnp.logical_and(f == last_f, t == n_used - 1))
    def _drain():
        _wait_rows(h_hbm, xbuf.at[1 - slot], sem.at[1 - slot])


def _experts(tile_expert, n_used, row_token, h_flat, wg, wu, wd, layer):
    n_rows = row_token.shape[0]
    d = h_flat.shape[1]
    ff = wg.shape[3]
    nt = n_rows // ROW_TILE
    rt3 = row_token.reshape(nt, 1, ROW_TILE)
    grid_spec = pltpu.PrefetchScalarGridSpec(
        num_scalar_prefetch=2,
        grid=(nt, ff // FF_TILE),
        in_specs=[
            pl.BlockSpec((1, 1, ROW_TILE), lambda t, f, te, nu: (t, 0, 0), memory_space=pltpu.SMEM),
            pl.BlockSpec((1, 1, ROW_TILE), lambda t, f, te, nu: (jnp.minimum(t + 1, nt - 1), 0, 0),
                         memory_space=pltpu.SMEM),
            pl.BlockSpec(memory_space=pl.ANY),
            pl.BlockSpec((1, 1, d, FF_TILE), lambda t, f, te, nu: (layer, te[t], 0, f)),
            pl.BlockSpec((1, 1, d, FF_TILE), lambda t, f, te, nu: (layer, te[t], 0, f)),
            pl.BlockSpec((1, 1, FF_TILE, d), lambda t, f, te, nu: (layer, te[t], f, 0)),
        ],
        out_specs=pl.BlockSpec((ROW_TILE, d), lambda t, f, te, nu: (t, 0)),
        scratch_shapes=[
            pltpu.VMEM((2, ROW_TILE, d), F32),
            pltpu.VMEM((ROW_TILE, d), BF16),
            pltpu.SemaphoreType.DMA((2,)),
        ],
    )
    assert ROW_TILE % (ff // FF_TILE) == 0
    return pl.pallas_call(
        functools.partial(_expert_body, chunk=ROW_TILE // (ff // FF_TILE)),
        grid_spec=grid_spec,
        out_shape=jax.ShapeDtypeStruct((n_rows, d), F32),
        compiler_params=_params("arbitrary", "arbitrary"),
        name="experts",
    )(tile_expert, n_used, rt3, rt3, h_flat, wg, wu, wd)


def _combine_body(pos_ref, posn_ref, r_ref, x_ref, mod_ref, ye_hbm, o_ref, buf, sem):
    tm = x_ref.shape[1]
    step = pl.program_id(0) * pl.num_programs(1) + pl.program_id(1)
    n_steps = pl.num_programs(0) * pl.num_programs(1)
    slot = step % 2

    def gather(p_ref, s):
        for kk in range(2):
            _gather_rows(ye_hbm, buf.at[s, kk], sem.at[s, kk], lambda r, kk=kk: p_ref[0, kk, r])

    @pl.when(step == 0)
    def _first():
        gather(pos_ref, 0)

    @pl.when(step + 1 < n_steps)
    def _next():
        gather(posn_ref, 1 - slot)

    eye = (lax.broadcasted_iota(jnp.int32, (tm, tm), 0) == lax.broadcasted_iota(jnp.int32, (tm, tm), 1)).astype(BF16)
    r_hi, r_lo = _split_bf16(r_ref[0])
    rcol = _dot_nt(eye, r_hi) + _dot_nt(eye, r_lo)
    for kk in range(2):
        _wait_rows(ye_hbm, buf.at[slot, kk], sem.at[slot, kk])
    mix = buf[slot, 0] * rcol[:, 2:3] + buf[slot, 1] * rcol[:, 3:4]
    o_ref[0] = x_ref[0] + mod_ref[0, 5:6, :] * mix


def _combine(pos, route, xa, mod_i, ye, n_lat_tiles, n_samples):
    b, t, d = xa.shape
    tm = TOK_TILE
    nj = t // tm
    return pl.pallas_call(
        _combine_body,
        grid=(b, nj),
        in_specs=[
            pl.BlockSpec((1, 2, tm), lambda bi, j: (bi * nj + j, 0, 0), memory_space=pltpu.SMEM),
            pl.BlockSpec((1, 2, tm), lambda bi, j: (jnp.minimum(bi * nj + j + 1, b * nj - 1), 0, 0),
                         memory_space=pltpu.SMEM),
            pl.BlockSpec((1, N_EXPERTS, tm), lambda bi, j: (bi, 0, j)),
            _tok_spec(d, tm), _mod_spec(d, n_lat_tiles, n_samples),
            pl.BlockSpec(memory_space=pl.ANY),
        ],
        out_specs=_tok_spec(d, tm),
        out_shape=jax.ShapeDtypeStruct((b, t, d), F32),
        scratch_shapes=[pltpu.VMEM((2, 2, tm, d), F32), pltpu.SemaphoreType.DMA((2, 2))],
        compiler_params=_params("arbitrary", "arbitrary"),
        name="moe_combine",
    )(pos, pos, route, xa, mod_i, ye)


def _moe(xa, mod_i, g2, wr_t, wg, wu, wd, layer, n_lat_tiles, n_samples):
    b, t, d = xa.shape
    h, route = _router(xa, mod_i, g2, wr_t, n_lat_tiles, n_samples)
    n_tok = b * t
    expert = route[:, 0:2, :].astype(jnp.int32)
    onehot = (expert.reshape(-1)[:, None] == jnp.arange(N_EXPERTS)[None, :]).astype(jnp.int32)
    counts = jnp.sum(onehot, axis=0)
    rank = jnp.sum((jnp.cumsum(onehot, axis=0) - onehot) * onehot, axis=1)
    padded = ((counts + ROW_TILE - 1) // ROW_TILE) * ROW_TILE
    ends = jnp.cumsum(padded)
    pos = ((ends - padded)[expert.reshape(-1)] + rank).reshape(b, 2, t)
    n_rows = ((2 * n_tok + N_EXPERTS * (ROW_TILE - 1)) // ROW_TILE + 1) * ROW_TILE
    token = jnp.broadcast_to(jnp.arange(b)[:, None, None] * t + jnp.arange(t)[None, None, :], (b, 2, t))
    row_token = jnp.zeros((n_rows,), jnp.int32).at[pos.reshape(-1)].set(
        token.reshape(-1), unique_indices=True, mode="promise_in_bounds")
    tile_start = jnp.arange(n_rows // ROW_TILE) * ROW_TILE
    tile_expert = jnp.minimum(jnp.sum(tile_start[:, None] >= ends[None, :], axis=1), N_EXPERTS - 1).astype(jnp.int32)
    n_used = (ends[-1:] // ROW_TILE).astype(jnp.int32)
    ye = _experts(tile_expert, n_used, row_token, h.reshape(n_tok, d), wg, wu, wd, layer)
    nj = t // TOK_TILE
    pos_tiles = pos.reshape(b, 2, nj, TOK_TILE).transpose(0, 2, 1, 3).reshape(b * nj, 2, TOK_TILE)
    return _combine(pos_tiles, route, xa, mod_i, ye, n_lat_tiles, n_samples)


def _rope_tables(seq, ctx):
    rows = seq // GRID_W
    row = jnp.repeat(jnp.arange(rows), GRID_W).astype(F32)
    col = jnp.tile(jnp.arange(GRID_W), rows).astype(F32)
    half = DA_HEAD_DIM // 2
    inv = ROPE_BASE ** (-jnp.arange(0, half, 2, dtype=F32) / half)
    ang = jnp.concatenate([row[:, None] * inv, row[:, None] * inv, col[:, None] * inv, col[:, None] * inv], axis=1)
    sign = jnp.tile(jnp.concatenate([-jnp.ones((half // 2,), F32), jnp.ones((half // 2,), F32)]), 2)
    cos = jnp.concatenate([jnp.cos(ang), jnp.ones((ctx, DA_HEAD_DIM), F32)], axis=0)
    sin = jnp.concatenate([jnp.sin(ang) * sign, jnp.zeros((ctx, DA_HEAD_DIM), F32)], axis=0)
    return jnp.tile(cos, (1, LANES // DA_HEAD_DIM)), jnp.tile(sin, (1, LANES // DA_HEAD_DIM))


def _filter_weights(w1, b1, w2, b2, w3, freq):
    col = lambda v: v.reshape(-1, 1)
    bands = jnp.linspace(1e-4, HY_BANDS - 1, HY_BANDS, dtype=F32)
    deltas = jnp.abs(jnp.linspace(HY_MIN_DECAY, HY_MAX_DECAY, HY_WIDTH, dtype=F32))
    half = HY_ORDER * HY_WIDTH
    return (w1[0:1].T, w1[1:1 + HY_BANDS].T, w1[1 + HY_BANDS:].T, col(b1), w2.T, col(b2), col(freq),
            w3[:, :half].T, w3[:, half:].T, col(bands), col(jnp.tile(deltas, HY_ORDER)))


def kernel(x, c, ctx, c_ctx, w_mod, b_mod, norm1_g, norm2_g, w_in, hy_conv_w, hy_conv_b, hy_f_w1, hy_f_b1, hy_f_w2, hy_f_b2, hy_f_w3, hy_f_freq, hy_skip, q_norm_g, k_norm_g, lam_q1, lam_k1, lam_q2, lam_k2, subln_g, sg_norm_g, sg_w, sg_b, w_branch, w_out, ffn_wg, ffn_wu, ffn_wd, router_w, moe_wg, moe_wu, moe_wd):
    n_samples, seq, d = x.shape
    n_ctx = ctx.shape[1]
    depth = w_mod.shape[0]
    assert seq % CONV_TILE == 0 and n_ctx % CONV_TILE == 0 and seq % GRID_W == 0 and seq % n_ctx == 0
    assert n_samples + 1 <= MOD_ROWS
    n_lat_tiles = seq // TOK_TILE
    tot = seq + n_ctx

    xa = jnp.concatenate([x, ctx], axis=1)
    cond = jnp.zeros((MOD_ROWS, d), F32).at[:n_samples].set(c).at[n_samples].set(c_ctx)
    mod = _modvec(cond, w_mod, b_mod).reshape(depth, MOD_ROWS, 6, d)

    cos_t, sin_t = _rope_tables(seq, n_ctx)
    group = jnp.arange(QK_COLS) // DA_HEAD_DIM
    gmat = (group[:, None] == group[None, :]).astype(BF16)
    w_in_b = w_in.astype(BF16)
    w_br_b = w_branch.astype(BF16)
    w_out_b = w_out.astype(BF16)
    ffn_b = [w.astype(BF16) for w in (ffn_wg, ffn_wu, ffn_wd)]

    for i in range(depth):
        lam_init = 0.8 - 0.6 * math.exp(-0.3 * i)
        g1 = norm1_g[i].reshape(1, d)
        g2 = norm2_g[i].reshape(1, d)
        hy, q, k, v, y_sg = _inproj(
            xa, mod[i], g1, w_in_b, i, cos_t, sin_t,
            jnp.tile(q_norm_g[i], QK_COLS // DA_HEAD_DIM).reshape(1, QK_COLS),
            jnp.tile(k_norm_g[i], QK_COLS // DA_HEAD_DIM).reshape(1, QK_COLS),
            gmat, sg_norm_g[i].reshape(1, SG_WIDTH), sg_w[i].astype(BF16),
            jnp.repeat(sg_b[i].T, SG_WIDTH // SG_GROUPS, axis=1), n_lat_tiles, n_samples)

        fw = _filter_weights(hy_f_w1[i], hy_f_b1[i], hy_f_w2[i], hy_f_b2[i], hy_f_w3[i], hy_f_freq[i])
        hm_l = _hyfilter(fw, seq).reshape(HY_ORDER, HY_WIDTH, 2 * seq // LANES, LANES)
        hm_c = _hyfilter(fw, n_ctx).reshape(HY_ORDER, HY_WIDTH, 2 * n_ctx // LANES, LANES)
        cw = hy_conv_w[i].reshape(3, 3, HY_WIDTH)
        cbias = hy_conv_b[i].reshape(3, HY_WIDTH)
        sc_rows = [cw.reshape(9, HY_WIDTH), cbias, hy_skip[i]]
        sc_rows.append(jnp.zeros((HY_SCALARS - 12 - HY_ORDER, HY_WIDTH), F32))
        sc_tab = jnp.concatenate(sc_rows, axis=0).T.reshape(-1)
        y_hy = _hyconv(sc_tab, jnp.transpose(hy, (2, 0, 1)), hm_l, hm_c, seq, n_ctx)
        y_hy = jnp.transpose(y_hy, (1, 2, 0)).astype(BF16)

        lam_p = jnp.stack([lam_q1[i], lam_k1[i], lam_q2[i], lam_k2[i]], axis=0)
        y_da = _attention(lam_p, q, k, v, subln_g[i].reshape(1, DA_V_DIM), lam_init, seq)

        xa = _merge(xa, mod[i], g1, w_in_b, i, y_hy, y_da, y_sg, w_br_b, w_out_b,
                    seq if i == depth - 1 else tot, n_lat_tiles, n_samples)

        j = i // 2
        if i % 2 == 0:
            xa = _ffn_dense(xa, mod[i], g2, *ffn_b, j, n_lat_tiles, n_samples)
        else:
            xa = _moe(xa, mod[i], g2, router_w[j].T, moe_wg, moe_wu, moe_wd, j, n_lat_tiles, n_samples)
    return xa[:, :seq]
```

```python
import functools
import math

import jax
import jax.numpy as jnp
from jax import lax
from jax.experimental import pallas as pl
from jax.experimental.pallas import tpu as pltpu

F32 = jnp.float32
BF16 = jnp.bfloat16
EPS = 1e-6

GRID_W = 64
HY_WIDTH = 512
HY_ORDER = 2
HY_BANDS = 16
HY_FFN = 64
HY_MAX_DECAY = math.log(1e-2) / 0.3
HY_MIN_DECAY = math.log(1e-2) / 1.5
DA_HEADS = 4
DA_HEAD_DIM = 64
DA_V_DIM = 2 * DA_HEAD_DIM
ROPE_BASE = 10000.0
SG_WIDTH = 512
SG_GROUPS = 4
SG_CHUNK = 128
N_EXPERTS = 8
HY_COLS = 3 * HY_WIDTH
QK_COLS = DA_HEADS * 2 * DA_HEAD_DIM
DA_COLS = 2 * QK_COLS + DA_HEADS * DA_V_DIM
SG_COLS = 2 * SG_WIDTH
PRE_COLS = HY_COLS + DA_COLS + SG_COLS

LANES = 128
SUBLANES = 8
TOK_TILE = 256
CONV_TILE = 256
HY_CHANNELS_PER_STEP = 16
HY_SCALARS = 16
ROW_TILE = 896
FF_TILE = 512
MOD_ROWS = 16
VMEM_LIMIT = 56 * 1024 * 1024


def _dot(a, b):
    return jnp.dot(a, b, preferred_element_type=F32)


def _dot_nt(a, b):
    return lax.dot_general(a, b, (((1,), (1,)), ((), ())), preferred_element_type=F32)


_hdot = functools.partial(jnp.dot, precision=lax.Precision.HIGHEST, preferred_element_type=F32)


def _split_bf16(a):
    hi = a.astype(BF16)
    return hi, (a - hi.astype(F32)).astype(BF16)


def _dot3(a, b):
    a_hi, a_lo = _split_bf16(a)
    b_hi, b_lo = _split_bf16(b)
    return _dot(a_hi, b_hi) + _dot(a_lo, b_hi) + _dot(a_hi, b_lo)


def _norm_mod(x, g, shift, scale):
    ms = jnp.mean(x * x, axis=-1, keepdims=True)
    return (x * lax.rsqrt(ms + EPS) * g) * (1.0 + scale) + shift


def _params(*sem):
    return pltpu.CompilerParams(dimension_semantics=sem, vmem_limit_bytes=VMEM_LIMIT)


def _modvec_body(cond_ref, w_ref, b_ref, o_ref):
    cnd = cond_ref[...]
    s = cnd * jax.nn.sigmoid(cnd)
    s_hi, s_lo = _split_bf16(s)
    w_hi, w_lo = _split_bf16(w_ref[0])
    o_ref[0] = _dot(s_hi, w_hi) + _dot(s_lo, w_hi) + _dot(s_hi, w_lo) + b_ref[0]


def _modvec(cond, w_mod, b_mod):
    depth, d, n = w_mod.shape
    tn = n // 4
    return pl.pallas_call(
        _modvec_body,
        grid=(depth, n // tn),
        in_specs=[
            pl.BlockSpec((MOD_ROWS, d), lambda i, j: (0, 0)),
            pl.BlockSpec((1, d, tn), lambda i, j: (i, 0, j)),
            pl.BlockSpec((1, 1, tn), lambda i, j: (i, 0, j)),
        ],
        out_specs=pl.BlockSpec((1, MOD_ROWS, tn), lambda i, j: (i, 0, j)),
        out_shape=jax.ShapeDtypeStruct((depth, MOD_ROWS, n), F32),
        compiler_params=_params("parallel", "parallel"),
        name="modvec",
    )(cond, w_mod, b_mod.reshape(depth, 1, n))


def _tok_spec(width, tm):
    return pl.BlockSpec((1, tm, width), lambda b, j: (b, j, 0))


def _const_spec(shape):
    nd = len(shape)
    return pl.BlockSpec(shape, lambda b, j: (0,) * nd)


def _mod_spec(d, n_lat_tiles, n_samples):
    return pl.BlockSpec((1, 6, d), lambda b, j: (jnp.where(j < n_lat_tiles, b, n_samples), 0, 0))


def _qk_norm_rope(a, g_tile, gmat, cos, sin_signed, first_half):
    ss = _dot((a * a).astype(BF16), gmat)
    an = a * lax.rsqrt(ss * (1.0 / DA_HEAD_DIM) + EPS) * g_tile
    outs = []
    for ci in range(QK_COLS // LANES):
        ch = an[:, ci * LANES:(ci + 1) * LANES]
        partner = jnp.where(first_half, pltpu.roll(ch, LANES - 16, 1), pltpu.roll(ch, 16, 1))
        outs.append(ch * cos + partner * sin_signed)
    return jnp.concatenate(outs, axis=1)


def _inproj_body(x_ref, mod_ref, g1_ref, w_ref, cos_ref, sin_ref, qg_ref, kg_ref, gmat_ref,
                 sgg_ref, sgw_ref, sgb_ref, hy_ref, q_ref, k_ref, v_ref, sg_ref):
    tm = x_ref.shape[1]
    h = _norm_mod(x_ref[0], g1_ref[...], mod_ref[0, 0:1, :], mod_ref[0, 1:2, :]).astype(BF16)
    hy_ref[0] = _dot(h, w_ref[0, :, 0:HY_COLS])

    lane = lax.broadcasted_iota(jnp.int32, (1, LANES), 1)
    first_half = (lane % 32) < 16
    cos = cos_ref[...]
    sin_signed = sin_ref[...]
    gmat = gmat_ref[...]
    o = HY_COLS
    q = _qk_norm_rope(_dot(h, w_ref[0, :, o:o + QK_COLS]), qg_ref[...], gmat, cos, sin_signed, first_half)
    q_ref[0] = (q * (DA_HEAD_DIM ** -0.5 * math.log2(math.e))).astype(BF16)
    o += QK_COLS
    k = _qk_norm_rope(_dot(h, w_ref[0, :, o:o + QK_COLS]), kg_ref[...], gmat, cos, sin_signed, first_half)
    k_ref[0] = k.astype(BF16)
    o += QK_COLS
    v_ref[0] = _dot(h, w_ref[0, :, o:o + DA_HEADS * DA_V_DIM]).astype(BF16)
    o += DA_HEADS * DA_V_DIM

    z = jax.nn.gelu(_dot(h, w_ref[0, :, o:o + SG_COLS]))
    u = z[:, :SG_WIDTH]
    vv = z[:, SG_WIDTH:]
    vn = (vv * lax.rsqrt(jnp.mean(vv * vv, axis=-1, keepdims=True) + EPS) * sgg_ref[...]).astype(BF16)
    gw = SG_WIDTH // SG_GROUPS
    for ch in range(tm // SG_CHUNK):
        r0 = ch * SG_CHUNK
        cols = []
        for g in range(SG_GROUPS):
            s = _dot(sgw_ref[g], vn[r0:r0 + SG_CHUNK, g * gw:(g + 1) * gw]) + sgb_ref[:, g * gw:(g + 1) * gw]
            cols.append(u[r0:r0 + SG_CHUNK, g * gw:(g + 1) * gw] * s)
        sg_ref[0, r0:r0 + SG_CHUNK, :] = jnp.concatenate(cols, axis=1).astype(BF16)


def _inproj(xa, mod_i, g1, w_in, layer, cos_t, sin_t, qg, kg, gmat, sgg, sgw, sgb, n_lat_tiles, n_samples):
    b, t, d = xa.shape
    tm = TOK_TILE
    outs = [jax.ShapeDtypeStruct((b, t, HY_COLS), F32)] + [jax.ShapeDtypeStruct((b, t, QK_COLS), BF16)] * 4
    return pl.pallas_call(
        _inproj_body,
        grid=(b, t // tm),
        in_specs=[
            _tok_spec(d, tm),
            _mod_spec(d, n_lat_tiles, n_samples),
            _const_spec((1, d)),
            pl.BlockSpec((1, d, PRE_COLS), lambda bi, j: (layer, 0, 0)),
            pl.BlockSpec((tm, LANES), lambda bi, j: (j, 0)),
            pl.BlockSpec((tm, LANES), lambda bi, j: (j, 0)),
            _const_spec((1, QK_COLS)),
            _const_spec((1, QK_COLS)),
            _const_spec((QK_COLS, QK_COLS)),
            _const_spec((1, SG_WIDTH)),
            _const_spec((SG_GROUPS, SG_CHUNK, SG_CHUNK)),
            _const_spec((SG_CHUNK, SG_WIDTH)),
        ],
        out_specs=[_tok_spec(HY_COLS, tm)] + [_tok_spec(QK_COLS, tm)] * 4,
        out_shape=outs,
        compiler_params=_params("parallel", "parallel"),
        name="inproj",
    )(xa, mod_i, g1, w_in, cos_t, sin_t, qg, kg, gmat, sgg, sgw, sgb)


def _hyfilter_body(w1t_ref, w1c_ref, w1s_ref, b1_ref, w2_ref, b2_ref, fr_ref, w3f_ref, w3b_ref,
                   bands_ref, dl_ref, o_ref, *, seq):
    n = 2 * seq
    xi = lax.broadcasted_iota(jnp.int32, (1, n), 1)
    lag = xi - (seq - 1)
    pos = jnp.abs(lag).astype(F32)
    t = pos / (seq - 1)
    ang = 2.0 * math.pi * pos * bands_ref[...] / seq
    fr = fr_ref[...]
    z1 = w1t_ref[...] * t + _hdot(w1c_ref[...], jnp.cos(ang)) + _hdot(w1s_ref[...], jnp.sin(ang)) + b1_ref[...]
    h1 = jnp.sin(fr * z1)
    h2 = jnp.sin(fr * (_hdot(w2_ref[...], h1) + b2_ref[...]))
    bwd = _dot3(w3b_ref[...], h2[:, :seq])
    fwd = _dot3(w3f_ref[...], h2[:, seq:])
    fwd0 = _dot3(w3f_ref[...], h2[:, seq - LANES:seq])
    at_zero = jnp.where(lag[:, seq - LANES:seq] == 0, fwd0, 0.0)
    k = jnp.concatenate([bwd[:, :seq - LANES], bwd[:, seq - LANES:] + at_zero, fwd], axis=1)
    k = jnp.where(xi < n - 1, k * jnp.exp(-t * dl_ref[...]), 0.0)
    o_ref[...] = k / jnp.sum(jnp.abs(k), axis=-1, keepdims=True)


def _hyfilter(fw, seq):
    rows = HY_ORDER * HY_WIDTH
    rb = 256
    n = 2 * seq
    small = lambda shape: pl.BlockSpec(shape, lambda i: (0, 0))
    return pl.pallas_call(
        functools.partial(_hyfilter_body, seq=seq),
        grid=(rows // rb,),
        in_specs=[
            small((HY_FFN, 1)), small((HY_FFN, HY_BANDS)), small((HY_FFN, HY_BANDS)), small((HY_FFN, 1)),
            small((HY_FFN, HY_FFN)), small((HY_FFN, 1)), small((HY_FFN, 1)),
            pl.BlockSpec((rb, HY_FFN), lambda i: (i, 0)),
            pl.BlockSpec((rb, HY_FFN), lambda i: (i, 0)),
            small((HY_BANDS, 1)),
            pl.BlockSpec((rb, 1), lambda i: (i, 0)),
        ],
        out_specs=pl.BlockSpec((rb, n), lambda i: (i, 0)),
        out_shape=jax.ShapeDtypeStruct((rows, n), F32),
        compiler_params=_params("parallel"),
        name="hyfilter",
    )(*fw)


def _hyconv_body(sc_ref, z_ref, x1_ref, x2_ref, hml_ref, hmc_ref, o_ref, big_l, big_c, *, seq, ctx, cb):
    tot = seq + ctx
    lane = lax.broadcasted_iota(jnp.int32, (1, tot), 1)
    has_prev = jnp.logical_and(lane != 0, lane != seq)
    has_next = jnp.logical_and(lane != seq - 1, lane != tot - 1)
    row = lax.broadcasted_iota(jnp.int32, (LANES, LANES), 0)
    col = lax.broadcasted_iota(jnp.int32, (LANES, LANES), 1)
    lower = col <= row
    c0 = pl.program_id(0) * cb

    def short_conv(p, base, part):
        prev = jnp.where(has_prev, pltpu.roll(p, 1, 1), 0.0)
        nxt = jnp.where(has_next, pltpu.roll(p, tot - 1, 1), 0.0)
        return (sc_ref[base + 9 + part] + sc_ref[base + part] * prev
                + sc_ref[base + 3 + part] * p + sc_ref[base + 6 + part] * nxt)

    def build(hm_ref, o, ci, big_ref):
        n_rows = hm_ref.shape[2]
        prev = None
        for rp in range(n_rows - 1, -1, -1):
            r = hm_ref[o, ci, rp:rp + 1, :]
            cur = pltpu.roll(jnp.broadcast_to(r, (LANES, LANES)), 1, 1, stride=1, stride_axis=0)
            if prev is not None:
                rho = n_rows - 2 - rp
                big_ref[o, rho * LANES:(rho + 1) * LANES, :] = jnp.where(lower, cur, prev).astype(BF16)
            prev = cur

    def long_conv(zz, big_ref, o, length, base):
        nblk = length // CONV_TILE
        mid = (2 * nblk - 1) * LANES
        ys = [None] * nblk
        for dd in range(-(nblk - 1), nblk):
            r0 = mid - CONV_TILE * dd
            w = jnp.concatenate([big_ref[o, r0:r0 + CONV_TILE, :],
                                 big_ref[o, r0 - LANES:r0 - LANES + CONV_TILE, :]], axis=1)
            js = list(range(max(0, -dd), min(nblk, nblk - dd)))
            lhs = jnp.concatenate([zz[:, base + j * CONV_TILE:base + (j + 1) * CONV_TILE] for j in js], axis=0)
            out = _dot(lhs.astype(BF16), w)
            nb = zz.shape[0]
            for kk, j in enumerate(js):
                piece = out[kk * nb:(kk + 1) * nb]
                ys[j + dd] = piece if ys[j + dd] is None else ys[j + dd] + piece
        return ys

    def chan(ci, carry):
        base = (c0 + ci) * HY_SCALARS
        z = short_conv(z_ref[ci], base, 0)
        gates = (short_conv(x1_ref[ci], base, 1), short_conv(x2_ref[ci], base, 2))
        for o in range(HY_ORDER):
            build(hml_ref, o, ci, big_l)
            build(hmc_ref, o, ci, big_c)
        for o in range(HY_ORDER):
            y = jnp.concatenate(long_conv(z, big_l, o, seq, 0) + long_conv(z, big_c, o, ctx, seq), axis=1)
            z = gates[o] * (y + sc_ref[base + 12 + o] * z)
        o_ref[ci] = z
        return carry

    lax.fori_loop(0, cb, chan, 0, unroll=8)


def _hyconv(sc_tab, u3, hm_l, hm_c, seq, ctx):
    _, b, tot = u3.shape
    cb = HY_CHANNELS_PER_STEP
    nblk = HY_WIDTH // cb
    slab = lambda part: pl.BlockSpec((cb, b, tot), lambda i: (part * nblk + i, 0, 0))
    return pl.pallas_call(
        functools.partial(_hyconv_body, seq=seq, ctx=ctx, cb=cb),
        grid=(nblk,),
        in_specs=[
            pl.BlockSpec(memory_space=pltpu.SMEM),
            slab(0), slab(1), slab(2),
            pl.BlockSpec((HY_ORDER, cb, hm_l.shape[2], LANES), lambda i: (0, i, 0, 0)),
            pl.BlockSpec((HY_ORDER, cb, hm_c.shape[2], LANES), lambda i: (0, i, 0, 0)),
        ],
        out_specs=pl.BlockSpec((cb, b, tot), lambda i: (i, 0, 0)),
        out_shape=jax.ShapeDtypeStruct((HY_WIDTH, b, tot), F32),
        scratch_shapes=[
            pltpu.VMEM((HY_ORDER, (hm_l.shape[2] - 1) * LANES, LANES), BF16),
            pltpu.VMEM((HY_ORDER, (hm_c.shape[2] - 1) * LANES, LANES), BF16),
        ],
        compiler_params=_params("parallel"),
        name="hyconv",
    )(sc_tab, u3, u3, u3, hm_l, hm_c)


def _zero_after(x):
    bits = lax.bitcast_convert_type(x[-8:, -LANES:], jnp.uint32)
    z = lax.shift_right_logical(lax.shift_right_logical(bits, jnp.uint32(16)), jnp.uint32(16))
    return z[0:1].astype(F32).astype(BF16)


def _attn_body(lam_ref, q_ref, k_ref, v_ref, sub_ref, o_ref, *, lam_init, heads):
    hw = 2 * DA_HEAD_DIM
    lane = lax.broadcasted_iota(jnp.int32, (1, hw), 1)
    first = lane < DA_HEAD_DIM
    lp = lam_ref[...]
    lam = (jnp.exp(jnp.sum(lp[0:1] * lp[1:2], keepdims=True))
           - jnp.exp(jnp.sum(lp[2:3] * lp[3:4], keepdims=True)) + lam_init)
    sub = sub_ref[...] * (1.0 - lam_init)

    def scores(h, after):
        q = q_ref[0, :, h * hw:(h + 1) * hw]
        if after is not None:
            q = q + after
        k = k_ref[0, :, h * hw:(h + 1) * hw]
        zero = jnp.zeros_like(q)
        return _dot_nt(jnp.where(first, q, zero), k), _dot_nt(jnp.where(first, zero, q), k)

    def weights(s1, s2):
        p1 = jnp.exp2(s1 - jnp.max(s1, axis=-1, keepdims=True))
        p2 = jnp.exp2(s2 - jnp.max(s2, axis=-1, keepdims=True))
        l1 = jnp.sum(p1, axis=-1, keepdims=True)
        l2 = jnp.sum(p2, axis=-1, keepdims=True)
        return (p1 - p2 * (lam * l1 / l2)).astype(BF16), 1.0 / l1

    def values(h, w):
        o = _dot(w[0], v_ref[0, :, h * hw:(h + 1) * hw]) * w[1]
        on = o * lax.rsqrt(jnp.mean(o * o, axis=-1, keepdims=True) + EPS) * sub
        o_ref[0, :, h * hw:(h + 1) * hw] = on.astype(BF16)

    s = [None] * heads
    w = [None] * heads
    after = None
    for step in range(heads + 2):
        if step < heads:
            s[step] = scores(step, after)
            after = _zero_after(s[step][1])
        if 0 <= step - 1 < heads:
            w[step - 1] = weights(*s[step - 1])
            s[step - 1] = None
        if 0 <= step - 2 < heads:
            values(step - 2, w[step - 2])


def _attention(lam_p, q, k, v, sub, lam_init, seq):
    b, tot, _ = q.shape
    ctx = tot - seq
    tq = TOK_TILE
    hw = 2 * DA_HEAD_DIM
    width = DA_HEADS * hw

    def call(q0, n_q, keys, k0, name):
        return pl.pallas_call(
            functools.partial(_attn_body, lam_init=lam_init, heads=DA_HEADS),
            grid=(b, n_q),
            in_specs=[
                pl.BlockSpec((4, DA_HEAD_DIM), lambda bi, j: (0, 0)),
                pl.BlockSpec((1, tq, width), lambda bi, j: (bi, q0 + j, 0)),
                pl.BlockSpec((1, keys, width), lambda bi, j: (bi, k0, 0)),
                pl.BlockSpec((1, keys, width), lambda bi, j: (bi, k0, 0)),
                pl.BlockSpec((1, hw), lambda bi, j: (0, 0)),
            ],
            out_specs=pl.BlockSpec((1, tq, width), lambda bi, j: (bi, j, 0)),
            out_shape=jax.ShapeDtypeStruct((b, n_q * tq, width), BF16),
            compiler_params=_params("parallel", "parallel"),
            name=name,
        )(lam_p, q, k, v, sub)

    return call(0, seq // tq, tot, 0, "diffattn"), call(seq // tq, ctx // tq, ctx, seq // ctx, "diffattn_ctx")


def _merge_body(x_ref, mod_ref, g1_ref, wg0_ref, wg1_ref, wg2_ref, yh_ref, ydl_ref, ydc_ref, ys_ref, wb_ref, wo_ref,
                o_ref, *, n_lat_tiles):
    x = x_ref[0]
    h = _norm_mod(x, g1_ref[...], mod_ref[0, 0:1, :], mod_ref[0, 1:2, :]).astype(BF16)
    y_da = jnp.where(pl.program_id(1) < n_lat_tiles, ydl_ref[0], ydc_ref[0])
    acc = None
    for n, (wg_ref, y) in enumerate(((wg0_ref, yh_ref[0]), (wg1_ref, y_da), (wg2_ref, ys_ref[0]))):
        gate = jax.nn.sigmoid(_dot(h, wg_ref[0]))
        term = gate * _dot(y, wb_ref[0, n])
        acc = term if acc is None else acc + term
    o_ref[0] = x + mod_ref[0, 2:3, :] * _dot(acc.astype(BF16), wo_ref[0])


def _merge(xa, mod_i, g1, w_in, layer, y_hy, y_da, y_sg, w_br, w_out, rows, n_lat_tiles, n_samples):
    b, _, d = xa.shape
    t = rows
    tm = TOK_TILE
    assert PRE_COLS % d == 0
    gate_spec = lambda n: pl.BlockSpec((1, d, d), lambda bi, j: (layer, 0, PRE_COLS // d + n))
    y_lat, y_ctx = y_da
    n_ctx_tiles = y_ctx.shape[1] // tm
    return pl.pallas_call(
        functools.partial(_merge_body, n_lat_tiles=n_lat_tiles),
        grid=(b, t // tm),
        in_specs=[
            _tok_spec(d, tm), _mod_spec(d, n_lat_tiles, n_samples), _const_spec((1, d)),
            gate_spec(0), gate_spec(1), gate_spec(2),
            _tok_spec(HY_WIDTH, tm),
            pl.BlockSpec((1, tm, HY_WIDTH), lambda bi, j: (bi, jnp.minimum(j, n_lat_tiles - 1), 0)),
            pl.BlockSpec((1, tm, HY_WIDTH), lambda bi, j: (bi, jnp.clip(j - n_lat_tiles, 0, n_ctx_tiles - 1), 0)),
            _tok_spec(HY_WIDTH, tm),
            pl.BlockSpec((1, 3, HY_WIDTH, d), lambda bi, j: (layer, 0, 0, 0)),
            pl.BlockSpec((1, d, d), lambda bi, j: (layer, 0, 0)),
        ],
        out_specs=_tok_spec(d, tm),
        out_shape=jax.ShapeDtypeStruct((b, t, d), F32),
        compiler_params=_params("parallel", "parallel"),
        name="merge",
    )(xa, mod_i, g1, w_in, w_in, w_in, y_hy, y_lat, y_ctx, y_sg, w_br, w_out)


def _ffn_body(x_ref, mod_ref, g2_ref, wg_ref, wu_ref, wd_ref, o_ref, *, chunk):
    x = x_ref[0]
    h = _norm_mod(x, g2_ref[...], mod_ref[0, 3:4, :], mod_ref[0, 4:5, :]).astype(BF16)
    ff = wg_ref.shape[2]
    acc = None
    for f0 in range(0, ff, chunk):
        f1 = min(ff, f0 + chunk)
        a = _dot(h, wg_ref[0, :, f0:f1])
        mid = (a * jax.nn.sigmoid(a) * _dot(h, wu_ref[0, :, f0:f1])).astype(BF16)
        term = _dot(mid, wd_ref[0, f0:f1, :])
        acc = term if acc is None else acc + term
    o_ref[0] = x + mod_ref[0, 5:6, :] * acc


def _ffn_dense(xa, mod_i, g2, wg, wu, wd, layer, n_lat_tiles, n_samples):
    b, t, d = xa.shape
    ff = wg.shape[2]
    tm = TOK_TILE
    return pl.pallas_call(
        functools.partial(_ffn_body, chunk=1024),
        grid=(b, t // tm),
        in_specs=[
            _tok_spec(d, tm), _mod_spec(d, n_lat_tiles, n_samples), _const_spec((1, d)),
            pl.BlockSpec((1, d, ff), lambda bi, j: (layer, 0, 0)),
            pl.BlockSpec((1, d, ff), lambda bi, j: (layer, 0, 0)),
            pl.BlockSpec((1, ff, d), lambda bi, j: (layer, 0, 0)),
        ],
        out_specs=_tok_spec(d, tm),
        out_shape=jax.ShapeDtypeStruct((b, t, d), F32),
        compiler_params=_params("parallel", "parallel"),
        name="ffn_dense",
    )(xa, mod_i, g2, wg, wu, wd)


def _router_body(x_ref, mod_ref, g2_ref, wr_ref, h_ref, r_ref):
    h = _norm_mod(x_ref[0], g2_ref[...], mod_ref[0, 3:4, :], mod_ref[0, 4:5, :])
    h_ref[0] = h
    h_hi, h_lo = _split_bf16(h)
    w_hi, w_lo = _split_bf16(wr_ref[...])
    logits = _dot_nt(w_hi, h_hi) + _dot_nt(w_hi, h_lo) + _dot_nt(w_lo, h_hi)
    eid = lax.broadcasted_iota(jnp.int32, logits.shape, 0)
    m1 = jnp.max(logits, axis=0, keepdims=True)
    i1 = jnp.min(jnp.where(logits == m1, eid, N_EXPERTS), axis=0, keepdims=True)
    rest = jnp.where(eid == i1, -jnp.inf, logits)
    m2 = jnp.max(rest, axis=0, keepdims=True)
    i2 = jnp.min(jnp.where(rest == m2, eid, N_EXPERTS), axis=0, keepdims=True)
    w1 = 1.0 / (1.0 + jnp.exp(m2 - m1))
    rows = lax.broadcasted_iota(jnp.int32, logits.shape, 0)
    out = jnp.where(rows == 0, i1.astype(F32), jnp.where(rows == 1, i2.astype(F32),
                    jnp.where(rows == 2, w1, jnp.where(rows == 3, 1.0 - w1, 0.0))))
    r_ref[0] = out


def _router(xa, mod_i, g2, wr_t, n_lat_tiles, n_samples):
    b, t, d = xa.shape
    tm = TOK_TILE
    return pl.pallas_call(
        _router_body,
        grid=(b, t // tm),
        in_specs=[
            _tok_spec(d, tm), _mod_spec(d, n_lat_tiles, n_samples), _const_spec((1, d)),
            _const_spec((N_EXPERTS, d)),
        ],
        out_specs=[_tok_spec(d, tm), pl.BlockSpec((1, N_EXPERTS, tm), lambda bi, j: (bi, 0, j))],
        out_shape=[jax.ShapeDtypeStruct((b, t, d), F32), jax.ShapeDtypeStruct((b, N_EXPERTS, t), F32)],
        compiler_params=_params("parallel", "parallel"),
        name="router",
    )(xa, mod_i, g2, wr_t)


def _row_copy(src_hbm, dst, sem, src_row, dst_tile, dst_sub):
    out = dst.at[pl.ds(pl.multiple_of(dst_tile * SUBLANES, SUBLANES), SUBLANES), :].at[pl.ds(dst_sub, 1), :]
    return pltpu.make_async_copy(src_hbm.at[pl.ds(src_row, 1), :], out, sem)


def _gather_rows(src_hbm, dst, sem, idx):
    tiles_per_row = LANES // SUBLANES
    for hi in range(dst.shape[0] // LANES):
        def start(g, c, hi=hi):
            for u in range(SUBLANES):
                _row_copy(src_hbm, dst, sem, idx(hi, g * SUBLANES + u), hi * tiles_per_row + g, u).start(priority=u % 2)
            return c
        lax.fori_loop(0, tiles_per_row, start, 0)


def _wait_rows(src_hbm, dst, sem):
    pltpu.make_async_copy(src_hbm.at[pl.ds(0, dst.shape[0]), :], dst, sem).wait()


def _expert_body(te_ref, nu_ref, rt_ref, rtn_ref, h_hbm, wg_ref, wu_ref, wd_ref, o_ref, xbuf, xbf, sem, *, chunk):
    t = pl.program_id(0)
    f = pl.program_id(1)
    last_f = pl.num_programs(1) - 1
    n_used = nu_ref[0]
    used = t < n_used
    slot = t % 2

    @pl.when(jnp.logical_and(f == 0, jnp.logical_and(used, t == 0)))
    def _first_gather():
        _gather_rows(h_hbm, xbuf.at[0], sem.at[0], lambda hi, lo: rt_ref[0, hi, lo])

    @pl.when(f == 0)
    def _zero():
        o_ref[...] = jnp.zeros_like(o_ref)

    @pl.when(jnp.logical_and(used, f == 0))
    def _stage():
        _wait_rows(h_hbm, xbuf.at[slot], sem.at[slot])
        xbf[...] = xbuf[slot].astype(BF16)

    @pl.when(used)
    def _compute():
        xb = xbf[...]
        a = _dot(xb, wg_ref[0, 0].astype(BF16))
        b = _dot(xb, wu_ref[0, 0].astype(BF16))
        for u in range(chunk):
            _row_copy(h_hbm, xbuf.at[1 - slot], sem.at[1 - slot], rtn_ref[0, f, u],
                      f * (chunk // SUBLANES) + u // SUBLANES, u % SUBLANES).start(priority=u % 2)
        mid = (a * jax.nn.sigmoid(a) * b).astype(BF16)
        o_ref[...] += _dot(mid, wd_ref[0, 0].astype(BF16))

    @pl.when(jnp.logical_and(f == last_f, t == n_used - 1))
    def _drain():
        _wait_rows(h_hbm, xbuf.at[1 - slot], sem.at[1 - slot])


def _experts(tile_expert, n_used, row_token, h_flat, wg, wu, wd, layer):
    n_rows = row_token.shape[0]
    d = h_flat.shape[1]
    ff = wg.shape[3]
    nt = n_rows // ROW_TILE
    rt3 = row_token.reshape(nt, ROW_TILE // LANES, LANES)
    grid_spec = pltpu.PrefetchScalarGridSpec(
        num_scalar_prefetch=2,
        grid=(nt, ff // FF_TILE),
        in_specs=[
            pl.BlockSpec((1, ROW_TILE // LANES, LANES), lambda t, f, te, nu: (t, 0, 0), memory_space=pltpu.SMEM),
            pl.BlockSpec((1, ROW_TILE // LANES, LANES), lambda t, f, te, nu: (jnp.minimum(t + 1, nt - 1), 0, 0),
                         memory_space=pltpu.SMEM),
            pl.BlockSpec(memory_space=pl.ANY),
            pl.BlockSpec((1, 1, d, FF_TILE), lambda t, f, te, nu: (layer, te[t], 0, f)),
            pl.BlockSpec((1, 1, d, FF_TILE), lambda t, f, te, nu: (layer, te[t], 0, f)),
            pl.BlockSpec((1, 1, FF_TILE, d), lambda t, f, te, nu: (layer, te[t], f, 0)),
        ],
        out_specs=pl.BlockSpec((ROW_TILE, d), lambda t, f, te, nu: (t, 0)),
        scratch_shapes=[
            pltpu.VMEM((2, ROW_TILE, d), F32),
            pltpu.VMEM((ROW_TILE, d), BF16),
            pltpu.SemaphoreType.DMA((2,)),
        ],
    )
    assert ROW_TILE == LANES * (ff // FF_TILE)
    return pl.pallas_call(
        functools.partial(_expert_body, chunk=ROW_TILE // (ff // FF_TILE)),
        grid_spec=grid_spec,
        out_shape=jax.ShapeDtypeStruct((n_rows, d), F32),
        compiler_params=_params("arbitrary", "arbitrary"),
        name="experts",
    )(tile_expert, n_used, rt3, rt3, h_flat, wg, wu, wd)


def _combine_body(pos_ref, posn_ref, r_ref, x_ref, mod_ref, ye_hbm, o_ref, buf, sem):
    tm = x_ref.shape[1]
    step = pl.program_id(0) * pl.num_programs(1) + pl.program_id(1)
    n_steps = pl.num_programs(0) * pl.num_programs(1)
    slot = step % 2

    def gather(p_ref, s):
        for kk in range(2):
            _gather_rows(ye_hbm, buf.at[s, kk], sem.at[s, kk], lambda hi, lo, kk=kk: p_ref[0, kk, hi, lo])

    @pl.when(step == 0)
    def _first():
        gather(pos_ref, 0)

    @pl.when(step + 1 < n_steps)
    def _next():
        gather(posn_ref, 1 - slot)

    eye = (lax.broadcasted_iota(jnp.int32, (tm, tm), 0) == lax.broadcasted_iota(jnp.int32, (tm, tm), 1)).astype(BF16)
    r_hi, r_lo = _split_bf16(r_ref[0])
    rcol = _dot_nt(eye, r_hi) + _dot_nt(eye, r_lo)
    for kk in range(2):
        _wait_rows(ye_hbm, buf.at[slot, kk], sem.at[slot, kk])
    mix = buf[slot, 0] * rcol[:, 2:3] + buf[slot, 1] * rcol[:, 3:4]
    o_ref[0] = x_ref[0] + mod_ref[0, 5:6, :] * mix


def _combine(pos, route, xa, mod_i, ye, n_lat_tiles, n_samples):
    b, t, d = xa.shape
    tm = TOK_TILE
    nj = t // tm
    return pl.pallas_call(
        _combine_body,
        grid=(b, nj),
        in_specs=[
            pl.BlockSpec((1, 2, tm // LANES, LANES), lambda bi, j: (bi * nj + j, 0, 0, 0), memory_space=pltpu.SMEM),
            pl.BlockSpec((1, 2, tm // LANES, LANES), lambda bi, j: (jnp.minimum(bi * nj + j + 1, b * nj - 1), 0, 0, 0),
                         memory_space=pltpu.SMEM),
            pl.BlockSpec((1, N_EXPERTS, tm), lambda bi, j: (bi, 0, j)),
            _tok_spec(d, tm), _mod_spec(d, n_lat_tiles, n_samples),
            pl.BlockSpec(memory_space=pl.ANY),
        ],
        out_specs=_tok_spec(d, tm),
        out_shape=jax.ShapeDtypeStruct((b, t, d), F32),
        scratch_shapes=[pltpu.VMEM((2, 2, tm, d), F32), pltpu.SemaphoreType.DMA((2, 2))],
        compiler_params=_params("arbitrary", "arbitrary"),
        name="moe_combine",
    )(pos, pos, route, xa, mod_i, ye)


def _moe(xa, mod_i, g2, wr_t, wg, wu, wd, layer, n_lat_tiles, n_samples):
    b, t, d = xa.shape
    h, route = _router(xa, mod_i, g2, wr_t, n_lat_tiles, n_samples)
    n_tok = b * t
    expert = route[:, 0:2, :].astype(jnp.int32)
    onehot = (expert.reshape(-1)[:, None] == jnp.arange(N_EXPERTS)[None, :]).astype(jnp.int32)
    counts = jnp.sum(onehot, axis=0)
    rank = jnp.sum((jnp.cumsum(onehot, axis=0) - onehot) * onehot, axis=1)
    padded = ((counts + ROW_TILE - 1) // ROW_TILE) * ROW_TILE
    ends = jnp.cumsum(padded)
    pos = ((ends - padded)[expert.reshape(-1)] + rank).reshape(b, 2, t)
    n_rows = ((2 * n_tok + N_EXPERTS * (ROW_TILE - 1)) // ROW_TILE + 1) * ROW_TILE
    token = jnp.broadcast_to(jnp.arange(b)[:, None, None] * t + jnp.arange(t)[None, None, :], (b, 2, t))
    row_token = jnp.zeros((n_rows,), jnp.int32).at[pos.reshape(-1)].set(
        token.reshape(-1), unique_indices=True, mode="promise_in_bounds")
    tile_start = jnp.arange(n_rows // ROW_TILE) * ROW_TILE
    tile_expert = jnp.minimum(jnp.sum(tile_start[:, None] >= ends[None, :], axis=1), N_EXPERTS - 1).astype(jnp.int32)
    n_used = (ends[-1:] // ROW_TILE).astype(jnp.int32)
    ye = _experts(tile_expert, n_used, row_token, h.reshape(n_tok, d), wg, wu, wd, layer)
    nj = t // TOK_TILE
    pos_tiles = pos.reshape(b, 2, nj, TOK_TILE).transpose(0, 2, 1, 3).reshape(b * nj, 2, TOK_TILE // LANES, LANES)
    return _combine(pos_tiles, route, xa, mod_i, ye, n_lat_tiles, n_samples)


def _rope_tables(seq, ctx):
    rows = seq // GRID_W
    row = jnp.repeat(jnp.arange(rows), GRID_W).astype(F32)
    col = jnp.tile(jnp.arange(GRID_W), rows).astype(F32)
    half = DA_HEAD_DIM // 2
    inv = ROPE_BASE ** (-jnp.arange(0, half, 2, dtype=F32) / half)
    ang = jnp.concatenate([row[:, None] * inv, row[:, None] * inv, col[:, None] * inv, col[:, None] * inv], axis=1)
    sign = jnp.tile(jnp.concatenate([-jnp.ones((half // 2,), F32), jnp.ones((half // 2,), F32)]), 2)
    cos = jnp.concatenate([jnp.cos(ang), jnp.ones((ctx, DA_HEAD_DIM), F32)], axis=0)
    sin = jnp.concatenate([jnp.sin(ang) * sign, jnp.zeros((ctx, DA_HEAD_DIM), F32)], axis=0)
    return jnp.tile(cos, (1, LANES // DA_HEAD_DIM)), jnp.tile(sin, (1, LANES // DA_HEAD_DIM))


def _filter_weights(w1, b1, w2, b2, w3, freq):
    col = lambda v: v.reshape(-1, 1)
    bands = jnp.linspace(1e-4, HY_BANDS - 1, HY_BANDS, dtype=F32)
    deltas = jnp.abs(jnp.linspace(HY_MIN_DECAY, HY_MAX_DECAY, HY_WIDTH, dtype=F32))
    half = HY_ORDER * HY_WIDTH
    return (w1[0:1].T, w1[1:1 + HY_BANDS].T, w1[1 + HY_BANDS:].T, col(b1), w2.T, col(b2), col(freq),
            w3[:, :half].T, w3[:, half:].T, col(bands), col(jnp.tile(deltas, HY_ORDER)))


def kernel(x, c, ctx, c_ctx, w_mod, b_mod, norm1_g, norm2_g, w_in, hy_conv_w, hy_conv_b, hy_f_w1, hy_f_b1, hy_f_w2, hy_f_b2, hy_f_w3, hy_f_freq, hy_skip, q_norm_g, k_norm_g, lam_q1, lam_k1, lam_q2, lam_k2, subln_g, sg_norm_g, sg_w, sg_b, w_branch, w_out, ffn_wg, ffn_wu, ffn_wd, router_w, moe_wg, moe_wu, moe_wd):
    n_samples, seq, d = x.shape
    n_ctx = ctx.shape[1]
    depth = w_mod.shape[0]
    assert seq % CONV_TILE == 0 and n_ctx % CONV_TILE == 0 and seq % GRID_W == 0 and seq % n_ctx == 0
    assert n_samples + 1 <= MOD_ROWS
    n_lat_tiles = seq // TOK_TILE
    tot = seq + n_ctx

    xa = jnp.concatenate([x, ctx], axis=1)
    cond = jnp.zeros((MOD_ROWS, d), F32).at[:n_samples].set(c).at[n_samples].set(c_ctx)
    mod = _modvec(cond, w_mod, b_mod).reshape(depth, MOD_ROWS, 6, d)

    cos_t, sin_t = _rope_tables(seq, n_ctx)
    group = jnp.arange(QK_COLS) // DA_HEAD_DIM
    gmat = (group[:, None] == group[None, :]).astype(BF16)
    w_in_b = w_in.astype(BF16)
    w_br_b = w_branch.astype(BF16)
    w_out_b = w_out.astype(BF16)
    ffn_b = [w.astype(BF16) for w in (ffn_wg, ffn_wu, ffn_wd)]

    for i in range(depth):
        lam_init = 0.8 - 0.6 * math.exp(-0.3 * i)
        g1 = norm1_g[i].reshape(1, d)
        g2 = norm2_g[i].reshape(1, d)
        hy, q, k, v, y_sg = _inproj(
            xa, mod[i], g1, w_in_b, i, cos_t, sin_t,
            jnp.tile(q_norm_g[i], QK_COLS // DA_HEAD_DIM).reshape(1, QK_COLS),
            jnp.tile(k_norm_g[i], QK_COLS // DA_HEAD_DIM).reshape(1, QK_COLS),
            gmat, sg_norm_g[i].reshape(1, SG_WIDTH), sg_w[i].astype(BF16),
            jnp.repeat(sg_b[i].T, SG_WIDTH // SG_GROUPS, axis=1), n_lat_tiles, n_samples)

        fw = _filter_weights(hy_f_w1[i], hy_f_b1[i], hy_f_w2[i], hy_f_b2[i], hy_f_w3[i], hy_f_freq[i])
        hm_l = _hyfilter(fw, seq).reshape(HY_ORDER, HY_WIDTH, 2 * seq // LANES, LANES)
        hm_c = _hyfilter(fw, n_ctx).reshape(HY_ORDER, HY_WIDTH, 2 * n_ctx // LANES, LANES)
        cw = hy_conv_w[i].reshape(3, 3, HY_WIDTH)
        cbias = hy_conv_b[i].reshape(3, HY_WIDTH)
        sc_rows = [cw.reshape(9, HY_WIDTH), cbias, hy_skip[i]]
        sc_rows.append(jnp.zeros((HY_SCALARS - 12 - HY_ORDER, HY_WIDTH), F32))
        sc_tab = jnp.concatenate(sc_rows, axis=0).T.reshape(-1)
        y_hy = _hyconv(sc_tab, jnp.transpose(hy, (2, 0, 1)), hm_l, hm_c, seq, n_ctx)
        y_hy = jnp.transpose(y_hy, (1, 2, 0)).astype(BF16)

        lam_p = jnp.stack([lam_q1[i], lam_k1[i], lam_q2[i], lam_k2[i]], axis=0)
        y_da = _attention(lam_p, q, k, v, subln_g[i].reshape(1, DA_V_DIM), lam_init, seq)

        xa = _merge(xa, mod[i], g1, w_in_b, i, y_hy, y_da, y_sg, w_br_b, w_out_b,
                    seq if i == depth - 1 else tot, n_lat_tiles, n_samples)

        j = i // 2
        if i % 2 == 0:
            xa = _ffn_dense(xa, mod[i], g2, *ffn_b, j, n_lat_tiles, n_samples)
        else:
            xa = _moe(xa, mod[i], g2, router_w[j].T, moe_wg, moe_wu, moe_wd, j, n_lat_tiles, n_samples)
    return xa[:, :seq]
```

```python
import functools
import math

import jax
import jax.numpy as jnp
from jax import lax
from jax.experimental import pallas as pl
from jax.experimental.pallas import tpu as pltpu

F32 = jnp.float32
BF16 = jnp.bfloat16
EPS = 1e-6

GRID_W = 64
HY_WIDTH = 512
HY_ORDER = 2
HY_BANDS = 16
HY_FFN = 64
HY_MAX_DECAY = math.log(1e-2) / 0.3
HY_MIN_DECAY = math.log(1e-2) / 1.5
DA_HEADS = 4
DA_HEAD_DIM = 64
DA_V_DIM = 2 * DA_HEAD_DIM
ROPE_BASE = 10000.0
SG_WIDTH = 512
SG_GROUPS = 4
SG_CHUNK = 128
N_EXPERTS = 8
HY_COLS = 3 * HY_WIDTH
QK_COLS = DA_HEADS * 2 * DA_HEAD_DIM
DA_COLS = 2 * QK_COLS + DA_HEADS * DA_V_DIM
SG_COLS = 2 * SG_WIDTH
PRE_COLS = HY_COLS + DA_COLS + SG_COLS

LANES = 128
SUBLANES = 8
TOK_TILE = 256
CONV_TILE = 256
HY_CHANNELS_PER_STEP = 16
HY_SCALARS = 16
ROW_TILE = 896
FF_TILE = 512
MOD_ROWS = 16
VMEM_LIMIT = 56 * 1024 * 1024


def _dot(a, b):
    return jnp.dot(a, b, preferred_element_type=F32)


def _dot_nt(a, b):
    return lax.dot_general(a, b, (((1,), (1,)), ((), ())), preferred_element_type=F32)


_hdot = functools.partial(jnp.dot, precision=lax.Precision.HIGHEST, preferred_element_type=F32)


def _split_bf16(a):
    hi = a.astype(BF16)
    return hi, (a - hi.astype(F32)).astype(BF16)


def _dot3(a, b):
    a_hi, a_lo = _split_bf16(a)
    b_hi, b_lo = _split_bf16(b)
    return _dot(a_hi, b_hi) + _dot(a_lo, b_hi) + _dot(a_hi, b_lo)


def _norm_mod(x, g, shift, scale):
    ms = jnp.mean(x * x, axis=-1, keepdims=True)
    return (x * lax.rsqrt(ms + EPS) * g) * (1.0 + scale) + shift


def _params(*sem):
    return pltpu.CompilerParams(dimension_semantics=sem, vmem_limit_bytes=VMEM_LIMIT)


def _modvec_body(cond_ref, w_ref, b_ref, o_ref):
    cnd = cond_ref[...]
    s = cnd * jax.nn.sigmoid(cnd)
    s_hi, s_lo = _split_bf16(s)
    w_hi, w_lo = _split_bf16(w_ref[0])
    o_ref[0] = _dot(s_hi, w_hi) + _dot(s_lo, w_hi) + _dot(s_hi, w_lo) + b_ref[0]


def _modvec(cond, w_mod, b_mod):
    depth, d, n = w_mod.shape
    tn = n // 4
    return pl.pallas_call(
        _modvec_body,
        grid=(depth, n // tn),
        in_specs=[
            pl.BlockSpec((MOD_ROWS, d), lambda i, j: (0, 0)),
            pl.BlockSpec((1, d, tn), lambda i, j: (i, 0, j)),
            pl.BlockSpec((1, 1, tn), lambda i, j: (i, 0, j)),
        ],
        out_specs=pl.BlockSpec((1, MOD_ROWS, tn), lambda i, j: (i, 0, j)),
        out_shape=jax.ShapeDtypeStruct((depth, MOD_ROWS, n), F32),
        compiler_params=_params("parallel", "parallel"),
        name="modvec",
    )(cond, w_mod, b_mod.reshape(depth, 1, n))


def _tok_spec(width, tm):
    return pl.BlockSpec((1, tm, width), lambda b, j: (b, j, 0))


def _const_spec(shape):
    nd = len(shape)
    return pl.BlockSpec(shape, lambda b, j: (0,) * nd)


def _mod_spec(d, n_lat_tiles, n_samples):
    return pl.BlockSpec((1, 6, d), lambda b, j: (jnp.where(j < n_lat_tiles, b, n_samples), 0, 0))


def _qk_norm_rope(a, g_tile, gmat, cos, sin_signed, first_half):
    ss = _dot((a * a).astype(BF16), gmat)
    an = a * lax.rsqrt(ss * (1.0 / DA_HEAD_DIM) + EPS) * g_tile
    outs = []
    for ci in range(QK_COLS // LANES):
        ch = an[:, ci * LANES:(ci + 1) * LANES]
        partner = jnp.where(first_half, pltpu.roll(ch, LANES - 16, 1), pltpu.roll(ch, 16, 1))
        outs.append(ch * cos + partner * sin_signed)
    return jnp.concatenate(outs, axis=1)


def _inproj_body(x_ref, mod_ref, g1_ref, w_ref, cos_ref, sin_ref, qg_ref, kg_ref, gmat_ref,
                 sgg_ref, sgw_ref, sgb_ref, hy_ref, q_ref, k_ref, v_ref, sg_ref):
    tm = x_ref.shape[1]
    h = _norm_mod(x_ref[0], g1_ref[...], mod_ref[0, 0:1, :], mod_ref[0, 1:2, :]).astype(BF16)
    hy_ref[0] = _dot(h, w_ref[0, :, 0:HY_COLS])

    lane = lax.broadcasted_iota(jnp.int32, (1, LANES), 1)
    first_half = (lane % 32) < 16
    cos = cos_ref[...]
    sin_signed = sin_ref[...]
    gmat = gmat_ref[...]
    o = HY_COLS
    q = _qk_norm_rope(_dot(h, w_ref[0, :, o:o + QK_COLS]), qg_ref[...], gmat, cos, sin_signed, first_half)
    q_ref[0] = (q * (DA_HEAD_DIM ** -0.5 * math.log2(math.e))).astype(BF16)
    o += QK_COLS
    k = _qk_norm_rope(_dot(h, w_ref[0, :, o:o + QK_COLS]), kg_ref[...], gmat, cos, sin_signed, first_half)
    k_ref[0] = k.astype(BF16)
    o += QK_COLS
    v_ref[0] = _dot(h, w_ref[0, :, o:o + DA_HEADS * DA_V_DIM]).astype(BF16)
    o += DA_HEADS * DA_V_DIM

    z = jax.nn.gelu(_dot(h, w_ref[0, :, o:o + SG_COLS]))
    u = z[:, :SG_WIDTH]
    vv = z[:, SG_WIDTH:]
    vn = (vv * lax.rsqrt(jnp.mean(vv * vv, axis=-1, keepdims=True) + EPS) * sgg_ref[...]).astype(BF16)
    gw = SG_WIDTH // SG_GROUPS
    for ch in range(tm // SG_CHUNK):
        r0 = ch * SG_CHUNK
        cols = []
        for g in range(SG_GROUPS):
            s = _dot(sgw_ref[g], vn[r0:r0 + SG_CHUNK, g * gw:(g + 1) * gw]) + sgb_ref[:, g * gw:(g + 1) * gw]
            cols.append(u[r0:r0 + SG_CHUNK, g * gw:(g + 1) * gw] * s)
        sg_ref[0, r0:r0 + SG_CHUNK, :] = jnp.concatenate(cols, axis=1).astype(BF16)


def _inproj(xa, mod_i, g1, w_in, layer, cos_t, sin_t, qg, kg, gmat, sgg, sgw, sgb, n_lat_tiles, n_samples):
    b, t, d = xa.shape
    tm = TOK_TILE
    outs = [jax.ShapeDtypeStruct((b, t, HY_COLS), F32)] + [jax.ShapeDtypeStruct((b, t, QK_COLS), BF16)] * 4
    return pl.pallas_call(
        _inproj_body,
        grid=(b, t // tm),
        in_specs=[
            _tok_spec(d, tm),
            _mod_spec(d, n_lat_tiles, n_samples),
            _const_spec((1, d)),
            pl.BlockSpec((1, d, PRE_COLS), lambda bi, j: (layer, 0, 0)),
            pl.BlockSpec((tm, LANES), lambda bi, j: (j, 0)),
            pl.BlockSpec((tm, LANES), lambda bi, j: (j, 0)),
            _const_spec((1, QK_COLS)),
            _const_spec((1, QK_COLS)),
            _const_spec((QK_COLS, QK_COLS)),
            _const_spec((1, SG_WIDTH)),
            _const_spec((SG_GROUPS, SG_CHUNK, SG_CHUNK)),
            _const_spec((SG_CHUNK, SG_WIDTH)),
        ],
        out_specs=[_tok_spec(HY_COLS, tm)] + [_tok_spec(QK_COLS, tm)] * 4,
        out_shape=outs,
        compiler_params=_params("parallel", "parallel"),
        name="inproj",
    )(xa, mod_i, g1, w_in, cos_t, sin_t, qg, kg, gmat, sgg, sgw, sgb)


def _hyfilter_body(w1t_ref, w1c_ref, w1s_ref, b1_ref, w2_ref, b2_ref, fr_ref, w3f_ref, w3b_ref,
                   bands_ref, dl_ref, o_ref, *, seq):
    n = 2 * seq
    xi = lax.broadcasted_iota(jnp.int32, (1, n), 1)
    lag = xi - (seq - 1)
    pos = jnp.abs(lag).astype(F32)
    t = pos / (seq - 1)
    ang = 2.0 * math.pi * pos * bands_ref[...] / seq
    fr = fr_ref[...]
    z1 = w1t_ref[...] * t + _hdot(w1c_ref[...], jnp.cos(ang)) + _hdot(w1s_ref[...], jnp.sin(ang)) + b1_ref[...]
    h1 = jnp.sin(fr * z1)
    h2 = jnp.sin(fr * (_hdot(w2_ref[...], h1) + b2_ref[...]))
    bwd = _dot3(w3b_ref[...], h2[:, :seq])
    fwd = _dot3(w3f_ref[...], h2[:, seq:])
    fwd0 = _dot3(w3f_ref[...], h2[:, seq - LANES:seq])
    at_zero = jnp.where(lag[:, seq - LANES:seq] == 0, fwd0, 0.0)
    k = jnp.concatenate([bwd[:, :seq - LANES], bwd[:, seq - LANES:] + at_zero, fwd], axis=1)
    k = jnp.where(xi < n - 1, k * jnp.exp(-t * dl_ref[...]), 0.0)
    o_ref[...] = k / jnp.sum(jnp.abs(k), axis=-1, keepdims=True)


def _hyfilter(fw, seq):
    rows = HY_ORDER * HY_WIDTH
    rb = 256
    n = 2 * seq
    small = lambda shape: pl.BlockSpec(shape, lambda i: (0, 0))
    return pl.pallas_call(
        functools.partial(_hyfilter_body, seq=seq),
        grid=(rows // rb,),
        in_specs=[
            small((HY_FFN, 1)), small((HY_FFN, HY_BANDS)), small((HY_FFN, HY_BANDS)), small((HY_FFN, 1)),
            small((HY_FFN, HY_FFN)), small((HY_FFN, 1)), small((HY_FFN, 1)),
            pl.BlockSpec((rb, HY_FFN), lambda i: (i, 0)),
            pl.BlockSpec((rb, HY_FFN), lambda i: (i, 0)),
            small((HY_BANDS, 1)),
            pl.BlockSpec((rb, 1), lambda i: (i, 0)),
        ],
        out_specs=pl.BlockSpec((rb, n), lambda i: (i, 0)),
        out_shape=jax.ShapeDtypeStruct((rows, n), F32),
        compiler_params=_params("parallel"),
        name="hyfilter",
    )(*fw)


def _hyconv_body(sc_ref, z_ref, x1_ref, x2_ref, hml_ref, hmc_ref, o_ref, big_l, big_c, *, seq, ctx, cb):
    tot = seq + ctx
    lane = lax.broadcasted_iota(jnp.int32, (1, tot), 1)
    has_prev = jnp.logical_and(lane != 0, lane != seq)
    has_next = jnp.logical_and(lane != seq - 1, lane != tot - 1)
    row = lax.broadcasted_iota(jnp.int32, (LANES, LANES), 0)
    col = lax.broadcasted_iota(jnp.int32, (LANES, LANES), 1)
    lower = col <= row
    c0 = pl.program_id(0) * cb

    def short_conv(p, base, part):
        prev = jnp.where(has_prev, pltpu.roll(p, 1, 1), 0.0)
        nxt = jnp.where(has_next, pltpu.roll(p, tot - 1, 1), 0.0)
        return (sc_ref[base + 9 + part] + sc_ref[base + part] * prev
                + sc_ref[base + 3 + part] * p + sc_ref[base + 6 + part] * nxt)

    def build(hm_ref, o, ci, big_ref):
        n_rows = hm_ref.shape[2]
        prev = None
        for rp in range(n_rows - 1, -1, -1):
            r = hm_ref[o, ci, rp:rp + 1, :]
            cur = pltpu.roll(jnp.broadcast_to(r, (LANES, LANES)), 1, 1, stride=1, stride_axis=0)
            if prev is not None:
                rho = n_rows - 2 - rp
                big_ref[o, rho * LANES:(rho + 1) * LANES, :] = jnp.where(lower, cur, prev).astype(BF16)
            prev = cur

    def long_conv(zz, big_ref, o, length, base):
        nblk = length // CONV_TILE
        mid = (2 * nblk - 1) * LANES
        ys = [None] * nblk
        for dd in range(-(nblk - 1), nblk):
            r0 = mid - CONV_TILE * dd
            w = jnp.concatenate([big_ref[o, r0:r0 + CONV_TILE, :],
                                 big_ref[o, r0 - LANES:r0 - LANES + CONV_TILE, :]], axis=1)
            js = list(range(max(0, -dd), min(nblk, nblk - dd)))
            lhs = jnp.concatenate([zz[:, base + j * CONV_TILE:base + (j + 1) * CONV_TILE] for j in js], axis=0)
            out = _dot(lhs.astype(BF16), w)
            nb = zz.shape[0]
            for kk, j in enumerate(js):
                piece = out[kk * nb:(kk + 1) * nb]
                ys[j + dd] = piece if ys[j + dd] is None else ys[j + dd] + piece
        return ys

    def chan(ci, carry):
        base = (c0 + ci) * HY_SCALARS
        z = short_conv(z_ref[ci], base, 0)
        gates = (short_conv(x1_ref[ci], base, 1), short_conv(x2_ref[ci], base, 2))
        for o in range(HY_ORDER):
            build(hml_ref, o, ci, big_l)
            build(hmc_ref, o, ci, big_c)
        for o in range(HY_ORDER):
            y = jnp.concatenate(long_conv(z, big_l, o, seq, 0) + long_conv(z, big_c, o, ctx, seq), axis=1)
            z = gates[o] * (y + sc_ref[base + 12 + o] * z)
        o_ref[ci] = z
        return carry

    lax.fori_loop(0, cb, chan, 0, unroll=16)


def _hyconv(sc_tab, u3, hm_l, hm_c, seq, ctx):
    _, b, tot = u3.shape
    cb = HY_CHANNELS_PER_STEP
    nblk = HY_WIDTH // cb
    slab = lambda part: pl.BlockSpec((cb, b, tot), lambda i: (part * nblk + i, 0, 0))
    return pl.pallas_call(
        functools.partial(_hyconv_body, seq=seq, ctx=ctx, cb=cb),
        grid=(nblk,),
        in_specs=[
            pl.BlockSpec(memory_space=pltpu.SMEM),
            slab(0), slab(1), slab(2),
            pl.BlockSpec((HY_ORDER, cb, hm_l.shape[2], LANES), lambda i: (0, i, 0, 0)),
            pl.BlockSpec((HY_ORDER, cb, hm_c.shape[2], LANES), lambda i: (0, i, 0, 0)),
        ],
        out_specs=pl.BlockSpec((cb, b, tot), lambda i: (i, 0, 0)),
        out_shape=jax.ShapeDtypeStruct((HY_WIDTH, b, tot), F32),
        scratch_shapes=[
            pltpu.VMEM((HY_ORDER, (hm_l.shape[2] - 1) * LANES, LANES), BF16),
            pltpu.VMEM((HY_ORDER, (hm_c.shape[2] - 1) * LANES, LANES), BF16),
        ],
        compiler_params=_params("parallel"),
        name="hyconv",
    )(sc_tab, u3, u3, u3, hm_l, hm_c)


def _zero_after(x):
    bits = lax.bitcast_convert_type(x[-8:, -LANES:], jnp.uint32)
    z = lax.shift_right_logical(lax.shift_right_logical(bits, jnp.uint32(16)), jnp.uint32(16))
    return z[0:1].astype(F32).astype(BF16)


def _attn_body(lam_ref, q_ref, k_ref, v_ref, sub_ref, o_ref, *, lam_init, heads):
    hw = 2 * DA_HEAD_DIM
    lane = lax.broadcasted_iota(jnp.int32, (1, hw), 1)
    first = lane < DA_HEAD_DIM
    lp = lam_ref[...]
    lam = (jnp.exp(jnp.sum(lp[0:1] * lp[1:2], keepdims=True))
           - jnp.exp(jnp.sum(lp[2:3] * lp[3:4], keepdims=True)) + lam_init)
    sub = sub_ref[...] * (1.0 - lam_init)

    def scores(h, after):
        q = q_ref[0, :, h * hw:(h + 1) * hw]
        if after is not None:
            q = q + after
        k = k_ref[0, :, h * hw:(h + 1) * hw]
        zero = jnp.zeros_like(q)
        return _dot_nt(jnp.where(first, q, zero), k), _dot_nt(jnp.where(first, zero, q), k)

    def weights(s1, s2):
        p1 = jnp.exp2(s1 - jnp.max(s1, axis=-1, keepdims=True))
        p2 = jnp.exp2(s2 - jnp.max(s2, axis=-1, keepdims=True))
        l1 = jnp.sum(p1, axis=-1, keepdims=True)
        l2 = jnp.sum(p2, axis=-1, keepdims=True)
        return (p1 - p2 * (lam * l1 / l2)).astype(BF16), 1.0 / l1

    def values(h, w):
        o = _dot(w[0], v_ref[0, :, h * hw:(h + 1) * hw]) * w[1]
        on = o * lax.rsqrt(jnp.mean(o * o, axis=-1, keepdims=True) + EPS) * sub
        o_ref[0, :, h * hw:(h + 1) * hw] = on.astype(BF16)

    s = [None] * heads
    w = [None] * heads
    after = None
    for step in range(heads + 2):
        if step < heads:
            s[step] = scores(step, after)
            after = _zero_after(s[step][1])
        if 0 <= step - 1 < heads:
            w[step - 1] = weights(*s[step - 1])
            s[step - 1] = None
        if 0 <= step - 2 < heads:
            values(step - 2, w[step - 2])


def _attention(lam_p, q, k, v, sub, lam_init, seq):
    b, tot, _ = q.shape
    ctx = tot - seq
    tq = TOK_TILE
    hw = 2 * DA_HEAD_DIM
    width = DA_HEADS * hw

    def call(q0, n_q, keys, k0, name):
        return pl.pallas_call(
            functools.partial(_attn_body, lam_init=lam_init, heads=DA_HEADS),
            grid=(b, n_q),
            in_specs=[
                pl.BlockSpec((4, DA_HEAD_DIM), lambda bi, j: (0, 0)),
                pl.BlockSpec((1, tq, width), lambda bi, j: (bi, q0 + j, 0)),
                pl.BlockSpec((1, keys, width), lambda bi, j: (bi, k0, 0)),
                pl.BlockSpec((1, keys, width), lambda bi, j: (bi, k0, 0)),
                pl.BlockSpec((1, hw), lambda bi, j: (0, 0)),
            ],
            out_specs=pl.BlockSpec((1, tq, width), lambda bi, j: (bi, j, 0)),
            out_shape=jax.ShapeDtypeStruct((b, n_q * tq, width), BF16),
            compiler_params=_params("parallel", "parallel"),
            name=name,
        )(lam_p, q, k, v, sub)

    return call(0, seq // tq, tot, 0, "diffattn"), call(seq // tq, ctx // tq, ctx, seq // ctx, "diffattn_ctx")


def _merge_body(x_ref, mod_ref, g1_ref, wg0_ref, wg1_ref, wg2_ref, yh_ref, ydl_ref, ydc_ref, ys_ref, wb_ref, wo_ref,
                o_ref, *, n_lat_tiles):
    x = x_ref[0]
    h = _norm_mod(x, g1_ref[...], mod_ref[0, 0:1, :], mod_ref[0, 1:2, :]).astype(BF16)
    y_da = jnp.where(pl.program_id(1) < n_lat_tiles, ydl_ref[0], ydc_ref[0])
    acc = None
    for n, (wg_ref, y) in enumerate(((wg0_ref, yh_ref[0]), (wg1_ref, y_da), (wg2_ref, ys_ref[0]))):
        gate = jax.nn.sigmoid(_dot(h, wg_ref[0]))
        term = gate * _dot(y, wb_ref[0, n])
        acc = term if acc is None else acc + term
    o_ref[0] = x + mod_ref[0, 2:3, :] * _dot(acc.astype(BF16), wo_ref[0])


def _merge(xa, mod_i, g1, w_in, layer, y_hy, y_da, y_sg, w_br, w_out, rows, n_lat_tiles, n_samples):
    b, _, d = xa.shape
    t = rows
    tm = TOK_TILE
    assert PRE_COLS % d == 0
    gate_spec = lambda n: pl.BlockSpec((1, d, d), lambda bi, j: (layer, 0, PRE_COLS // d + n))
    y_lat, y_ctx = y_da
    n_ctx_tiles = y_ctx.shape[1] // tm
    return pl.pallas_call(
        functools.partial(_merge_body, n_lat_tiles=n_lat_tiles),
        grid=(b, t // tm),
        in_specs=[
            _tok_spec(d, tm), _mod_spec(d, n_lat_tiles, n_samples), _const_spec((1, d)),
            gate_spec(0), gate_spec(1), gate_spec(2),
            _tok_spec(HY_WIDTH, tm),
            pl.BlockSpec((1, tm, HY_WIDTH), lambda bi, j: (bi, jnp.minimum(j, n_lat_tiles - 1), 0)),
            pl.BlockSpec((1, tm, HY_WIDTH), lambda bi, j: (bi, jnp.clip(j - n_lat_tiles, 0, n_ctx_tiles - 1), 0)),
            _tok_spec(HY_WIDTH, tm),
            pl.BlockSpec((1, 3, HY_WIDTH, d), lambda bi, j: (layer, 0, 0, 0)),
            pl.BlockSpec((1, d, d), lambda bi, j: (layer, 0, 0)),
        ],
        out_specs=_tok_spec(d, tm),
        out_shape=jax.ShapeDtypeStruct((b, t, d), F32),
        compiler_params=_params("parallel", "parallel"),
        name="merge",
    )(xa, mod_i, g1, w_in, w_in, w_in, y_hy, y_lat, y_ctx, y_sg, w_br, w_out)


def _ffn_body(x_ref, mod_ref, g2_ref, wg_ref, wu_ref, wd_ref, o_ref, *, chunk):
    x = x_ref[0]
    h = _norm_mod(x, g2_ref[...], mod_ref[0, 3:4, :], mod_ref[0, 4:5, :]).astype(BF16)
    ff = wg_ref.shape[2]
    acc = None
    for f0 in range(0, ff, chunk):
        f1 = min(ff, f0 + chunk)
        a = _dot(h, wg_ref[0, :, f0:f1])
        mid = (a * jax.nn.sigmoid(a) * _dot(h, wu_ref[0, :, f0:f1])).astype(BF16)
        term = _dot(mid, wd_ref[0, f0:f1, :])
        acc = term if acc is None else acc + term
    o_ref[0] = x + mod_ref[0, 5:6, :] * acc


def _ffn_dense(xa, mod_i, g2, wg, wu, wd, layer, n_lat_tiles, n_samples):
    b, t, d = xa.shape
    ff = wg.shape[2]
    tm = TOK_TILE
    return pl.pallas_call(
        functools.partial(_ffn_body, chunk=1024),
        grid=(b, t // tm),
        in_specs=[
            _tok_spec(d, tm), _mod_spec(d, n_lat_tiles, n_samples), _const_spec((1, d)),
            pl.BlockSpec((1, d, ff), lambda bi, j: (layer, 0, 0)),
            pl.BlockSpec((1, d, ff), lambda bi, j: (layer, 0, 0)),
            pl.BlockSpec((1, ff, d), lambda bi, j: (layer, 0, 0)),
        ],
        out_specs=_tok_spec(d, tm),
        out_shape=jax.ShapeDtypeStruct((b, t, d), F32),
        compiler_params=_params("parallel", "parallel"),
        name="ffn_dense",
    )(xa, mod_i, g2, wg, wu, wd)


def _router_body(x_ref, mod_ref, g2_ref, wr_ref, h_ref, r_ref):
    h = _norm_mod(x_ref[0], g2_ref[...], mod_ref[0, 3:4, :], mod_ref[0, 4:5, :])
    h_ref[0] = h
    h_hi, h_lo = _split_bf16(h)
    w_hi, w_lo = _split_bf16(wr_ref[...])
    logits = _dot_nt(w_hi, h_hi) + _dot_nt(w_hi, h_lo) + _dot_nt(w_lo, h_hi)
    eid = lax.broadcasted_iota(jnp.int32, logits.shape, 0)
    m1 = jnp.max(logits, axis=0, keepdims=True)
    i1 = jnp.min(jnp.where(logits == m1, eid, N_EXPERTS), axis=0, keepdims=True)
    rest = jnp.where(eid == i1, -jnp.inf, logits)
    m2 = jnp.max(rest, axis=0, keepdims=True)
    i2 = jnp.min(jnp.where(rest == m2, eid, N_EXPERTS), axis=0, keepdims=True)
    w1 = 1.0 / (1.0 + jnp.exp(m2 - m1))
    rows = lax.broadcasted_iota(jnp.int32, logits.shape, 0)
    out = jnp.where(rows == 0, i1.astype(F32), jnp.where(rows == 1, i2.astype(F32),
                    jnp.where(rows == 2, w1, jnp.where(rows == 3, 1.0 - w1, 0.0))))
    r_ref[0] = out


def _router(xa, mod_i, g2, wr_t, n_lat_tiles, n_samples):
    b, t, d = xa.shape
    tm = TOK_TILE
    return pl.pallas_call(
        _router_body,
        grid=(b, t // tm),
        in_specs=[
            _tok_spec(d, tm), _mod_spec(d, n_lat_tiles, n_samples), _const_spec((1, d)),
            _const_spec((N_EXPERTS, d)),
        ],
        out_specs=[_tok_spec(d, tm), pl.BlockSpec((1, N_EXPERTS, tm), lambda bi, j: (bi, 0, j))],
        out_shape=[jax.ShapeDtypeStruct((b, t, d), F32), jax.ShapeDtypeStruct((b, N_EXPERTS, t), F32)],
        compiler_params=_params("parallel", "parallel"),
        name="router",
    )(xa, mod_i, g2, wr_t)


def _row_copy(src_hbm, dst, sem, src_row, dst_tile, dst_sub):
    out = dst.at[pl.ds(pl.multiple_of(dst_tile * SUBLANES, SUBLANES), SUBLANES), :].at[pl.ds(dst_sub, 1), :]
    return pltpu.make_async_copy(src_hbm.at[pl.ds(src_row, 1), :], out, sem)


def _gather_rows(src_hbm, dst, sem, idx):
    def start(g, c):
        for u in range(SUBLANES):
            _row_copy(src_hbm, dst, sem, idx(g, u), g, u).start(priority=u % 2)
        return c
    lax.fori_loop(0, dst.shape[0] // SUBLANES, start, 0)


def _wait_rows(src_hbm, dst, sem):
    pltpu.make_async_copy(src_hbm.at[pl.ds(0, dst.shape[0]), :], dst, sem).wait()


def _expert_body(te_ref, nu_ref, rt_ref, rtn_ref, h_hbm, wg_ref, wu_ref, wd_ref, o_ref, xbuf, xbf, sem, *, chunk):
    t = pl.program_id(0)
    f = pl.program_id(1)
    last_f = pl.num_programs(1) - 1
    n_used = nu_ref[0]
    used = t < n_used
    slot = t % 2

    @pl.when(jnp.logical_and(f == 0, jnp.logical_and(used, t == 0)))
    def _first_gather():
        _gather_rows(h_hbm, xbuf.at[0], sem.at[0], lambda g, u: rt_ref[0, g, u])

    @pl.when(f == 0)
    def _zero():
        o_ref[...] = jnp.zeros_like(o_ref)

    @pl.when(jnp.logical_and(used, f == 0))
    def _stage():
        _wait_rows(h_hbm, xbuf.at[slot], sem.at[slot])
        xbf[...] = xbuf[slot].astype(BF16)

    @pl.when(used)
    def _compute():
        xb = xbf[...]
        a = _dot(xb, wg_ref[0, 0].astype(BF16))
        b = _dot(xb, wu_ref[0, 0].astype(BF16))
        for u in range(chunk):
            tile = f * (chunk // SUBLANES) + u // SUBLANES
            _row_copy(h_hbm, xbuf.at[1 - slot], sem.at[1 - slot], rtn_ref[0, tile, u % SUBLANES],
                      tile, u % SUBLANES).start(priority=u % 2)
        mid = (a * jax.nn.sigmoid(a) * b).astype(BF16)
        o_ref[...] += _dot(mid, wd_ref[0, 0].astype(BF16))

    @pl.when(jnp.logical_and(f == last_f, t == n_used - 1))
    def _drain():
        _wait_rows(h_hbm, xbuf.at[1 - slot], sem.at[1 - slot])


def _experts(tile_expert, n_used, row_token, h_flat, wg, wu, wd, layer):
    n_rows = row_token.shape[0]
    d = h_flat.shape[1]
    ff = wg.shape[3]
    nt = n_rows // ROW_TILE
    rt3 = row_token.reshape(nt, ROW_TILE // SUBLANES, SUBLANES)
    grid_spec = pltpu.PrefetchScalarGridSpec(
        num_scalar_prefetch=2,
        grid=(nt, ff // FF_TILE),
        in_specs=[
            pl.BlockSpec((1, ROW_TILE // SUBLANES, SUBLANES), lambda t, f, te, nu: (t, 0, 0), memory_space=pltpu.SMEM),
            pl.BlockSpec((1, ROW_TILE // SUBLANES, SUBLANES), lambda t, f, te, nu: (jnp.minimum(t + 1, nt - 1), 0, 0),
                         memory_space=pltpu.SMEM),
            pl.BlockSpec(memory_space=pl.ANY),
            pl.BlockSpec((1, 1, d, FF_TILE), lambda t, f, te, nu: (layer, te[t], 0, f)),
            pl.BlockSpec((1, 1, d, FF_TILE), lambda t, f, te, nu: (layer, te[t], 0, f)),
            pl.BlockSpec((1, 1, FF_TILE, d), lambda t, f, te, nu: (layer, te[t], f, 0)),
        ],
        out_specs=pl.BlockSpec((ROW_TILE, d), lambda t, f, te, nu: (t, 0)),
        scratch_shapes=[
            pltpu.VMEM((2, ROW_TILE, d), F32),
            pltpu.VMEM((ROW_TILE, d), BF16),
            pltpu.SemaphoreType.DMA((2,)),
        ],
    )
    assert ROW_TILE == LANES * (ff // FF_TILE)
    return pl.pallas_call(
        functools.partial(_expert_body, chunk=ROW_TILE // (ff // FF_TILE)),
        grid_spec=grid_spec,
        out_shape=jax.ShapeDtypeStruct((n_rows, d), F32),
        compiler_params=_params("arbitrary", "arbitrary"),
        name="experts",
    )(tile_expert, n_used, rt3, rt3, h_flat, wg, wu, wd)


def _combine_body(pos_ref, posn_ref, r_ref, x_ref, mod_ref, ye_hbm, o_ref, buf, sem):
    tm = x_ref.shape[1]
    step = pl.program_id(0) * pl.num_programs(1) + pl.program_id(1)
    n_steps = pl.num_programs(0) * pl.num_programs(1)
    slot = step % 2

    def gather(p_ref, s):
        for kk in range(2):
            _gather_rows(ye_hbm, buf.at[s, kk], sem.at[s, kk], lambda g, u, kk=kk: p_ref[0, kk, g, u])

    @pl.when(step == 0)
    def _first():
        gather(pos_ref, 0)

    @pl.when(step + 1 < n_steps)
    def _next():
        gather(posn_ref, 1 - slot)

    eye = (lax.broadcasted_iota(jnp.int32, (tm, tm), 0) == lax.broadcasted_iota(jnp.int32, (tm, tm), 1)).astype(BF16)
    r_hi, r_lo = _split_bf16(r_ref[0])
    rcol = _dot_nt(eye, r_hi) + _dot_nt(eye, r_lo)
    for kk in range(2):
        _wait_rows(ye_hbm, buf.at[slot, kk], sem.at[slot, kk])
    mix = buf[slot, 0] * rcol[:, 2:3] + buf[slot, 1] * rcol[:, 3:4]
    o_ref[0] = x_ref[0] + mod_ref[0, 5:6, :] * mix


def _combine(pos, route, xa, mod_i, ye, n_lat_tiles, n_samples):
    b, t, d = xa.shape
    tm = TOK_TILE
    nj = t // tm
    return pl.pallas_call(
        _combine_body,
        grid=(b, nj),
        in_specs=[
            pl.BlockSpec((1, 2, tm // SUBLANES, SUBLANES), lambda bi, j: (bi * nj + j, 0, 0, 0), memory_space=pltpu.SMEM),
            pl.BlockSpec((1, 2, tm // SUBLANES, SUBLANES),
                         lambda bi, j: (jnp.minimum(bi * nj + j + 1, b * nj - 1), 0, 0, 0),
                         memory_space=pltpu.SMEM),
            pl.BlockSpec((1, N_EXPERTS, tm), lambda bi, j: (bi, 0, j)),
            _tok_spec(d, tm), _mod_spec(d, n_lat_tiles, n_samples),
            pl.BlockSpec(memory_space=pl.ANY),
        ],
        out_specs=_tok_spec(d, tm),
        out_shape=jax.ShapeDtypeStruct((b, t, d), F32),
        scratch_shapes=[pltpu.VMEM((2, 2, tm, d), F32), pltpu.SemaphoreType.DMA((2, 2))],
        compiler_params=_params("arbitrary", "arbitrary"),
        name="moe_combine",
    )(pos, pos, route, xa, mod_i, ye)


def _moe(xa, mod_i, g2, wr_t, wg, wu, wd, layer, n_lat_tiles, n_samples):
    b, t, d = xa.shape
    h, route = _router(xa, mod_i, g2, wr_t, n_lat_tiles, n_samples)
    n_tok = b * t
    expert = route[:, 0:2, :].astype(jnp.int32)
    onehot = (expert.reshape(-1)[:, None] == jnp.arange(N_EXPERTS)[None, :]).astype(jnp.int32)
    counts = jnp.sum(onehot, axis=0)
    rank = jnp.sum((jnp.cumsum(onehot, axis=0) - onehot) * onehot, axis=1)
    padded = ((counts + ROW_TILE - 1) // ROW_TILE) * ROW_TILE
    ends = jnp.cumsum(padded)
    pos = ((ends - padded)[expert.reshape(-1)] + rank).reshape(b, 2, t)
    n_rows = ((2 * n_tok + N_EXPERTS * (ROW_TILE - 1)) // ROW_TILE + 1) * ROW_TILE
    token = jnp.broadcast_to(jnp.arange(b)[:, None, None] * t + jnp.arange(t)[None, None, :], (b, 2, t))
    row_token = jnp.zeros((n_rows,), jnp.int32).at[pos.reshape(-1)].set(
        token.reshape(-1), unique_indices=True, mode="promise_in_bounds")
    tile_start = jnp.arange(n_rows // ROW_TILE) * ROW_TILE
    tile_expert = jnp.minimum(jnp.sum(tile_start[:, None] >= ends[None, :], axis=1), N_EXPERTS - 1).astype(jnp.int32)
    n_used = (ends[-1:] // ROW_TILE).astype(jnp.int32)
    ye = _experts(tile_expert, n_used, row_token, h.reshape(n_tok, d), wg, wu, wd, layer)
    nj = t // TOK_TILE
    pos_tiles = pos.reshape(b, 2, nj, TOK_TILE).transpose(0, 2, 1, 3).reshape(b * nj, 2, TOK_TILE // SUBLANES, SUBLANES)
    return _combine(pos_tiles, route, xa, mod_i, ye, n_lat_tiles, n_samples)


def _rope_tables(seq, ctx):
    rows = seq // GRID_W
    row = jnp.repeat(jnp.arange(rows), GRID_W).astype(F32)
    col = jnp.tile(jnp.arange(GRID_W), rows).astype(F32)
    half = DA_HEAD_DIM // 2
    inv = ROPE_BASE ** (-jnp.arange(0, half, 2, dtype=F32) / half)
    ang = jnp.concatenate([row[:, None] * inv, row[:, None] * inv, col[:, None] * inv, col[:, None] * inv], axis=1)
    sign = jnp.tile(jnp.concatenate([-jnp.ones((half // 2,), F32), jnp.ones((half // 2,), F32)]), 2)
    cos = jnp.concatenate([jnp.cos(ang), jnp.ones((ctx, DA_HEAD_DIM), F32)], axis=0)
    sin = jnp.concatenate([jnp.sin(ang) * sign, jnp.zeros((ctx, DA_HEAD_DIM), F32)], axis=0)
    return jnp.tile(cos, (1, LANES // DA_HEAD_DIM)), jnp.tile(sin, (1, LANES // DA_HEAD_DIM))


def _filter_weights(w1, b1, w2, b2, w3, freq):
    col = lambda v: v.reshape(-1, 1)
    bands = jnp.linspace(1e-4, HY_BANDS - 1, HY_BANDS, dtype=F32)
    deltas = jnp.abs(jnp.linspace(HY_MIN_DECAY, HY_MAX_DECAY, HY_WIDTH, dtype=F32))
    half = HY_ORDER * HY_WIDTH
    return (w1[0:1].T, w1[1:1 + HY_BANDS].T, w1[1 + HY_BANDS:].T, col(b1), w2.T, col(b2), col(freq),
            w3[:, :half].T, w3[:, half:].T, col(bands), col(jnp.tile(deltas, HY_ORDER)))


def kernel(x, c, ctx, c_ctx, w_mod, b_mod, norm1_g, norm2_g, w_in, hy_conv_w, hy_conv_b, hy_f_w1, hy_f_b1, hy_f_w2, hy_f_b2, hy_f_w3, hy_f_freq, hy_skip, q_norm_g, k_norm_g, lam_q1, lam_k1, lam_q2, lam_k2, subln_g, sg_norm_g, sg_w, sg_b, w_branch, w_out, ffn_wg, ffn_wu, ffn_wd, router_w, moe_wg, moe_wu, moe_wd):
    n_samples, seq, d = x.shape
    n_ctx = ctx.shape[1]
    depth = w_mod.shape[0]
    assert seq % CONV_TILE == 0 and n_ctx % CONV_TILE == 0 and seq % GRID_W == 0 and seq % n_ctx == 0
    assert n_samples + 1 <= MOD_ROWS
    n_lat_tiles = seq // TOK_TILE
    tot = seq + n_ctx

    xa = jnp.concatenate([x, ctx], axis=1)
    cond = jnp.zeros((MOD_ROWS, d), F32).at[:n_samples].set(c).at[n_samples].set(c_ctx)
    mod = _modvec(cond, w_mod, b_mod).reshape(depth, MOD_ROWS, 6, d)

    cos_t, sin_t = _rope_tables(seq, n_ctx)
    group = jnp.arange(QK_COLS) // DA_HEAD_DIM
    gmat = (group[:, None] == group[None, :]).astype(BF16)
    w_in_b = w_in.astype(BF16)
    w_br_b = w_branch.astype(BF16)
    w_out_b = w_out.astype(BF16)
    ffn_b = [w.astype(BF16) for w in (ffn_wg, ffn_wu, ffn_wd)]

    for i in range(depth):
        lam_init = 0.8 - 0.6 * math.exp(-0.3 * i)
        g1 = norm1_g[i].reshape(1, d)
        g2 = norm2_g[i].reshape(1, d)
        hy, q, k, v, y_sg = _inproj(
            xa, mod[i], g1, w_in_b, i, cos_t, sin_t,
            jnp.tile(q_norm_g[i], QK_COLS // DA_HEAD_DIM).reshape(1, QK_COLS),
            jnp.tile(k_norm_g[i], QK_COLS // DA_HEAD_DIM).reshape(1, QK_COLS),
            gmat, sg_norm_g[i].reshape(1, SG_WIDTH), sg_w[i].astype(BF16),
            jnp.repeat(sg_b[i].T, SG_WIDTH // SG_GROUPS, axis=1), n_lat_tiles, n_samples)

        fw = _filter_weights(hy_f_w1[i], hy_f_b1[i], hy_f_w2[i], hy_f_b2[i], hy_f_w3[i], hy_f_freq[i])
        hm_l = _hyfilter(fw, seq).reshape(HY_ORDER, HY_WIDTH, 2 * seq // LANES, LANES)
        hm_c = _hyfilter(fw, n_ctx).reshape(HY_ORDER, HY_WIDTH, 2 * n_ctx // LANES, LANES)
        cw = hy_conv_w[i].reshape(3, 3, HY_WIDTH)
        cbias = hy_conv_b[i].reshape(3, HY_WIDTH)
        sc_rows = [cw.reshape(9, HY_WIDTH), cbias, hy_skip[i]]
        sc_rows.append(jnp.zeros((HY_SCALARS - 12 - HY_ORDER, HY_WIDTH), F32))
        sc_tab = jnp.concatenate(sc_rows, axis=0).T.reshape(-1)
        y_hy = _hyconv(sc_tab, jnp.transpose(hy, (2, 0, 1)), hm_l, hm_c, seq, n_ctx)
        y_hy = jnp.transpose(y_hy, (1, 2, 0)).astype(BF16)

        lam_p = jnp.stack([lam_q1[i], lam_k1[i], lam_q2[i], lam_k2[i]], axis=0)
        y_da = _attention(lam_p, q, k, v, subln_g[i].reshape(1, DA_V_DIM), lam_init, seq)

        xa = _merge(xa, mod[i], g1, w_in_b, i, y_hy, y_da, y_sg, w_br_b, w_out_b,
                    seq if i == depth - 1 else tot, n_lat_tiles, n_samples)

        j = i // 2
        if i % 2 == 0:
            xa = _ffn_dense(xa, mod[i], g2, *ffn_b, j, n_lat_tiles, n_samples)
        else:
            xa = _moe(xa, mod[i], g2, router_w[j].T, moe_wg, moe_wu, moe_wd, j, n_lat_tiles, n_samples)
    return xa[:, :seq]
```
